```python
import numpy as np
import jax, jax.numpy as jnp
from jax import lax

D_MODEL = 1024
BATCH = 8
SEQ = 4096
DEPTH = 4

HEAD_DIM = 64
ROPE_THETA = 10000.0
NORM_EPS = 1e-6
BLOCK = 128
NEG = -1e30

A_HEADS = 4
A_PAIRS = ((128, 1), (512, 4), (2048, 16))

B_HEADS = 8
B_KV_GROUPS = 2
CMP_LEN = 32
CMP_STRIDE = 16
CMP_HIDDEN = 256
SLC_BLOCK = 64
N_SELECT = 16
WINDOW = 512
SLC_QCHUNK = 64
FORCE_BONUS = 1e4

C_HEADS = 4
Q_LORA = 256
KV_LORA = 128
QK_NOPE = 64
QK_ROPE = 32
V_DIM = 64

A_WIDTH = A_HEADS * HEAD_DIM
B_WIDTH = B_HEADS * HEAD_DIM
C_WIDTH = C_HEADS * V_DIM
MIX_WIDTH = A_WIDTH + B_WIDTH + C_WIDTH
B_KV_WIDTH = B_KV_GROUPS * HEAD_DIM
IN_SPLITS = (A_WIDTH, A_WIDTH, A_WIDTH, B_WIDTH, 6 * B_KV_WIDTH, 3 * B_HEADS, Q_LORA, KV_LORA, QK_ROPE)
IN_WIDTH = sum(IN_SPLITS)

N_EXPERTS = 32
TOP_K = 4
D_FF = 1024
SWIGLU_ALPHA = 1.702
SWIGLU_LIMIT = 7.0

kernel_name = "hybrid_dilated_nsa_mla_moe_adaln"

F32 = jnp.float32


def rms_norm(x, g):
    xf = x.astype(F32)
    y = xf * lax.rsqrt(jnp.mean(xf * xf, axis=-1, keepdims=True) + NORM_EPS)
    return (y * g.astype(F32)).astype(x.dtype)


def rope(t, pos):
    dim = t.shape[-1]
    half = dim // 2
    inv = jnp.power(ROPE_THETA, -jnp.arange(half, dtype=F32) * (2.0 / dim))
    ang = pos.astype(F32)[..., None] * inv
    cos = jnp.cos(ang)[:, :, None, :]
    sin = jnp.sin(ang)[:, :, None, :]
    t1 = t[..., :half].astype(F32)
    t2 = t[..., half:].astype(F32)
    return jnp.concatenate([t1 * cos - t2 * sin, t2 * cos + t1 * sin], axis=-1).astype(t.dtype)


def masked_softmax(s, mask):
    s = jnp.where(mask, s, NEG)
    m = jnp.max(s, axis=-1, keepdims=True)
    e = jnp.where(mask, jnp.exp(s - m), 0.0)
    den = jnp.sum(e, axis=-1, keepdims=True)
    return e / jnp.where(den > 0, den, 1.0), m, den


def banded_attention(q, k, v, max_dist, block):
    N, L, H, dh = q.shape
    G = k.shape[2]
    hpg = H // G
    nb = L // block
    n_prev = -(-max_dist // block)
    pad = n_prev * block
    kp = jnp.pad(k, ((0, 0), (pad, 0), (0, 0), (0, 0))).reshape(N, nb + n_prev, block, G, dh)
    vp = jnp.pad(v, ((0, 0), (pad, 0), (0, 0), (0, 0))).reshape(N, nb + n_prev, block, G, v.shape[-1])
    kb = jnp.concatenate([kp[:, j:j + nb] for j in range(n_prev + 1)], axis=2)
    vb = jnp.concatenate([vp[:, j:j + nb] for j in range(n_prev + 1)], axis=2)
    qb = q.reshape(N, nb, block, G, hpg, dh)
    s = jnp.einsum('nbqghd,nbkgd->nbghqk', qb, kb, preferred_element_type=F32) * (dh ** -0.5)
    n_keys = (n_prev + 1) * block
    qi = np.arange(block)[:, None] + pad
    ki = np.arange(n_keys)[None, :]
    dist = qi - ki
    band = (dist >= 0) & (dist <= max_dist)
    key_abs = np.arange(nb)[:, None] * block + np.arange(n_keys)[None, :] - pad
    mask = jnp.asarray(band[None] & (key_abs >= 0)[:, None, :])
    p, m, den = masked_softmax(s, mask[None, :, None, None])
    o = jnp.einsum('nbghqk,nbkgd->nbqghd', p.astype(v.dtype), vb).reshape(N, L, H, v.shape[-1])
    lse = (m + jnp.log(den))[..., 0].transpose(0, 1, 4, 2, 3).reshape(N, L, H)
    return o, lse


def dilated_attention(q, k, v):
    B, S, H, dh = q.shape
    outs, lses = [], []
    for window, d in A_PAIRS:
        L = S // d
        Lp = -(-L // BLOCK) * BLOCK

        def regroup(t):
            t = t.reshape(B, L, d, H, dh).transpose(0, 2, 1, 3, 4).reshape(B * d, L, H, dh)
            return jnp.pad(t, ((0, 0), (0, Lp - L), (0, 0), (0, 0)))

        o, lse = banded_attention(regroup(q), regroup(k), regroup(v), window // d, BLOCK)
        outs.append(o[:, :L].reshape(B, d, L, H, dh).transpose(0, 2, 1, 3, 4).reshape(B, S, H, dh))
        lses.append(lse[:, :L].reshape(B, d, L, H).transpose(0, 2, 1, 3).reshape(B, S, H))
    w = jax.nn.softmax(jnp.stack(lses, axis=0), axis=0)
    o = jnp.sum(w[..., None] * jnp.stack(outs, axis=0).astype(F32), axis=0)
    return o.astype(q.dtype)


def compress_blocks(t, cmp_idx, pos_emb, w1, b1, w2, b2):
    blk = t[:, cmp_idx] + pos_emb[None, :, None, :].astype(t.dtype)
    Bn, n, l, G, dh = blk.shape
    flat = blk.transpose(0, 1, 3, 2, 4).reshape(Bn, n, G, l * dh)
    return jax.nn.gelu(flat @ w1 + b1) @ w2 + b2


def native_sparse_attention(q, k_cmp, v_cmp, k_slc, v_slc, k_win, v_win, gate_logits, pos,
                            pos_k, pos_v, w1_k, b1_k, w2_k, b2_k, w1_v, b1_v, w2_v, b2_v):
    B, S, H, dh = q.shape
    G = B_KV_GROUPS
    hpg = H // G
    scale = dh ** -0.5
    q = rope(q, pos)
    qg = q.reshape(B, S, G, hpg, dh)
    t_idx = np.arange(S)

    n_cmp = (S - CMP_LEN) // CMP_STRIDE + 1
    cmp_idx = np.arange(n_cmp)[:, None] * CMP_STRIDE + np.arange(CMP_LEN)[None, :]
    cmp_end = cmp_idx[:, -1]
    kc = compress_blocks(k_cmp, cmp_idx, pos_k, w1_k, b1_k, w2_k, b2_k)
    kc = rope(kc, pos[:, cmp_end])
    vc = compress_blocks(v_cmp, cmp_idx, pos_v, w1_v, b1_v, w2_v, b2_v)
    s_cmp = jnp.einsum('bsghd,bcgd->bghsc', qg, kc, preferred_element_type=F32) * scale
    cmp_mask = jnp.asarray(cmp_end[None, :] <= t_idx[:, None])
    p_cmp, _, _ = masked_softmax(s_cmp, cmp_mask)
    o_cmp = jnp.einsum('bghsc,bcgd->bsghd', p_cmp.astype(vc.dtype), vc).reshape(B, S, H, dh)

    n_slc = S // SLC_BLOCK
    overlap = ((cmp_idx[:, :, None] // SLC_BLOCK) == np.arange(n_slc)[None, None, :]).mean(axis=1)
    imp = jnp.sum(p_cmp, axis=2) @ jnp.asarray(overlap.astype(np.float32))
    blk_j = np.arange(n_slc)[None, :]
    cur = (t_idx // SLC_BLOCK)[:, None]
    valid = jnp.asarray(blk_j * SLC_BLOCK <= t_idx[:, None])
    forced = jnp.asarray((blk_j == 0) | (blk_j == cur) | (blk_j == cur - 1))
    score = jnp.where(forced, imp + FORCE_BONUS, jnp.where(valid, imp, -1.0))
    n_sel = min(N_SELECT, n_slc)
    _, sel = lax.top_k(score, n_sel)

    k_slc = rope(k_slc, pos)
    kbt = k_slc.reshape(B, n_slc, SLC_BLOCK, G, dh).transpose(0, 3, 1, 2, 4)
    vbt = v_slc.reshape(B, n_slc, SLC_BLOCK, G, dh).transpose(0, 3, 1, 2, 4)
    nq = S // SLC_QCHUNK
    q_chunks = qg.reshape(B, nq, SLC_QCHUNK, G, hpg, dh).transpose(1, 0, 2, 3, 4, 5)
    sel_chunks = sel.reshape(B, G, nq, SLC_QCHUNK, n_sel).transpose(2, 0, 1, 3, 4)
    bi = jnp.arange(B)[:, None, None, None]
    gi = jnp.arange(G)[None, :, None, None]

    def select_step(args):
        qc, sc, i = args
        kg = kbt[bi, gi, sc].reshape(B, G, SLC_QCHUNK, n_sel * SLC_BLOCK, dh)
        vg = vbt[bi, gi, sc].reshape(B, G, SLC_QCHUNK, n_sel * SLC_BLOCK, dh)
        kpos = sc[..., None] * SLC_BLOCK + jnp.arange(SLC_BLOCK)
        qpos = i * SLC_QCHUNK + jnp.arange(SLC_QCHUNK)
        mask = (kpos <= qpos[None, None, :, None, None]).reshape(B, G, SLC_QCHUNK, n_sel * SLC_BLOCK)
        s = jnp.einsum('bqghd,bgqkd->bghqk', qc, kg, preferred_element_type=F32) * scale
        p, _, _ = masked_softmax(s, mask[:, :, None])
        return jnp.einsum('bghqk,bgqkd->bqghd', p.astype(vg.dtype), vg)

    o_slc = lax.map(select_step, (q_chunks, sel_chunks, jnp.arange(nq)))
    o_slc = o_slc.transpose(1, 0, 2, 3, 4, 5).reshape(B, S, H, dh)

    o_win, _ = banded_attention(q, rope(k_win, pos), v_win, WINDOW - 1, BLOCK)

    g = jax.nn.sigmoid(gate_logits.astype(F32)).reshape(B, S, H, 3, 1)
    o = g[..., 0, :] * o_cmp + g[..., 1, :] * o_slc + g[..., 2, :] * o_win
    return o.astype(q.dtype).reshape(B, S, H * dh)


def blocked_causal_attention(q, k, v):
    B, S, H, dq = q.shape
    nq = S // BLOCK
    q_chunks = q.reshape(B, nq, BLOCK, H, dq).transpose(1, 0, 2, 3, 4)
    kpos = jnp.arange(S)

    def step(args):
        qb, i = args
        s = jnp.einsum('bqhd,bkhd->bhqk', qb, k, preferred_element_type=F32) * (dq ** -0.5)
        qpos = i * BLOCK + jnp.arange(BLOCK)
        p, _, _ = masked_softmax(s, kpos[None, :] <= qpos[:, None])
        return jnp.einsum('bhqk,bkhd->bqhd', p.astype(v.dtype), v)

    o = lax.map(step, (q_chunks, jnp.arange(nq)))
    return o.transpose(1, 0, 2, 3, 4).reshape(B, S, H, v.shape[-1])


def latent_attention(c_q, c_kv, k_rope, pos, g_q, g_kv, w_q_up, w_kv_up):
    B, S, _ = c_q.shape
    q = (rms_norm(c_q, g_q) @ w_q_up).reshape(B, S, C_HEADS, QK_NOPE + QK_ROPE)
    q = jnp.concatenate([q[..., :QK_NOPE], rope(q[..., QK_NOPE:], pos)], axis=-1)
    kv = (rms_norm(c_kv, g_kv) @ w_kv_up).reshape(B, S, C_HEADS, QK_NOPE + V_DIM)
    k_nope, v = kv[..., :QK_NOPE], kv[..., QK_NOPE:]
    kr = rope(k_rope[:, :, None, :], pos)
    k = jnp.concatenate([k_nope, jnp.broadcast_to(kr, (B, S, C_HEADS, QK_ROPE))], axis=-1)
    return blocked_causal_attention(q, k, v).reshape(B, S, C_WIDTH)


def clamped_swiglu(u):
    x_glu = jnp.minimum(u[..., ::2], SWIGLU_LIMIT)
    x_lin = jnp.clip(u[..., 1::2], -SWIGLU_LIMIT, SWIGLU_LIMIT)
    return x_glu * jax.nn.sigmoid(SWIGLU_ALPHA * x_glu) * (x_lin + 1.0)


def moe(h, w_router, b_router, w1, b1, w2, b2):
    B, S, D = h.shape
    hf = h.reshape(B * S, D)
    logits = (hf @ w_router + b_router).astype(F32)
    top_vals, top_idx = lax.top_k(logits, TOP_K)
    gate = jax.nn.softmax(top_vals, axis=-1)
    cw = jnp.sum(jax.nn.one_hot(top_idx, N_EXPERTS, dtype=F32) * gate[..., None], axis=1)
    y = jnp.zeros((B * S, D), F32)
    for e in range(N_EXPERTS):
        out = clamped_swiglu(hf @ w1[e] + b1[e]) @ w2[e] + b2[e]
        y = y + cw[:, e:e + 1] * out.astype(F32)
    return y.astype(h.dtype).reshape(B, S, D)


def split_cols(t, sizes):
    parts, off = [], 0
    for n in sizes:
        parts.append(t[..., off:off + n])
        off += n
    return parts


def setup_inputs(seed: int = 0) -> dict:
    key = jax.random.key(seed)
    ks = iter(jax.random.split(key, 48))

    def nrm(shape, scale):
        return jax.random.normal(next(ks), shape, F32) * scale

    def gain(shape):
        return 1.0 + nrm(shape, 0.02)

    L, D = DEPTH, D_MODEL
    return {
        "x": nrm((BATCH, SEQ, D), 1.0),
        "c": nrm((BATCH, D), 1.0),
        "positions": jnp.arange(SEQ, dtype=jnp.int32)[None, :]
                     + jax.random.randint(next(ks), (BATCH, 1), 0, 1024, dtype=jnp.int32),
        "w_ada": nrm((L, D, 6 * D), 0.5 * D ** -0.5),
        "b_ada": nrm((L, 6 * D), 0.01),
        "g_norm1": gain((L, D)),
        "g_norm2": gain((L, D)),
        "w_in": nrm((L, D, IN_WIDTH), D ** -0.5),
        "nsa_pos_k": nrm((L, CMP_LEN, HEAD_DIM), 0.1),
        "nsa_pos_v": nrm((L, CMP_LEN, HEAD_DIM), 0.1),
        "nsa_w1_k": nrm((L, CMP_LEN * HEAD_DIM, CMP_HIDDEN), (CMP_LEN * HEAD_DIM) ** -0.5),
        "nsa_b1_k": nrm((L, CMP_HIDDEN), 0.01),
        "nsa_w2_k": nrm((L, CMP_HIDDEN, HEAD_DIM), CMP_HIDDEN ** -0.5),
        "nsa_b2_k": nrm((L, HEAD_DIM), 0.01),
        "nsa_w1_v": nrm((L, CMP_LEN * HEAD_DIM, CMP_HIDDEN), (CMP_LEN * HEAD_DIM) ** -0.5),
        "nsa_b1_v": nrm((L, CMP_HIDDEN), 0.01),
        "nsa_w2_v": nrm((L, CMP_HIDDEN, HEAD_DIM), CMP_HIDDEN ** -0.5),
        "nsa_b2_v": nrm((L, HEAD_DIM), 0.01),
        "mla_g_q": gain((L, Q_LORA)),
        "mla_g_kv": gain((L, KV_LORA)),
        "mla_w_q_up": nrm((L, Q_LORA, C_HEADS * (QK_NOPE + QK_ROPE)), Q_LORA ** -0.5),
        "mla_w_kv_up": nrm((L, KV_LORA, C_HEADS * (QK_NOPE + V_DIM)), KV_LORA ** -0.5),
        "g_out_a": gain((L, A_WIDTH)),
        "g_out_b": gain((L, B_WIDTH)),
        "g_out_c": gain((L, C_WIDTH)),
        "w_out": nrm((L, MIX_WIDTH, D), MIX_WIDTH ** -0.5),
        "w_router": nrm((L, D, N_EXPERTS), D ** -0.5),
        "b_router": nrm((L, N_EXPERTS), 0.01),
        "w_exp1": nrm((L, N_EXPERTS, D, 2 * D_FF), D ** -0.5),
        "b_exp1": nrm((L, N_EXPERTS, 2 * D_FF), 0.01),
        "w_exp2": nrm((L, N_EXPERTS, D_FF, D), D_FF ** -0.5),
        "b_exp2": nrm((L, N_EXPERTS, D), 0.01),
        "g_final": gain((D,)),
    }


def reference(x, c, positions, w_ada, b_ada, g_norm1, g_norm2, w_in,
              nsa_pos_k, nsa_pos_v, nsa_w1_k, nsa_b1_k, nsa_w2_k, nsa_b2_k,
              nsa_w1_v, nsa_b1_v, nsa_w2_v, nsa_b2_v,
              mla_g_q, mla_g_kv, mla_w_q_up, mla_w_kv_up,
              g_out_a, g_out_b, g_out_c, w_out,
              w_router, b_router, w_exp1, b_exp1, w_exp2, b_exp2, g_final):
    B, S, D = x.shape
    c_act = jax.nn.silu(c.astype(F32))
    for l in range(DEPTH):
        mods = (c_act @ w_ada[l].astype(F32) + b_ada[l].astype(F32)).reshape(B, 6, D)
        sh1, sc1, gt1, sh2, sc2, gt2 = [mods[:, i, None, :] for i in range(6)]

        h = (rms_norm(x, g_norm1[l]).astype(F32) * (1.0 + sc1) + sh1).astype(x.dtype)
        proj = h @ w_in[l]
        qa, ka, va, qb, kvb, gb, cq, ckv, kr = split_cols(proj, IN_SPLITS)

        qa = rope(qa.reshape(B, S, A_HEADS, HEAD_DIM), positions)
        ka = rope(ka.reshape(B, S, A_HEADS, HEAD_DIM), positions)
        oa = dilated_attention(qa, ka, va.reshape(B, S, A_HEADS, HEAD_DIM)).reshape(B, S, A_WIDTH)

        kvb = kvb.reshape(B, S, 6, B_KV_GROUPS, HEAD_DIM)
        ob = native_sparse_attention(
            qb.reshape(B, S, B_HEADS, HEAD_DIM),
            kvb[:, :, 0], kvb[:, :, 1], kvb[:, :, 2], kvb[:, :, 3], kvb[:, :, 4], kvb[:, :, 5],
            gb, positions,
            nsa_pos_k[l], nsa_pos_v[l], nsa_w1_k[l], nsa_b1_k[l], nsa_w2_k[l], nsa_b2_k[l],
            nsa_w1_v[l], nsa_b1_v[l], nsa_w2_v[l], nsa_b2_v[l])

        oc = latent_attention(cq, ckv, kr, positions, mla_g_q[l], mla_g_kv[l],
                              mla_w_q_up[l], mla_w_kv_up[l])

        mixed = jnp.concatenate([rms_norm(oa, g_out_a[l]), rms_norm(ob, g_out_b[l]),
                                 rms_norm(oc, g_out_c[l])], axis=-1) @ w_out[l]
        x = (x.astype(F32) + gt1 * mixed.astype(F32)).astype(x.dtype)

        h2 = (rms_norm(x, g_norm2[l]).astype(F32) * (1.0 + sc2) + sh2).astype(x.dtype)
        y = moe(h2, w_router[l], b_router[l], w_exp1[l], b_exp1[l], w_exp2[l], b_exp2[l])
        x = (x.astype(F32) + gt2 * y.astype(F32)).astype(x.dtype)
    return rms_norm(x, g_final)
```

```python
import functools

import numpy as np
import jax
import jax.numpy as jnp
from jax import lax
from jax.experimental import pallas as pl
from jax.experimental.pallas import tpu as pltpu

F32 = jnp.float32
BF16 = jnp.bfloat16
I32 = jnp.int32
U32 = jnp.uint32

LANES = 128
HEAD = 64
NEG = -1e30
LOWEST = -3.0e38
NORM_EPS = 1e-6
ROPE_THETA = 10000.0

A_HEADS, B_HEADS, B_GROUPS, C_HEADS = 4, 8, 2, 4
A_PAIRS = ((128, 1), (512, 4), (2048, 16))
A_W, B_W, C_W = 256, 512, 256
CMP_LEN, CMP_STRIDE, CMP_HIDDEN = 32, 16, 256
SLC_BLOCK, N_SELECT, WINDOW = 64, 16, 512
FORCE_BONUS = 1e4
Q_LORA, KV_LORA, QK_NOPE, QK_ROPE = 256, 128, 64, 32
N_EXPERTS, TOP_K, D_FF = 32, 4, 1024
SWIGLU_ALPHA, SWIGLU_LIMIT = 1.702, 7.0
IN_COLS = 2560
KR_LANE = 64

VMEM_LIMIT = 56 * 1024 * 1024


def _params(sem, vmem=VMEM_LIMIT):
    return pltpu.CompilerParams(dimension_semantics=sem, vmem_limit_bytes=vmem)


def _rms(x, g):
    return x * lax.rsqrt(jnp.mean(x * x, axis=-1, keepdims=True) + NORM_EPS) * g


def _rope_rows():
    inv64 = ROPE_THETA ** (-np.arange(32, dtype=np.float32) * (2.0 / 64))
    inv32 = ROPE_THETA ** (-np.arange(16, dtype=np.float32) * (2.0 / 32))
    inv_ab = np.tile(np.concatenate([inv64, inv64]), 2)
    sgn_ab = np.tile(np.concatenate([-np.ones(32), np.ones(32)]), 2)
    inv_c = np.zeros(LANES, np.float32)
    sgn_c = np.zeros(LANES, np.float32)
    inv_c[KR_LANE:KR_LANE + 16] = inv32
    inv_c[KR_LANE + 16:KR_LANE + 32] = inv32
    sgn_c[KR_LANE:KR_LANE + 16] = -1.0
    sgn_c[KR_LANE + 16:KR_LANE + 32] = 1.0
    rows = np.stack([inv_ab, sgn_ab, inv_c, sgn_c]).astype(np.float32)
    return jnp.asarray(rows)


def _rope_tab_kernel(pos_ref, rows_ref, cab_ref, sab_ref, cc_ref, sc_ref):
    pos = pos_ref[...].astype(F32)
    ang = pos * rows_ref[0:1, :]
    cab_ref[...] = jnp.cos(ang)
    sab_ref[...] = jnp.sin(ang) * rows_ref[1:2, :]
    ang = pos * rows_ref[2:3, :]
    cc_ref[...] = jnp.cos(ang)
    sc_ref[...] = jnp.sin(ang) * rows_ref[3:4, :]


def _rope_tables(pos_col):
    R = pos_col.shape[0]
    tm = min(R, 1024)
    spec = pl.BlockSpec((tm, LANES), lambda i: (i, 0))
    return pl.pallas_call(
        _rope_tab_kernel,
        out_shape=[jax.ShapeDtypeStruct((R, LANES), F32)] * 4,
        grid=(R // tm,),
        in_specs=[pl.BlockSpec((tm, 1), lambda i: (i, 0)), pl.BlockSpec((4, LANES), lambda i: (0, 0))],
        out_specs=[spec] * 4,
        compiler_params=_params(("parallel",)),
        name="rope_tables",
    )(pos_col, _rope_rows())


def _swap32(t, lane):
    return jnp.where((lane & 32) == 0, pltpu.roll(t, 96, 1), pltpu.roll(t, 32, 1))


def _swap16(t, lane):
    return jnp.where(lane < KR_LANE + 16, pltpu.roll(t, 112, 1), pltpu.roll(t, 16, 1))


def _mods_kernel(c_ref, w_ref, b_ref, o_ref):
    c = c_ref[...]
    ca = (c * jax.nn.sigmoid(c)).astype(BF16)
    o_ref[0] = jnp.dot(ca, w_ref[0].astype(BF16), preferred_element_type=F32) + b_ref[0]


def _ada_mods(c, w_ada, b_ada):
    L, D, N = w_ada.shape
    B = c.shape[0]
    tn = 1536
    return pl.pallas_call(
        _mods_kernel,
        out_shape=jax.ShapeDtypeStruct((L, B, N), F32),
        grid=(L, N // tn),
        in_specs=[pl.BlockSpec((B, D), lambda l, j: (0, 0)),
                  pl.BlockSpec((1, D, tn), lambda l, j: (l, 0, j)),
                  pl.BlockSpec((1, 1, tn), lambda l, j: (l, 0, j))],
        out_specs=pl.BlockSpec((1, B, tn), lambda l, j: (l, 0, j)),
        compiler_params=_params(("parallel", "parallel")),
        name="ada_mods",
    )(c, w_ada, b_ada.reshape(L, 1, N))


def _inproj_kernel(x_ref, mod_ref, g_ref, w_ref, cab_ref, sab_ref,
                   qa_ref, ka_ref, va_ref, qb_ref, kcmp_ref, vcmp_ref, kslc_ref, vslc_ref,
                   kwin_ref, vwin_ref, cq_ref, ckv_ref, misc_ref):
    tm = x_ref.shape[0]
    x = x_ref[...]
    h = (_rms(x, g_ref[...]) * (1.0 + mod_ref[0, 1:2, :]) + mod_ref[0, 0:1, :]).astype(BF16)
    cos = cab_ref[...]
    sin = sab_ref[...]
    lane = lax.broadcasted_iota(I32, (tm, LANES), 1)
    lo64 = lane < HEAD

    def proj(c0, n):
        return jnp.dot(h, w_ref[:, c0:c0 + n], preferred_element_type=F32)

    def rope(t):
        return t * cos + _swap32(t, lane) * sin

    def put_q(ref, h0, t, dst):
        t = rope(t) * (HEAD ** -0.5)
        tr = pltpu.roll(t, HEAD, 1)
        for i in range(2):
            src = t if dst[i] == i else tr
            ref[h0 + i] = jnp.where(lo64 if dst[i] == 0 else ~lo64, src, 0.0).astype(BF16)

    y = proj(0, 256)
    put_q(qa_ref, 0, y[:, :128], (0, 1))
    put_q(qa_ref, 2, y[:, 128:], (0, 1))
    y = proj(256, 256)
    ka_ref[:, 0:128] = rope(y[:, :128]).astype(BF16)
    ka_ref[:, 128:256] = rope(y[:, 128:]).astype(BF16)
    va_ref[...] = proj(512, 256).astype(BF16)
    for j in range(2):
        y = proj(768 + 256 * j, 256)
        put_q(qb_ref, 4 * j, y[:, :128], (j, j))
        put_q(qb_ref, 4 * j + 2, y[:, 128:], (j, j))
    y = proj(1280, 256)
    kcmp_ref[...] = y[:, :128]
    vcmp_ref[...] = y[:, 128:]
    y = proj(1536, 256)
    kslc_ref[...] = rope(y[:, :128]).astype(BF16)
    vslc_ref[...] = y[:, 128:].astype(BF16)
    y = proj(1792, 256)
    kwin_ref[...] = rope(y[:, :128]).astype(BF16)
    vwin_ref[...] = y[:, 128:].astype(BF16)
    cq_ref[...] = proj(2048, 256)
    y = proj(2304, 256)
    ckv_ref[...] = y[:, :128]
    misc_ref[...] = y[:, 128:]


def _in_proj(x2, mods_l, g1, w_in_r, cab, sab, S):
    T, D = x2.shape
    tm = min(512, S)
    spb = S // tm
    row = lambda i: (i, 0)
    shp = jax.ShapeDtypeStruct
    outs = [shp((A_HEADS, T, LANES), BF16), shp((T, A_W), BF16), shp((T, A_W), BF16),
            shp((B_HEADS, T, LANES), BF16),
            shp((T, LANES), F32), shp((T, LANES), F32), shp((T, LANES), BF16), shp((T, LANES), BF16),
            shp((T, LANES), BF16), shp((T, LANES), BF16),
            shp((T, Q_LORA), F32), shp((T, LANES), F32), shp((T, LANES), F32)]
    t128 = pl.BlockSpec((tm, LANES), row)
    t256 = pl.BlockSpec((tm, 256), row)
    out_specs = [pl.BlockSpec((A_HEADS, tm, LANES), lambda i: (0, i, 0)), t256, t256,
                 pl.BlockSpec((B_HEADS, tm, LANES), lambda i: (0, i, 0)),
                 t128, t128, t128, t128, t128, t128, t256, t128, t128]
    return pl.pallas_call(
        _inproj_kernel,
        out_shape=outs,
        grid=(T // tm,),
        in_specs=[pl.BlockSpec((tm, D), row),
                  pl.BlockSpec((1, 6, D), lambda i: (i // spb, 0, 0)),
                  pl.BlockSpec((1, D), lambda i: (0, 0)),
                  pl.BlockSpec((D, IN_COLS), lambda i: (0, 0)),
                  t128, t128],
        out_specs=out_specs,
        compiler_params=_params(("parallel",)),
        name="in_proj",
    )(x2, mods_l, g1, w_in_r, cab, sab)


def _band_kernel(*refs, nblk, tq, tk, max_dist, stacks, halves, with_lse, qi_axis):
    q_ref = refs[0]
    k_refs = refs[1:1 + nblk]
    v_refs = refs[1 + nblk:1 + 2 * nblk]
    o_ref = refs[1 + 2 * nblk]
    lse_ref = refs[2 + 2 * nblk] if with_lse else None
    qi = pl.program_id(qi_axis)
    qpos = qi * tq + lax.broadcasted_iota(I32, (tq, tk), 0)
    col = lax.broadcasted_iota(I32, (tq, tk), 1)
    lane = lax.broadcasted_iota(I32, (tq, LANES), 1)
    lo64 = lane < HEAD
    masks = []
    for i in range(nblk):
        kpos = (qi * (tq // tk) - (nblk - tq // tk) + i) * tk + col
        dist = qpos - kpos
        masks.append((dist >= 0) & (dist <= max_dist) & (kpos >= 0))
    outs = {}
    for (h0, n, blk) in stacks:
        lanes = slice(blk * LANES, (blk + 1) * LANES)
        q = q_ref[h0:h0 + n].reshape(n * tq, LANES)
        s_list = []
        for i in range(nblk):
            s = lax.dot_general(q, k_refs[i][:, lanes], (((1,), (1,)), ((), ())),
                                preferred_element_type=F32).reshape(n, tq, tk)
            s_list.append(jnp.where(masks[i][None], s, NEG))
        m = s_list[0].max(axis=-1, keepdims=True)
        for s in s_list[1:]:
            m = jnp.maximum(m, s.max(axis=-1, keepdims=True))
        l = jnp.zeros((n, tq, 1), F32)
        acc = jnp.zeros((n * tq, LANES), F32)
        for i in range(nblk):
            p = jnp.exp(s_list[i] - m)
            l = l + p.sum(axis=-1, keepdims=True)
            acc = acc + jnp.dot(p.astype(BF16).reshape(n * tq, tk), v_refs[i][:, lanes],
                                preferred_element_type=F32)
        acc = acc.reshape(n, tq, LANES) / l
        lse = m + jnp.log(l)
        for j in range(n):
            outs[h0 + j] = (acc[j], jnp.broadcast_to(lse[j], (tq, LANES)))
    nh = len(halves)
    for c in range(nh // 2):
        parts = []
        for h in (2 * c, 2 * c + 1):
            o, lse = outs[h]
            if halves[h] != h % 2:
                o = pltpu.roll(o, HEAD, 1)
            parts.append((o, lse))
        o_ref[:, c * LANES:(c + 1) * LANES] = jnp.where(lo64, parts[0][0], parts[1][0])
        if with_lse:
            lse_ref[:, c * LANES:(c + 1) * LANES] = jnp.where(lo64, parts[0][1], parts[1][1])


def _dilated_attention(qa, ka, va, B, S, d, max_dist):
    L = S // d
    tq = min(256, L)
    tk = 128
    nprev = -(-max_dist // tk)
    nblk = nprev + tq // tk
    q_v = qa.reshape(A_HEADS, B, L, d * LANES)
    k_v = ka.reshape(B, L, d * A_W)
    v_v = va.reshape(B, L, d * A_W)
    kv_specs = []
    for i in range(nblk):
        kv_specs.append(pl.BlockSpec(
            (None, tk, A_W),
            functools.partial(lambda b, r, qi, i: (b, jnp.maximum(qi * (tq // tk) - nprev + i, 0), r), i=i)))
    out_spec = pl.BlockSpec((None, tq, A_W), lambda b, r, qi: (b, qi, r))
    kern = functools.partial(_band_kernel, nblk=nblk, tq=tq, tk=tk, max_dist=max_dist,
                             stacks=((0, 2, 0), (2, 2, 1)), halves=(0, 1, 0, 1), with_lse=True, qi_axis=2)
    o, lse = pl.pallas_call(
        kern,
        out_shape=[jax.ShapeDtypeStruct((B, L, d * A_W), F32)] * 2,
        grid=(B, d, L // tq),
        in_specs=[pl.BlockSpec((A_HEADS, None, tq, LANES), lambda b, r, qi: (0, b, qi, r))]
                 + kv_specs + kv_specs,
        out_specs=[out_spec, out_spec],
        compiler_params=_params(("parallel", "parallel", "parallel")),
        name=f"dilated_attn_d{d}",
    )(q_v, *([k_v] * nblk), *([v_v] * nblk))
    return o.reshape(B * S, A_W), lse.reshape(B * S, A_W)


def _window_attention(qb, kwin, vwin, B, S):
    tq = min(256, S)
    tk = tq
    max_dist = WINDOW - 1
    nprev = -(-max_dist // tk)
    nblk = nprev + 1
    q_v = qb.reshape(B_HEADS, B, S, LANES)
    k_v = kwin.reshape(B, S, LANES)
    v_v = vwin.reshape(B, S, LANES)
    kv_specs = [pl.BlockSpec((None, tk, LANES),
                             functools.partial(lambda b, qi, i: (b, jnp.maximum(qi - nprev + i, 0), 0), i=i))
                for i in range(nblk)]
    kern = functools.partial(_band_kernel, nblk=nblk, tq=tq, tk=tk, max_dist=max_dist,
                             stacks=((0, B_HEADS, 0),), halves=(0, 0, 0, 0, 1, 1, 1, 1),
                             with_lse=False, qi_axis=1)
    o = pl.pallas_call(
        kern,
        out_shape=jax.ShapeDtypeStruct((B, S, B_W), F32),
        grid=(B, S // tq),
        in_specs=[pl.BlockSpec((B_HEADS, None, tq, LANES), lambda b, qi: (0, b, qi, 0))]
                 + kv_specs + kv_specs,
        out_specs=pl.BlockSpec((None, tq, B_W), lambda b, qi: (b, qi, 0)),
        compiler_params=_params(("parallel", "parallel")),
        name="window_attn",
    )(q_v, *([k_v] * nblk), *([v_v] * nblk))
    return o.reshape(B * S, B_W)


def _flash_kernel(qi_ref, kj_ref, first_ref, last_ref, *refs, tq, tk, stacks, halves, select):
    if select:
        q_ref, k_ref, v_ref, sel_ref, e_ref, o_ref, m_sc, l_sc, acc_sc = refs
    else:
        q_ref, k_ref, v_ref, o_ref, m_sc, l_sc, acc_sc = refs
    step = pl.program_id(1)
    qi = qi_ref[step]
    kj = kj_ref[step]

    @pl.when(first_ref[step] == 1)
    def _():
        m_sc[...] = jnp.full(m_sc.shape, NEG, F32)
        l_sc[...] = jnp.zeros(l_sc.shape, F32)
        acc_sc[...] = jnp.zeros(acc_sc.shape, F32)

    qpos = qi * tq + lax.broadcasted_iota(I32, (tq, tk), 0)
    kpos = kj * tk + lax.broadcasted_iota(I32, (tq, tk), 1)
    causal = kpos <= qpos
    if select:
        gmask = []
        for g in range(B_GROUPS):
            mf = lax.dot_general(sel_ref[g], e_ref[...], (((0,), (0,)), ((), ())),
                                 preferred_element_type=F32)
            gmask.append(causal & (mf > 0.5))
    for (h0, n, kb, vb) in stacks:
        q = q_ref[h0:h0 + n].reshape(n * tq, LANES)
        s = lax.dot_general(q, k_ref[:, kb * LANES:(kb + 1) * LANES], (((1,), (1,)), ((), ())),
                            preferred_element_type=F32).reshape(n, tq, tk)
        if select:
            hpg = n // B_GROUPS
            s = jnp.concatenate(
                [jnp.where(gmask[g][None], s[g * hpg:(g + 1) * hpg], NEG) for g in range(B_GROUPS)], axis=0)
        else:
            s = jnp.where(causal[None], s, NEG)
        m_prev = m_sc[h0:h0 + n]
        l_prev = l_sc[h0:h0 + n]
        m_next = jnp.maximum(m_prev, s.max(axis=-1, keepdims=True))
        alpha = jnp.exp(m_prev - m_next)
        p = jnp.exp(s - jnp.concatenate([m_next] * (tk // LANES), axis=-1))
        l_sc[h0:h0 + n] = alpha * l_prev + p.sum(axis=-1, keepdims=True)
        m_sc[h0:h0 + n] = m_next
        pv = jnp.dot(p.astype(BF16).reshape(n * tq, tk), v_ref[:, vb * LANES:(vb + 1) * LANES],
                     preferred_element_type=F32).reshape(n, tq, LANES)
        acc_sc[h0:h0 + n] = acc_sc[h0:h0 + n] * alpha + pv

    @pl.when(last_ref[step] == 1)
    def _():
        lane = lax.broadcasted_iota(I32, (tq, LANES), 1)
        lo64 = lane < HEAD
        nh = len(halves)
        for c in range(nh // 2):
            parts = []
            for h in (2 * c, 2 * c + 1):
                o = acc_sc[h] / l_sc[h]
                if halves[h] != h % 2:
                    o = pltpu.roll(o, HEAD, 1)
                parts.append(o)
            o_ref[:, c * LANES:(c + 1) * LANES] = jnp.where(lo64, parts[0], parts[1])


def _causal_schedule(S, tq, tk):
    qi, kj, first, last = [], [], [], []
    for i in range(S // tq):
        nk = (i * tq + tq - 1) // tk + 1
        for j in range(nk):
            qi.append(i)
            kj.append(j)
            first.append(1 if j == 0 else 0)
            last.append(1 if j == nk - 1 else 0)
    return [jnp.asarray(np.asarray(a, np.int32)) for a in (qi, kj, first, last)]


def _select_attention(qb, kslc, vslc, sel_t, B, S):
    tq = min(256, S)
    tk = tq
    n_slc = S // SLC_BLOCK
    tabs = _causal_schedule(S, tq, tk)
    nsteps = tabs[0].shape[0]
    expand = jnp.asarray((np.arange(n_slc)[:, None] == (np.arange(S)[None, :] // SLC_BLOCK)), BF16)
    kern = functools.partial(_flash_kernel, tq=tq, tk=tk, stacks=((0, B_HEADS, 0, 0),),
                             halves=(0, 0, 0, 0, 1, 1, 1, 1), select=True)
    o = pl.pallas_call(
        kern,
        out_shape=jax.ShapeDtypeStruct((B, S, B_W), F32),
        grid_spec=pltpu.PrefetchScalarGridSpec(
            num_scalar_prefetch=4,
            grid=(B, nsteps),
            in_specs=[pl.BlockSpec((B_HEADS, None, tq, LANES), lambda b, s, qi, kj, f, l: (0, b, qi[s], 0)),
                      pl.BlockSpec((None, tk, LANES), lambda b, s, qi, kj, f, l: (b, kj[s], 0)),
                      pl.BlockSpec((None, tk, LANES), lambda b, s, qi, kj, f, l: (b, kj[s], 0)),
                      pl.BlockSpec((None, B_GROUPS, n_slc, tq), lambda b, s, qi, kj, f, l: (b, 0, 0, qi[s])),
                      pl.BlockSpec((n_slc, tk), lambda b, s, qi, kj, f, l: (0, kj[s]))],
            out_specs=pl.BlockSpec((None, tq, B_W), lambda b, s, qi, kj, f, l: (b, qi[s], 0)),
            scratch_shapes=[pltpu.VMEM((B_HEADS, tq, LANES), F32)] * 3),
        compiler_params=_params(("parallel", "arbitrary")),
        name="select_attn",
    )(*tabs, qb.reshape(B_HEADS, B, S, LANES), kslc.reshape(B, S, LANES), vslc.reshape(B, S, LANES),
      sel_t, expand)
    return o.reshape(B * S, B_W)


def _latent_attention(qc, kc, vc, B, S):
    tq = min(256, S)
    tk = min(512, S)
    tabs = _causal_schedule(S, tq, tk)
    nsteps = tabs[0].shape[0]
    kern = functools.partial(_flash_kernel, tq=tq, tk=tk,
                             stacks=tuple((h, 1, h, h // 2) for h in range(C_HEADS)),
                             halves=(0, 1, 0, 1), select=False)
    o = pl.pallas_call(
        kern,
        out_shape=jax.ShapeDtypeStruct((B, S, C_W), F32),
        grid_spec=pltpu.PrefetchScalarGridSpec(
            num_scalar_prefetch=4,
            grid=(B, nsteps),
            in_specs=[pl.BlockSpec((C_HEADS, None, tq, LANES), lambda b, s, qi, kj, f, l: (0, b, qi[s], 0)),
                      pl.BlockSpec((None, tk, C_HEADS * LANES), lambda b, s, qi, kj, f, l: (b, kj[s], 0)),
                      pl.BlockSpec((None, tk, C_W), lambda b, s, qi, kj, f, l: (b, kj[s], 0))],
            out_specs=pl.BlockSpec((None, tq, C_W), lambda b, s, qi, kj, f, l: (b, qi[s], 0)),
            scratch_shapes=[pltpu.VMEM((C_HEADS, tq, LANES), F32)] * 3),
        compiler_params=_params(("parallel", "arbitrary")),
        name="latent_attn",
    )(*tabs, qc.reshape(C_HEADS, B, S, LANES), kc.reshape(B, S, C_HEADS * LANES), vc.reshape(B, S, C_W))
    return o.reshape(B * S, C_W)


def _gelu_tanh(x):
    return 0.5 * x * (1.0 + jnp.tanh(0.7978845608028654 * (x + 0.044715 * x * x * x)))


def _compress_kernel(u_ref, pos_ref, w1_ref, b1_ref, w2_ref, b2_ref, cos_ref, sin_ref, o_ref, *, rotate):
    n = u_ref.shape[0]
    half = CMP_STRIDE * HEAD
    u = u_ref[...]
    ua = (u + pos_ref[:, :half]).astype(BF16)
    ub = (u + pos_ref[:, half:]).astype(BF16)
    pa = jnp.dot(ua, w1_ref[:half, :].astype(BF16), preferred_element_type=F32)
    pb = jnp.dot(ub, w1_ref[half:, :].astype(BF16), preferred_element_type=F32)
    hid = _gelu_tanh(pa + pltpu.roll(pb, n - 1, 0) + b1_ref[...])
    y = jnp.dot(hid.astype(BF16), w2_ref[...].astype(BF16), preferred_element_type=F32) + b2_ref[...]
    if rotate:
        lane = lax.broadcasted_iota(I32, y.shape, 1)
        y = y * cos_ref[...] + _swap32(y, lane) * sin_ref[...]
    o_ref[...] = y


def _compress(t, pos_emb, w1, b1, w2, b2, cos_e, sin_e, B, S, rotate):
    nch = S // CMP_STRIDE
    u = t.reshape(B, S, B_GROUPS, HEAD).transpose(0, 2, 1, 3).reshape(B, B_GROUPS, nch, CMP_STRIDE * HEAD)
    w2d = jnp.concatenate([w2, w2], axis=1)
    b2d = jnp.concatenate([b2, b2]).reshape(1, LANES)
    kern = functools.partial(_compress_kernel, rotate=rotate)
    tab = pl.BlockSpec((None, nch, LANES), lambda b, g: (b, 0, 0))
    return pl.pallas_call(
        kern,
        out_shape=jax.ShapeDtypeStruct((B, B_GROUPS, nch, LANES), F32),
        grid=(B, B_GROUPS),
        in_specs=[pl.BlockSpec((None, None, nch, CMP_STRIDE * HEAD), lambda b, g: (b, g, 0, 0)),
                  pl.BlockSpec((1, CMP_LEN * HEAD), lambda b, g: (0, 0)),
                  pl.BlockSpec((CMP_LEN * HEAD, CMP_HIDDEN), lambda b, g: (0, 0)),
                  pl.BlockSpec((1, CMP_HIDDEN), lambda b, g: (0, 0)),
                  pl.BlockSpec((CMP_HIDDEN, LANES), lambda b, g: (0, 0)),
                  pl.BlockSpec((1, LANES), lambda b, g: (0, 0)),
                  tab, tab],
        out_specs=pl.BlockSpec((None, None, nch, LANES), lambda b, g: (b, g, 0, 0)),
        compiler_params=_params(("parallel", "parallel")),
        name="nsa_compress",
    )(u, pos_emb.reshape(1, CMP_LEN * HEAD), w1, b1.reshape(1, CMP_HIDDEN), w2d, b2d, cos_e, sin_e)


def _cmp_attn_kernel(q_ref, kc_ref, vc_ref, ov_ref, o_ref, sel_ref, *, tq, n_cmp, n_slc):
    qi = pl.program_id(1)
    nk = kc_ref.shape[0]
    q = q_ref[...].reshape(B_HEADS * tq, LANES)
    s = lax.dot_general(q, kc_ref[...], (((1,), (1,)), ((), ())),
                        preferred_element_type=F32).reshape(B_HEADS, tq, nk)
    t_idx = qi * tq + lax.broadcasted_iota(I32, (tq, nk), 0)
    c_idx = lax.broadcasted_iota(I32, (tq, nk), 1)
    mask = ((c_idx * CMP_STRIDE + (CMP_LEN - 1)) <= t_idx) & (c_idx < n_cmp)
    s = jnp.where(mask[None], s, NEG)
    m = s.max(axis=-1, keepdims=True)
    e = jnp.where(mask[None], jnp.exp(s - m), 0.0)
    den = e.sum(axis=-1, keepdims=True)
    p = e / jnp.where(den > 0, den, 1.0)
    o = jnp.dot(p.astype(BF16).reshape(B_HEADS * tq, nk), vc_ref[...],
                preferred_element_type=F32).reshape(B_HEADS, tq, LANES)
    lane = lax.broadcasted_iota(I32, (tq, LANES), 1)
    lo64 = lane < HEAD
    hpg = B_HEADS // B_GROUPS
    for c in range(B_HEADS // 2):
        parts = []
        for h in (2 * c, 2 * c + 1):
            oh = o[h]
            if h // hpg != h % 2:
                oh = pltpu.roll(oh, HEAD, 1)
            parts.append(oh)
        o_ref[:, c * LANES:(c + 1) * LANES] = jnp.where(lo64, parts[0], parts[1])

    j_idx = lax.broadcasted_iota(I32, (n_slc, tq), 0)
    t_col = qi * tq + lax.broadcasted_iota(I32, (n_slc, tq), 1)
    cur = t_col // SLC_BLOCK
    forced = (j_idx == 0) | (j_idx == cur) | (j_idx == cur - 1)
    valid = j_idx * SLC_BLOCK <= t_col
    nt = (((1,), (1,)), ((), ()))
    for g in range(B_GROUPS):
        ps = p[g * hpg]
        for h in range(1, hpg):
            ps = ps + p[g * hpg + h]
        hi = ps.astype(BF16)
        lo = (ps - hi.astype(F32)).astype(BF16)
        imp = (lax.dot_general(ov_ref[...], hi, nt, preferred_element_type=F32)
               + lax.dot_general(ov_ref[...], lo, nt, preferred_element_type=F32))
        score = jnp.where(forced, imp + FORCE_BONUS, jnp.where(valid, imp, -1.0))
        sel = jnp.zeros((n_slc, tq), F32)
        j_f = j_idx.astype(F32)
        for _ in range(min(N_SELECT, n_slc)):
            mx = score.max(axis=0, keepdims=True)
            idx = jnp.where(score == mx, j_f, float(n_slc)).min(axis=0, keepdims=True)
            pick = j_f == idx
            sel = jnp.where(pick, 1.0, sel)
            score = jnp.where(pick, LOWEST, score)
        sel_ref[g] = sel.astype(BF16)


def _cmp_attention(qb, kcc, vcc, B, S):
    tq = min(256, S)
    nk = S // CMP_STRIDE
    n_cmp = (S - CMP_LEN) // CMP_STRIDE + 1
    n_slc = S // SLC_BLOCK
    cmp_tok = np.arange(n_cmp)[:, None] * CMP_STRIDE + np.arange(CMP_LEN)[None, :]
    overlap = ((cmp_tok[:, :, None] // SLC_BLOCK) == np.arange(n_slc)[None, None, :]).mean(axis=1)
    ov_t = np.zeros((n_slc, nk), np.float32)
    ov_t[:, :n_cmp] = overlap.T
    kern = functools.partial(_cmp_attn_kernel, tq=tq, n_cmp=n_cmp, n_slc=n_slc)
    o, sel = pl.pallas_call(
        kern,
        out_shape=[jax.ShapeDtypeStruct((B, S, B_W), F32),
                   jax.ShapeDtypeStruct((B, B_GROUPS, n_slc, S), BF16)],
        grid=(B, S // tq),
        in_specs=[pl.BlockSpec((B_HEADS, None, tq, LANES), lambda b, qi: (0, b, qi, 0)),
                  pl.BlockSpec((None, nk, LANES), lambda b, qi: (b, 0, 0)),
                  pl.BlockSpec((None, nk, LANES), lambda b, qi: (b, 0, 0)),
                  pl.BlockSpec((n_slc, nk), lambda b, qi: (0, 0))],
        out_specs=[pl.BlockSpec((None, tq, B_W), lambda b, qi: (b, qi, 0)),
                   pl.BlockSpec((None, B_GROUPS, n_slc, tq), lambda b, qi: (b, 0, 0, qi))],
        compiler_params=_params(("parallel", "parallel")),
        name="cmp_attn",
    )(qb.reshape(B_HEADS, B, S, LANES), kcc, vcc, jnp.asarray(ov_t, BF16))
    return o.reshape(B * S, B_W), sel


def _mla_prep_kernel(cq_ref, ckv_ref, misc_ref, gq_ref, gkv_ref, wq_ref, wk_ref, wv_ref, cc_ref, sc_ref,
                     q_ref, k_ref, v_ref):
    tm = cq_ref.shape[0]
    lane = lax.broadcasted_iota(I32, (tm, LANES), 1)
    cos = cc_ref[...]
    sin = sc_ref[...]

    def rope(t):
        return t * cos + _swap16(t, lane) * sin

    nq = _rms(cq_ref[...], gq_ref[...]).astype(BF16)
    q = jnp.dot(nq, wq_ref[...], preferred_element_type=F32)
    scale = (QK_NOPE + QK_ROPE) ** -0.5
    for h in range(C_HEADS):
        q_ref[h] = (rope(q[:, h * LANES:(h + 1) * LANES]) * scale).astype(BF16)
    nkv = _rms(ckv_ref[...], gkv_ref[...]).astype(BF16)
    k = jnp.dot(nkv, wk_ref[...], preferred_element_type=F32)
    in_rope = (lane >= KR_LANE) & (lane < KR_LANE + QK_ROPE)
    kr = jnp.where(in_rope, rope(misc_ref[...]), 0.0)
    for h in range(C_HEADS):
        k_ref[:, h * LANES:(h + 1) * LANES] = (k[:, h * LANES:(h + 1) * LANES] + kr).astype(BF16)
    v_ref[...] = jnp.dot(nkv, wv_ref[...], preferred_element_type=F32).astype(BF16)


def _mla_prep(cq, ckv, misc, g_q, g_kv, wq_p, wk_p, wv_p, cc, sc):
    T = cq.shape[0]
    tm = min(512, T)
    row = lambda i: (i, 0)
    fix = lambda i: (0, 0)
    t128 = pl.BlockSpec((tm, LANES), row)
    return pl.pallas_call(
        _mla_prep_kernel,
        out_shape=[jax.ShapeDtypeStruct((C_HEADS, T, LANES), BF16),
                   jax.ShapeDtypeStruct((T, C_HEADS * LANES), BF16),
                   jax.ShapeDtypeStruct((T, C_W), BF16)],
        grid=(T // tm,),
        in_specs=[pl.BlockSpec((tm, Q_LORA), row), t128, t128,
                  pl.BlockSpec((1, Q_LORA), fix), pl.BlockSpec((1, KV_LORA), fix),
                  pl.BlockSpec((Q_LORA, C_HEADS * LANES), fix),
                  pl.BlockSpec((KV_LORA, C_HEADS * LANES), fix),
                  pl.BlockSpec((KV_LORA, C_W), fix), t128, t128],
        out_specs=[pl.BlockSpec((C_HEADS, tm, LANES), lambda i: (0, i, 0)),
                   pl.BlockSpec((tm, C_HEADS * LANES), row),
                   pl.BlockSpec((tm, C_W), row)],
        compiler_params=_params(("parallel",)),
        name="mla_prep",
    )(cq, ckv, misc, g_q, g_kv, wq_p, wk_p, wv_p, cc, sc)


def _outproj_kernel(x_ref, mod_ref, o1_ref, o2_ref, o3_ref, l1_ref, l2_ref, l3_ref,
                    ocmp_ref, oslc_ref, owin_ref, misc_ref, ge_ref, oc_ref,
                    ga_ref, gb_ref, gc_ref, w_ref, out_ref):
    l1, l2, l3 = l1_ref[...], l2_ref[...], l3_ref[...]
    mx = jnp.maximum(jnp.maximum(l1, l2), l3)
    e1, e2, e3 = jnp.exp(l1 - mx), jnp.exp(l2 - mx), jnp.exp(l3 - mx)
    oa = (e1 * o1_ref[...] + e2 * o2_ref[...] + e3 * o3_ref[...]) / (e1 + e2 + e3)

    sg = jax.nn.sigmoid(misc_ref[...])
    hi = sg.astype(BF16)
    lo = (sg - hi.astype(F32)).astype(BF16)

    def gate(br):
        return (jnp.dot(hi, ge_ref[br], preferred_element_type=F32)
                + jnp.dot(lo, ge_ref[br], preferred_element_type=F32))

    ob = gate(0) * ocmp_ref[...] + gate(1) * oslc_ref[...] + gate(2) * owin_ref[...]
    na = _rms(oa, ga_ref[...]).astype(BF16)
    nb = _rms(ob, gb_ref[...]).astype(BF16)
    nc = _rms(oc_ref[...], gc_ref[...]).astype(BF16)
    mixed = (jnp.dot(na, w_ref[0:A_W, :], preferred_element_type=F32)
             + jnp.dot(nb, w_ref[A_W:A_W + B_W, :], preferred_element_type=F32)
             + jnp.dot(nc, w_ref[A_W + B_W:, :], preferred_element_type=F32))
    out_ref[...] = x_ref[...] + mod_ref[0, 2:3, :] * mixed


def _gate_expand():
    ge = np.zeros((3, LANES, B_W), np.float32)
    for h in range(B_HEADS):
        for br in range(3):
            ge[br, h * 3 + br, h * HEAD:(h + 1) * HEAD] = 1.0
    return jnp.asarray(ge, BF16)


def _out_proj(x2, mods_l, oa_parts, lse_parts, ocmp, oslc, owin, misc, oc, g_a, g_b, g_c, w_out_b, S):
    T, D = x2.shape
    tm = min(256, S)
    spb = S // tm
    row = lambda i: (i, 0)
    fix = lambda i: (0, 0)
    t128 = pl.BlockSpec((tm, LANES), row)
    t256 = pl.BlockSpec((tm, 256), row)
    t512 = pl.BlockSpec((tm, 512), row)
    return pl.pallas_call(
        _outproj_kernel,
        out_shape=jax.ShapeDtypeStruct((T, D), F32),
        grid=(T // tm,),
        in_specs=[pl.BlockSpec((tm, D), row),
                  pl.BlockSpec((1, 6, D), lambda i: (i // spb, 0, 0)),
                  t256, t256, t256, t256, t256, t256, t512, t512, t512, t128,
                  pl.BlockSpec((3, LANES, B_W), lambda i: (0, 0, 0)),
                  t256,
                  pl.BlockSpec((1, A_W), fix), pl.BlockSpec((1, B_W), fix), pl.BlockSpec((1, C_W), fix),
                  pl.BlockSpec((D, D), fix)],
        out_specs=pl.BlockSpec((tm, D), row),
        compiler_params=_params(("parallel",)),
        name="out_proj",
    )(x2, mods_l, *oa_parts, *lse_parts, ocmp, oslc, owin, misc, _gate_expand(), oc, g_a, g_b, g_c, w_out_b)


def _route_kernel(x_ref, mod_ref, g_ref, wh_ref, wl_ref, b_ref, hp_ref, e_ref, r_ref, gt_ref, cnt_ref, carry):
    tm = x_ref.shape[0]

    @pl.when(pl.program_id(0) == 0)
    def _():
        carry[...] = jnp.zeros(carry.shape, F32)

    h = _rms(x_ref[...], g_ref[...]) * (1.0 + mod_ref[0, 4:5, :]) + mod_ref[0, 3:4, :]
    hb = h.astype(BF16)
    hlo = (h - hb.astype(F32)).astype(BF16)
    bits = lax.bitcast_convert_type(hb.astype(F32), U32)
    half = h.shape[1] // 2
    hp_ref[...] = (bits[:, half:] & jnp.uint32(0xFFFF0000)) | (bits[:, :half] >> 16)

    logits = (jnp.dot(hb, wh_ref[...], preferred_element_type=F32)
              + jnp.dot(hlo, wh_ref[...], preferred_element_type=F32)
              + jnp.dot(hb, wl_ref[...], preferred_element_type=F32)) + b_ref[...]
    lane = lax.broadcasted_iota(I32, (tm, LANES), 1)
    lane_f = lane.astype(F32)
    vals, idxs, picks = [], [], []
    lg = logits
    for _ in range(TOP_K):
        mx = lg.max(axis=-1, keepdims=True)
        idx = jnp.where(lg == mx, lane_f, float(LANES)).min(axis=-1, keepdims=True)
        pick = lane_f == idx
        vals.append(mx)
        idxs.append(idx.astype(I32))
        picks.append(pick)
        lg = jnp.where(pick, LOWEST, lg)
    es = [jnp.exp(v - vals[0]) for v in vals]
    den = es[0] + es[1] + es[2] + es[3]
    onehot = jnp.zeros((tm, LANES), F32)
    for pk in picks:
        onehot = jnp.where(pk, 1.0, onehot)
    tri = (lax.broadcasted_iota(I32, (tm, tm), 0) >= lax.broadcasted_iota(I32, (tm, tm), 1)).astype(BF16)
    cs = jnp.dot(tri, onehot.astype(BF16), preferred_element_type=F32)
    rank = carry[...] + cs - 1.0
    e_out = jnp.zeros((tm, LANES), I32)
    r_out = jnp.zeros((tm, LANES), I32)
    g_out = jnp.zeros((tm, LANES), F32)
    for k in range(TOP_K):
        rk = jnp.where(picks[k], rank, 0.0).sum(axis=-1, keepdims=True)
        e_out = jnp.where(lane == k, idxs[k], e_out)
        r_out = jnp.where(lane == k, rk.astype(I32), r_out)
        g_out = jnp.where(lane == k, es[k] / den, g_out)
    e_ref[...] = e_out
    r_ref[...] = r_out
    gt_ref[...] = g_out
    new = carry[...] + onehot.sum(axis=0, keepdims=True)
    carry[...] = new
    cnt_ref[...] = new


def _route(x2, mods_l, g2, wr_hi, wr_lo, br_p, S):
    T, D = x2.shape
    tm = min(256, S)
    spb = S // tm
    row = lambda i: (i, 0)
    fix = lambda i: (0, 0)
    t128 = pl.BlockSpec((tm, LANES), row)
    return pl.pallas_call(
        _route_kernel,
        out_shape=[jax.ShapeDtypeStruct((T, D // 2), U32),
                   jax.ShapeDtypeStruct((T, LANES), I32), jax.ShapeDtypeStruct((T, LANES), I32),
                   jax.ShapeDtypeStruct((T, LANES), F32), jax.ShapeDtypeStruct((1, LANES), F32)],
        grid=(T // tm,),
        in_specs=[pl.BlockSpec((tm, D), row),
                  pl.BlockSpec((1, 6, D), lambda i: (i // spb, 0, 0)),
                  pl.BlockSpec((1, D), fix),
                  pl.BlockSpec((D, LANES), fix), pl.BlockSpec((D, LANES), fix), pl.BlockSpec((1, LANES), fix)],
        out_specs=[pl.BlockSpec((tm, D // 2), row), t128, t128, t128, pl.BlockSpec((1, LANES), fix)],
        scratch_shapes=[pltpu.VMEM((1, LANES), F32)],
        compiler_params=_params(("arbitrary",)),
        name="moe_route",
    )(x2, mods_l, g2, wr_hi, wr_lo, br_p)


def _dispatch_kernel(dest_ref, hp_ref, xs_in_ref, xs_ref, d_s, sem_i, sem):
    del xs_in_ref
    td = hp_ref.shape[0]
    cp = pltpu.make_async_copy(dest_ref.at[0, 0], d_s, sem_i)
    cp.start()
    cp.wait()

    def body(i, c):
        t = i // TOP_K
        pltpu.make_async_copy(hp_ref.at[pl.ds(t, 1)], xs_ref.at[pl.ds(d_s[i], 1)], sem).start()
        return c

    lax.fori_loop(0, TOP_K * td, body, 0, unroll=8)
    for _ in range(TOP_K):
        pltpu.make_async_copy(hp_ref, xs_ref.at[pl.ds(0, td)], sem).wait()


def _dispatch(dest3, hp, n_rows):
    T, W = hp.shape
    td = dest3.shape[2] // TOP_K
    xs0 = jnp.zeros((n_rows, W), U32)
    return pl.pallas_call(
        _dispatch_kernel,
        out_shape=jax.ShapeDtypeStruct((n_rows, W), U32),
        grid=(T // td,),
        in_specs=[pl.BlockSpec((1, 1, TOP_K * td), lambda i: (i, 0, 0)),
                  pl.BlockSpec((td, W), lambda i: (i, 0)),
                  pl.BlockSpec(memory_space=pl.ANY)],
        out_specs=pl.BlockSpec(memory_space=pl.ANY),
        scratch_shapes=[pltpu.SMEM((TOP_K * td,), I32), pltpu.SemaphoreType.DMA, pltpu.SemaphoreType.DMA],
        input_output_aliases={2: 0},
        compiler_params=_params(("arbitrary",)),
        name="moe_dispatch",
    )(dest3, hp, xs0)


def _ffn_kernel(te_ref, nv_ref, x_ref, w1_ref, b1_ref, w2_ref, b2_ref, y_ref):
    @pl.when(pl.program_id(0) >= nv_ref[0])
    def _():
        y_ref[...] = jnp.zeros(y_ref.shape, F32)

    @pl.when(pl.program_id(0) < nv_ref[0])
    def _():
        xw = x_ref[...]
        lo = lax.bitcast_convert_type(xw << 16, F32)
        hi = lax.bitcast_convert_type(xw & jnp.uint32(0xFFFF0000), F32)
        xb = jnp.concatenate([lo, hi], axis=1).astype(BF16)
        u = jnp.dot(xb, w1_ref[0], preferred_element_type=F32) + b1_ref[0]
        glu = jnp.minimum(u[:, :D_FF], SWIGLU_LIMIT)
        lin = jnp.clip(u[:, D_FF:], -SWIGLU_LIMIT, SWIGLU_LIMIT)
        a = glu * jax.nn.sigmoid(SWIGLU_ALPHA * glu) * (lin + 1.0)
        y_ref[...] = jnp.dot(a.astype(BF16), w2_ref[0], preferred_element_type=F32) + b2_ref[0]


def _expert_ffn(tile_e, n_valid, xs, w1p, b1p, w2b, b2, tmf):
    n_rows, W = xs.shape
    D = 2 * W
    ntiles = n_rows // tmf
    return pl.pallas_call(
        _ffn_kernel,
        out_shape=jax.ShapeDtypeStruct((n_rows, D), F32),
        grid_spec=pltpu.PrefetchScalarGridSpec(
            num_scalar_prefetch=2,
            grid=(ntiles,),
            in_specs=[pl.BlockSpec((tmf, W), lambda j, te, nv: (jnp.minimum(j, nv[0] - 1), 0)),
                      pl.BlockSpec((1, D, 2 * D_FF), lambda j, te, nv: (te[j], 0, 0)),
                      pl.BlockSpec((1, 1, 2 * D_FF), lambda j, te, nv: (te[j], 0, 0)),
                      pl.BlockSpec((1, D_FF, D), lambda j, te, nv: (te[j], 0, 0)),
                      pl.BlockSpec((1, 1, D), lambda j, te, nv: (te[j], 0, 0))],
            out_specs=pl.BlockSpec((tmf, D), lambda j, te, nv: (j, 0))),
        compiler_params=_params(("arbitrary",)),
        name="moe_ffn",
    )(tile_e, n_valid, xs, w1p, b1p, w2b, b2)


def _combine_kernel(dest_ref, gt_ref, x_ref, mod_ref, ys_ref, o_ref, d_s, buf, sem_i, sem):
    tc = x_ref.shape[0]
    cp = pltpu.make_async_copy(dest_ref.at[0, 0], d_s, sem_i)
    cp.start()
    cp.wait()

    def body(i, c):
        t = i // TOP_K
        k = i % TOP_K
        pltpu.make_async_copy(ys_ref.at[pl.ds(d_s[i], 1)], buf.at[k, pl.ds(t, 1)], sem).start()
        return c

    lax.fori_loop(0, TOP_K * tc, body, 0, unroll=8)
    for k in range(TOP_K):
        pltpu.make_async_copy(ys_ref.at[pl.ds(0, tc)], buf.at[k], sem).wait()
    g = gt_ref[...]
    y = g[:, 0:1] * buf[0]
    for k in range(1, TOP_K):
        y = y + g[:, k:k + 1] * buf[k]
    o_ref[...] = x_ref[...] + mod_ref[0, 5:6, :] * y


def _combine(dest3, gates, x2, mods_l, ys, S):
    T, D = x2.shape
    tc = dest3.shape[2] // TOP_K
    spb = S // tc
    return pl.pallas_call(
        _combine_kernel,
        out_shape=jax.ShapeDtypeStruct((T, D), F32),
        grid=(T // tc,),
        in_specs=[pl.BlockSpec((1, 1, TOP_K * tc), lambda i: (i, 0, 0)),
                  pl.BlockSpec((tc, LANES), lambda i: (i, 0)),
                  pl.BlockSpec((tc, D), lambda i: (i, 0)),
                  pl.BlockSpec((1, 6, D), lambda i: (i // spb, 0, 0)),
                  pl.BlockSpec(memory_space=pl.ANY)],
        out_specs=pl.BlockSpec((tc, D), lambda i: (i, 0)),
        scratch_shapes=[pltpu.SMEM((TOP_K * tc,), I32), pltpu.VMEM((TOP_K, tc, D), F32),
                        pltpu.SemaphoreType.DMA, pltpu.SemaphoreType.DMA],
        compiler_params=_params(("arbitrary",)),
        name="moe_combine",
    )(dest3, gates, x2, mods_l, ys)


def _moe(x2, mods_l, g2, w_router, b_router, w1p, b1p, w2b, b2, S):
    T, D = x2.shape
    tmf = 256
    wr = jnp.zeros((D, LANES), F32).at[:, :N_EXPERTS].set(w_router)
    wr_hi = wr.astype(BF16)
    wr_lo = (wr - wr_hi.astype(F32)).astype(BF16)
    br_p = jnp.full((1, LANES), NEG, F32).at[0, :N_EXPERTS].set(b_router)
    hp, e_idx, ranks, gates, cnt = _route(x2, mods_l, g2, wr_hi, wr_lo, br_p, S)

    counts = cnt[0, :N_EXPERTS].astype(I32)
    padded = ((counts + tmf - 1) // tmf) * tmf
    ends = jnp.cumsum(padded)
    offs = ends - padded
    dest = offs[e_idx[:, :TOP_K]] + ranks[:, :TOP_K]
    n_rows = TOP_K * T + N_EXPERTS * tmf
    ntiles = n_rows // tmf
    n_valid = (ends[-1] // tmf).astype(I32).reshape(1)
    tile_start = jnp.minimum(jnp.arange(ntiles, dtype=I32), n_valid[0] - 1) * tmf
    tile_e = jnp.searchsorted(ends, tile_start, side="right").astype(I32)
    td = min(256, S)
    dest3 = dest.reshape(T // td, 1, TOP_K * td)

    xs = _dispatch(dest3, hp, n_rows)
    ys = _expert_ffn(tile_e, n_valid, xs, w1p, b1p, w2b, b2, tmf)
    return _combine(dest3, gates, x2, mods_l, ys, S)


def _final_kernel(x_ref, g_ref, o_ref):
    o_ref[...] = _rms(x_ref[...], g_ref[...])


def _final_norm(x2, g):
    T, D = x2.shape
    tm = min(512, T)
    return pl.pallas_call(
        _final_kernel,
        out_shape=jax.ShapeDtypeStruct((T, D), F32),
        grid=(T // tm,),
        in_specs=[pl.BlockSpec((tm, D), lambda i: (i, 0)), pl.BlockSpec((1, D), lambda i: (0, 0))],
        out_specs=pl.BlockSpec((tm, D), lambda i: (i, 0)),
        compiler_params=_params(("parallel",)),
        name="final_norm",
    )(x2, g.reshape(1, D))


def _rearrange_w_in(w):
    D = w.shape[0]
    main = w[:, :2048]
    gates = w[:, 2048:2072]
    cq = w[:, 2072:2328]
    ckv = w[:, 2328:2456]
    kr = w[:, 2456:2488]
    z = lambda n: jnp.zeros((D, n), w.dtype)
    misc = jnp.concatenate([gates, z(KR_LANE - 24), kr, z(LANES - KR_LANE - QK_ROPE)], axis=1)
    return jnp.concatenate([main, cq, ckv, misc], axis=1).astype(BF16)


def _mla_weights(w_q_up, w_kv_up):
    dq = QK_NOPE + QK_ROPE
    wq = jnp.zeros((Q_LORA, C_HEADS * LANES), F32)
    wk = jnp.zeros((KV_LORA, C_HEADS * LANES), F32)
    wv = []
    for h in range(C_HEADS):
        wq = wq.at[:, h * LANES:h * LANES + dq].set(w_q_up[:, h * dq:(h + 1) * dq])
        kv = w_kv_up[:, h * (QK_NOPE + HEAD):(h + 1) * (QK_NOPE + HEAD)]
        wk = wk.at[:, h * LANES:h * LANES + QK_NOPE].set(kv[:, :QK_NOPE])
        wv.append(kv[:, QK_NOPE:])
    return wq.astype(BF16), wk.astype(BF16), jnp.concatenate(wv, axis=1).astype(BF16)


def kernel(x, c, positions, w_ada, b_ada, g_norm1, g_norm2, w_in, nsa_pos_k, nsa_pos_v, nsa_w1_k, nsa_b1_k, nsa_w2_k, nsa_b2_k, nsa_w1_v, nsa_b1_v, nsa_w2_v, nsa_b2_v, mla_g_q, mla_g_kv, mla_w_q_up, mla_w_kv_up, g_out_a, g_out_b, g_out_c, w_out, w_router, b_router, w_exp1, b_exp1, w_exp2, b_exp2, g_final):
    B, S, D = x.shape
    T = B * S
    depth = w_ada.shape[0]
    x2 = x.reshape(T, D)

    cab, sab, cc, sc = _rope_tables(positions.reshape(T, 1))
    nch = S // CMP_STRIDE
    n_cmp = (S - CMP_LEN) // CMP_STRIDE + 1
    end_idx = np.minimum(np.arange(nch) * CMP_STRIDE + CMP_LEN - 1, S - 1)
    cos_e = cab.reshape(B, S, LANES)[:, end_idx]
    sin_e = sab.reshape(B, S, LANES)[:, end_idx]
    mods = _ada_mods(c, w_ada, b_ada).reshape(depth, B, 6, D)
    del n_cmp

    for l in range(depth):
        mods_l = mods[l]
        (qa, ka, va, qb, kcmp, vcmp, kslc, vslc, kwin, vwin, cq, ckv, misc) = _in_proj(
            x2, mods_l, g_norm1[l].reshape(1, D), _rearrange_w_in(w_in[l]), cab, sab, S)

        oa_parts, lse_parts = [], []
        for window, d in A_PAIRS:
            o, lse = _dilated_attention(qa, ka, va, B, S, d, window // d)
            oa_parts.append(o)
            lse_parts.append(lse)

        kc = _compress(kcmp, nsa_pos_k[l], nsa_w1_k[l], nsa_b1_k[l], nsa_w2_k[l], nsa_b2_k[l],
                       cos_e, sin_e, B, S, True)
        vc = _compress(vcmp, nsa_pos_v[l], nsa_w1_v[l], nsa_b1_v[l], nsa_w2_v[l], nsa_b2_v[l],
                       cos_e, sin_e, B, S, False)
        kcc = jnp.concatenate([kc[:, 0, :, :HEAD], kc[:, 1, :, :HEAD]], axis=-1).astype(BF16)
        vcc = jnp.concatenate([vc[:, 0, :, :HEAD], vc[:, 1, :, :HEAD]], axis=-1).astype(BF16)
        ocmp, sel_t = _cmp_attention(qb, kcc, vcc, B, S)
        oslc = _select_attention(qb, kslc, vslc, sel_t, B, S)
        owin = _window_attention(qb, kwin, vwin, B, S)

        wq_p, wk_p, wv_p = _mla_weights(mla_w_q_up[l], mla_w_kv_up[l])
        qc, kcl, vcl = _mla_prep(cq, ckv, misc, mla_g_q[l].reshape(1, Q_LORA), mla_g_kv[l].reshape(1, KV_LORA),
                                 wq_p, wk_p, wv_p, cc, sc)
        oc = _latent_attention(qc, kcl, vcl, B, S)

        x2 = _out_proj(x2, mods_l, oa_parts, lse_parts, ocmp, oslc, owin, misc, oc,
                       g_out_a[l].reshape(1, A_W), g_out_b[l].reshape(1, B_W), g_out_c[l].reshape(1, C_W),
                       w_out[l].astype(BF16), S)

        w1p = jnp.concatenate([w_exp1[l][..., 0::2], w_exp1[l][..., 1::2]], axis=-1).astype(BF16)
        b1p = jnp.concatenate([b_exp1[l][..., 0::2], b_exp1[l][..., 1::2]], axis=-1).reshape(N_EXPERTS, 1, 2 * D_FF)
        x2 = _moe(x2, mods_l, g_norm2[l].reshape(1, D), w_router[l], b_router[l],
                  w1p, b1p, w_exp2[l].astype(BF16), b_exp2[l].reshape(N_EXPERTS, 1, D), S)

    return _final_norm(x2, g_final).reshape(B, S, D)
```

```python
import functools

import numpy as np
import jax
import jax.numpy as jnp
from jax import lax
from jax.experimental import pallas as pl
from jax.experimental.pallas import tpu as pltpu

F32 = jnp.float32
BF16 = jnp.bfloat16
I32 = jnp.int32
U32 = jnp.uint32

LANES = 128
HEAD = 64
NEG = -1e30
LOWEST = -3.0e38
NORM_EPS = 1e-6
ROPE_THETA = 10000.0

A_HEADS, B_HEADS, B_GROUPS, C_HEADS = 4, 8, 2, 4
A_PAIRS = ((128, 1), (512, 4), (2048, 16))
A_W, B_W, C_W = 256, 512, 256
CMP_LEN, CMP_STRIDE, CMP_HIDDEN = 32, 16, 256
SLC_BLOCK, N_SELECT, WINDOW = 64, 16, 512
FORCE_BONUS = 1e4
Q_LORA, KV_LORA, QK_NOPE, QK_ROPE = 256, 128, 64, 32
N_EXPERTS, TOP_K, D_FF = 32, 4, 1024
SWIGLU_ALPHA, SWIGLU_LIMIT = 1.702, 7.0
IN_COLS = 2560
KR_LANE = 64

VMEM_LIMIT = 56 * 1024 * 1024


def _params(sem, vmem=VMEM_LIMIT):
    return pltpu.CompilerParams(dimension_semantics=sem, vmem_limit_bytes=vmem)


def _rms(x, g):
    return x * lax.rsqrt(jnp.mean(x * x, axis=-1, keepdims=True) + NORM_EPS) * g


def _rope_rows():
    inv64 = ROPE_THETA ** (-np.arange(32, dtype=np.float32) * (2.0 / 64))
    inv32 = ROPE_THETA ** (-np.arange(16, dtype=np.float32) * (2.0 / 32))
    inv_ab = np.tile(np.concatenate([inv64, inv64]), 2)
    sgn_ab = np.tile(np.concatenate([-np.ones(32), np.ones(32)]), 2)
    inv_c = np.zeros(LANES, np.float32)
    sgn_c = np.zeros(LANES, np.float32)
    inv_c[KR_LANE:KR_LANE + 16] = inv32
    inv_c[KR_LANE + 16:KR_LANE + 32] = inv32
    sgn_c[KR_LANE:KR_LANE + 16] = -1.0
    sgn_c[KR_LANE + 16:KR_LANE + 32] = 1.0
    rows = np.stack([inv_ab, sgn_ab, inv_c, sgn_c]).astype(np.float32)
    return jnp.asarray(rows)


def _rope_tab_kernel(pos_ref, rows_ref, cab_ref, sab_ref, cc_ref, sc_ref):
    pos = pos_ref[...].astype(F32)
    ang = pos * rows_ref[0:1, :]
    cab_ref[...] = jnp.cos(ang)
    sab_ref[...] = jnp.sin(ang) * rows_ref[1:2, :]
    ang = pos * rows_ref[2:3, :]
    cc_ref[...] = jnp.cos(ang)
    sc_ref[...] = jnp.sin(ang) * rows_ref[3:4, :]


def _rope_tables(pos_col):
    R = pos_col.shape[0]
    tm = min(R, 1024)
    spec = pl.BlockSpec((tm, LANES), lambda i: (i, 0))
    return pl.pallas_call(
        _rope_tab_kernel,
        out_shape=[jax.ShapeDtypeStruct((R, LANES), F32)] * 4,
        grid=(R // tm,),
        in_specs=[pl.BlockSpec((tm, 1), lambda i: (i, 0)), pl.BlockSpec((4, LANES), lambda i: (0, 0))],
        out_specs=[spec] * 4,
        compiler_params=_params(("parallel",)),
        name="rope_tables",
    )(pos_col, _rope_rows())


def _swap32(t, lane):
    return jnp.where((lane & 32) == 0, pltpu.roll(t, 96, 1), pltpu.roll(t, 32, 1))


def _swap16(t, lane):
    return jnp.where(lane < KR_LANE + 16, pltpu.roll(t, 112, 1), pltpu.roll(t, 16, 1))


def _mods_kernel(c_ref, w_ref, b_ref, o_ref):
    c = c_ref[...]
    ca = (c * jax.nn.sigmoid(c)).astype(BF16)
    o_ref[0] = jnp.dot(ca, w_ref[0].astype(BF16), preferred_element_type=F32) + b_ref[0]


def _ada_mods(c, w_ada, b_ada):
    L, D, N = w_ada.shape
    B = c.shape[0]
    tn = 1536
    return pl.pallas_call(
        _mods_kernel,
        out_shape=jax.ShapeDtypeStruct((L, B, N), F32),
        grid=(L, N // tn),
        in_specs=[pl.BlockSpec((B, D), lambda l, j: (0, 0)),
                  pl.BlockSpec((1, D, tn), lambda l, j: (l, 0, j)),
                  pl.BlockSpec((1, 1, tn), lambda l, j: (l, 0, j))],
        out_specs=pl.BlockSpec((1, B, tn), lambda l, j: (l, 0, j)),
        compiler_params=_params(("parallel", "parallel")),
        name="ada_mods",
    )(c, w_ada, b_ada.reshape(L, 1, N))


def _inproj_kernel(x_ref, mod_ref, g_ref, w_ref, cab_ref, sab_ref,
                   qa_ref, ka_ref, va_ref, qb_ref, kcmp_ref, vcmp_ref, kslc_ref, vslc_ref,
                   kwin_ref, vwin_ref, cq_ref, ckv_ref, misc_ref):
    tm = x_ref.shape[0]
    x = x_ref[...]
    h = (_rms(x, g_ref[...]) * (1.0 + mod_ref[0, 1:2, :]) + mod_ref[0, 0:1, :]).astype(BF16)
    cos = cab_ref[...]
    sin = sab_ref[...]
    lane = lax.broadcasted_iota(I32, (tm, LANES), 1)
    lo64 = lane < HEAD

    def proj(c0, n):
        return jnp.dot(h, w_ref[:, c0:c0 + n], preferred_element_type=F32)

    def rope(t):
        return t * cos + _swap32(t, lane) * sin

    def put_q(ref, h0, t, dst):
        t = rope(t) * (HEAD ** -0.5)
        tr = pltpu.roll(t, HEAD, 1)
        for i in range(2):
            src = t if dst[i] == i else tr
            ref[h0 + i] = jnp.where(lo64 if dst[i] == 0 else ~lo64, src, 0.0).astype(BF16)

    y = proj(0, 256)
    put_q(qa_ref, 0, y[:, :128], (0, 1))
    put_q(qa_ref, 2, y[:, 128:], (0, 1))
    y = proj(256, 256)
    ka_ref[:, 0:128] = rope(y[:, :128]).astype(BF16)
    ka_ref[:, 128:256] = rope(y[:, 128:]).astype(BF16)
    va_ref[...] = proj(512, 256).astype(BF16)
    for j in range(2):
        y = proj(768 + 256 * j, 256)
        put_q(qb_ref, 4 * j, y[:, :128], (j, j))
        put_q(qb_ref, 4 * j + 2, y[:, 128:], (j, j))
    y = proj(1280, 256)
    kcmp_ref[...] = y[:, :128]
    vcmp_ref[...] = y[:, 128:]
    y = proj(1536, 256)
    kslc_ref[...] = rope(y[:, :128]).astype(BF16)
    vslc_ref[...] = y[:, 128:].astype(BF16)
    y = proj(1792, 256)
    kwin_ref[...] = rope(y[:, :128]).astype(BF16)
    vwin_ref[...] = y[:, 128:].astype(BF16)
    cq_ref[...] = proj(2048, 256)
    y = proj(2304, 256)
    ckv_ref[...] = y[:, :128]
    misc_ref[...] = y[:, 128:]


def _in_proj(x2, mods_l, g1, w_in_r, cab, sab, S):
    T, D = x2.shape
    tm = min(512, S)
    spb = S // tm
    row = lambda i: (i, 0)
    shp = jax.ShapeDtypeStruct
    outs = [shp((A_HEADS, T, LANES), BF16), shp((T, A_W), BF16), shp((T, A_W), BF16),
            shp((B_HEADS, T, LANES), BF16),
            shp((T, LANES), F32), shp((T, LANES), F32), shp((T, LANES), BF16), shp((T, LANES), BF16),
            shp((T, LANES), BF16), shp((T, LANES), BF16),
            shp((T, Q_LORA), F32), shp((T, LANES), F32), shp((T, LANES), F32)]
    t128 = pl.BlockSpec((tm, LANES), row)
    t256 = pl.BlockSpec((tm, 256), row)
    out_specs = [pl.BlockSpec((A_HEADS, tm, LANES), lambda i: (0, i, 0)), t256, t256,
                 pl.BlockSpec((B_HEADS, tm, LANES), lambda i: (0, i, 0)),
                 t128, t128, t128, t128, t128, t128, t256, t128, t128]
    return pl.pallas_call(
        _inproj_kernel,
        out_shape=outs,
        grid=(T // tm,),
        in_specs=[pl.BlockSpec((tm, D), row),
                  pl.BlockSpec((1, 6, D), lambda i: (i // spb, 0, 0)),
                  pl.BlockSpec((1, D), lambda i: (0, 0)),
                  pl.BlockSpec((D, IN_COLS), lambda i: (0, 0)),
                  t128, t128],
        out_specs=out_specs,
        compiler_params=_params(("parallel",)),
        name="in_proj",
    )(x2, mods_l, g1, w_in_r, cab, sab)


_NT = (((1,), (1,)), ((), ()))
_TN = (((0,), (0,)), ((), ()))


KEY_SUB = 128
QUERY_SUB = 256


def _online_chunk(k, v, q, keep, bias, m, l, acc):
    s = lax.dot_general(k, q, _NT, preferred_element_type=F32)
    if bias is not None:
        s = s + bias
    s = jnp.where(keep, s, NEG)
    m_next = jnp.maximum(m, s.max(axis=0, keepdims=True))
    alpha = jnp.exp(m - m_next)
    p = jnp.exp(s - m_next)
    l = alpha * l + p.sum(axis=0, keepdims=True)
    acc = acc * alpha + lax.dot_general(v, p.astype(BF16), _TN, preferred_element_type=F32)
    return m_next, l, acc


def _mask_heads(s, masks, tq, fill):
    return jnp.concatenate([jnp.where(masks[h], s[:, h * tq:(h + 1) * tq], fill)
                            for h in range(len(masks))], axis=1)


def _emit_heads(o_ref, acc_t, halves, tq):
    nh = len(halves)
    lo64 = lax.broadcasted_iota(I32, (tq, LANES), 1) < HEAD
    for c in range(nh // 2):
        parts = []
        for h in (2 * c, 2 * c + 1):
            o = acc_t[:, h * tq:(h + 1) * tq].T
            if halves[h] != h % 2:
                o = pltpu.roll(o, HEAD, 1)
            parts.append(o)
        o_ref[:, c * LANES:(c + 1) * LANES] = jnp.where(lo64, parts[0], parts[1])


def _band_kernel(*refs, nblk, tq, tk, max_dist, stacks, halves, with_lse, qi_axis):
    q_ref = refs[0]
    k_refs = refs[1:1 + nblk]
    v_refs = refs[1 + nblk:1 + 2 * nblk]
    o_ref = refs[1 + 2 * nblk]
    lse_ref = refs[2 + 2 * nblk] if with_lse else None
    nh = len(halves)
    qi = pl.program_id(qi_axis)
    ks = min(KEY_SUB, tk)
    cq = min(QUERY_SUB, tq)
    rc = lax.broadcasted_iota(I32, (ks, cq), 0) - lax.broadcasted_iota(I32, (ks, cq), 1)
    head_blk = {}
    for (h0, n, blk) in stacks:
        for h in range(h0, h0 + n):
            head_blk[h] = blk
    outs, lses = [], []
    for h in range(nh):
        lanes = slice(head_blk[h] * LANES, (head_blk[h] + 1) * LANES)
        for c in range(tq // cq):
            q = q_ref[h, c * cq:(c + 1) * cq, :]
            m = jnp.full((1, cq), NEG, F32)
            l = jnp.zeros((1, cq), F32)
            acc = jnp.zeros((LANES, cq), F32)
            for i in range(nblk):
                kt = qi * (tq // tk) - (nblk - tq // tk) + i
                for kk in range(tk // ks):
                    base = qi * tq + c * cq - (kt * tk + kk * ks)
                    base = jnp.where(kt >= 0, base, -(1 << 20))
                    keep = (rc <= base) & (rc >= base - max_dist)
                    m, l, acc = _online_chunk(k_refs[i][kk * ks:(kk + 1) * ks, lanes],
                                              v_refs[i][kk * ks:(kk + 1) * ks, lanes], q, keep, None, m, l, acc)
            outs.append(acc / l)
            lses.append(jnp.broadcast_to(m + jnp.log(l), (LANES, cq)))
    _emit_heads(o_ref, jnp.concatenate(outs, axis=1), halves, tq)
    if with_lse:
        _emit_heads(lse_ref, jnp.concatenate(lses, axis=1), tuple(h % 2 for h in range(nh)), tq)


def _dilated_attention(qa, ka, va, B, S, d, max_dist):
    L = S // d
    tq = min(256, L)
    tk = 128
    nprev = -(-max_dist // tk)
    nblk = nprev + tq // tk
    q_v = qa.reshape(A_HEADS, B, L, d * LANES)
    k_v = ka.reshape(B, L, d * A_W)
    v_v = va.reshape(B, L, d * A_W)
    kv_specs = []
    for i in range(nblk):
        kv_specs.append(pl.BlockSpec(
            (None, tk, A_W),
            functools.partial(lambda b, r, qi, i: (b, jnp.maximum(qi * (tq // tk) - nprev + i, 0), r), i=i)))
    out_spec = pl.BlockSpec((None, tq, A_W), lambda b, r, qi: (b, qi, r))
    kern = functools.partial(_band_kernel, nblk=nblk, tq=tq, tk=tk, max_dist=max_dist,
                             stacks=((0, 2, 0), (2, 2, 1)), halves=(0, 1, 0, 1), with_lse=True, qi_axis=2)
    o, lse = pl.pallas_call(
        kern,
        out_shape=[jax.ShapeDtypeStruct((B, L, d * A_W), F32)] * 2,
        grid=(B, d, L // tq),
        in_specs=[pl.BlockSpec((A_HEADS, None, tq, LANES), lambda b, r, qi: (0, b, qi, r))]
                 + kv_specs + kv_specs,
        out_specs=[out_spec, out_spec],
        compiler_params=_params(("parallel", "parallel", "parallel")),
        name=f"dilated_attn_d{d}",
    )(q_v, *([k_v] * nblk), *([v_v] * nblk))
    return o.reshape(B * S, A_W), lse.reshape(B * S, A_W)


def _window_attention(qb, kwin, vwin, B, S):
    tq = min(256, S)
    tk = tq
    max_dist = WINDOW - 1
    nprev = -(-max_dist // tk)
    nblk = nprev + 1
    q_v = qb.reshape(B_HEADS, B, S, LANES)
    k_v = kwin.reshape(B, S, LANES)
    v_v = vwin.reshape(B, S, LANES)
    kv_specs = [pl.BlockSpec((None, tk, LANES),
                             functools.partial(lambda b, qi, i: (b, jnp.maximum(qi - nprev + i, 0), 0), i=i))
                for i in range(nblk)]
    kern = functools.partial(_band_kernel, nblk=nblk, tq=tq, tk=tk, max_dist=max_dist,
                             stacks=((0, B_HEADS, 0),), halves=(0, 0, 0, 0, 1, 1, 1, 1),
                             with_lse=False, qi_axis=1)
    o = pl.pallas_call(
        kern,
        out_shape=jax.ShapeDtypeStruct((B, S, B_W), F32),
        grid=(B, S // tq),
        in_specs=[pl.BlockSpec((B_HEADS, None, tq, LANES), lambda b, qi: (0, b, qi, 0))]
                 + kv_specs + kv_specs,
        out_specs=pl.BlockSpec((None, tq, B_W), lambda b, qi: (b, qi, 0)),
        compiler_params=_params(("parallel", "parallel")),
        name="window_attn",
    )(q_v, *([k_v] * nblk), *([v_v] * nblk))
    return o.reshape(B * S, B_W)


def _flash_kernel(qi_ref, kj_ref, first_ref, last_ref, *refs, tq, tk, qk_groups, pv_groups, halves, select):
    if select:
        q_ref, k_ref, v_ref, sel_ref, o_ref, m_sc, l_sc, acc_sc = refs
    else:
        q_ref, k_ref, v_ref, o_ref, m_sc, l_sc, acc_sc = refs
    nh = len(halves)
    step = pl.program_id(1)
    qi = qi_ref[step]
    kj = kj_ref[step]

    @pl.when(first_ref[step] == 1)
    def _():
        m_sc[...] = jnp.full(m_sc.shape, NEG, F32)
        l_sc[...] = jnp.zeros(l_sc.shape, F32)
        acc_sc[...] = jnp.zeros(acc_sc.shape, F32)

    ks = min(KEY_SUB, tk)
    cq = min(QUERY_SUB, tq)
    rc = lax.broadcasted_iota(I32, (ks, cq), 0) - lax.broadcasted_iota(I32, (ks, cq), 1)
    head_kb, head_vb = {}, {}
    for (h0, n, kb) in qk_groups:
        for h in range(h0, h0 + n):
            head_kb[h] = kb
    for (h0, n, vb) in pv_groups:
        for h in range(h0, h0 + n):
            head_vb[h] = vb
    hpg = nh // B_GROUPS
    spb = ks // SLC_BLOCK if select else 1
    for h in range(nh):
        klanes = slice(head_kb[h] * LANES, (head_kb[h] + 1) * LANES)
        vlanes = slice(head_vb[h] * LANES, (head_vb[h] + 1) * LANES)
        for c in range(tq // cq):
            cols = slice(h * tq + c * cq, h * tq + (c + 1) * cq)
            q = q_ref[h, c * cq:(c + 1) * cq, :]
            m, l, acc = m_sc[:, cols], l_sc[:, cols], acc_sc[:, cols]
            for kk in range(tk // ks):
                keep = rc <= (qi * tq + c * cq) - (kj * tk + kk * ks)
                bias = None
                if select:
                    rows = [jnp.broadcast_to(
                        (sel_ref[h // hpg, pl.ds((kj * (tk // ks) + kk) * spb + j, 1), c * cq:(c + 1) * cq] - 1.0)
                        * (-NEG), (SLC_BLOCK, cq)) for j in range(spb)]
                    bias = rows[0] if spb == 1 else jnp.concatenate(rows, axis=0)
                m, l, acc = _online_chunk(k_ref[kk * ks:(kk + 1) * ks, klanes], v_ref[kk * ks:(kk + 1) * ks, vlanes],
                                          q, keep, bias, m, l, acc)
            m_sc[:, cols] = m
            l_sc[:, cols] = l
            acc_sc[:, cols] = acc

    @pl.when(last_ref[step] == 1)
    def _():
        _emit_heads(o_ref, acc_sc[...] / l_sc[...], halves, tq)


def _causal_schedule(S, tq, tk):
    qi, kj, first, last = [], [], [], []
    for i in range(S // tq):
        nk = (i * tq + tq - 1) // tk + 1
        for j in range(nk):
            qi.append(i)
            kj.append(j)
            first.append(1 if j == 0 else 0)
            last.append(1 if j == nk - 1 else 0)
    return [jnp.asarray(np.asarray(a, np.int32)) for a in (qi, kj, first, last)]


def _select_attention(qb, kslc, vslc, sel_t, B, S):
    tq = min(512, S)
    tk = min(256, S)
    n_slc = S // SLC_BLOCK
    tabs = _causal_schedule(S, tq, tk)
    nsteps = tabs[0].shape[0]
    kern = functools.partial(_flash_kernel, tq=tq, tk=tk, qk_groups=((0, B_HEADS, 0),),
                             pv_groups=((0, B_HEADS, 0),), halves=(0, 0, 0, 0, 1, 1, 1, 1), select=True)
    o = pl.pallas_call(
        kern,
        out_shape=jax.ShapeDtypeStruct((B, S, B_W), F32),
        grid_spec=pltpu.PrefetchScalarGridSpec(
            num_scalar_prefetch=4,
            grid=(B, nsteps),
            in_specs=[pl.BlockSpec((B_HEADS, None, tq, LANES), lambda b, s, qi, kj, f, l: (0, b, qi[s], 0)),
                      pl.BlockSpec((None, tk, LANES), lambda b, s, qi, kj, f, l: (b, kj[s], 0)),
                      pl.BlockSpec((None, tk, LANES), lambda b, s, qi, kj, f, l: (b, kj[s], 0)),
                      pl.BlockSpec((None, B_GROUPS, n_slc, tq), lambda b, s, qi, kj, f, l: (b, 0, 0, qi[s]))],
            out_specs=pl.BlockSpec((None, tq, B_W), lambda b, s, qi, kj, f, l: (b, qi[s], 0)),
            scratch_shapes=[pltpu.VMEM((1, B_HEADS * tq), F32), pltpu.VMEM((1, B_HEADS * tq), F32),
                            pltpu.VMEM((LANES, B_HEADS * tq), F32)]),
        compiler_params=_params(("parallel", "arbitrary")),
        name="select_attn",
    )(*tabs, qb.reshape(B_HEADS, B, S, LANES), kslc.reshape(B, S, LANES), vslc.reshape(B, S, LANES), sel_t)
    return o.reshape(B * S, B_W)


def _latent_attention(qc, kc, vc, B, S):
    tq = min(512, S)
    tk = min(512, S)
    tabs = _causal_schedule(S, tq, tk)
    nsteps = tabs[0].shape[0]
    kern = functools.partial(_flash_kernel, tq=tq, tk=tk,
                             qk_groups=tuple((h, 1, h) for h in range(C_HEADS)),
                             pv_groups=((0, 2, 0), (2, 2, 1)), halves=(0, 1, 0, 1), select=False)
    o = pl.pallas_call(
        kern,
        out_shape=jax.ShapeDtypeStruct((B, S, C_W), F32),
        grid_spec=pltpu.PrefetchScalarGridSpec(
            num_scalar_prefetch=4,
            grid=(B, nsteps),
            in_specs=[pl.BlockSpec((C_HEADS, None, tq, LANES), lambda b, s, qi, kj, f, l: (0, b, qi[s], 0)),
                      pl.BlockSpec((None, tk, C_HEADS * LANES), lambda b, s, qi, kj, f, l: (b, kj[s], 0)),
                      pl.BlockSpec((None, tk, C_W), lambda b, s, qi, kj, f, l: (b, kj[s], 0))],
            out_specs=pl.BlockSpec((None, tq, C_W), lambda b, s, qi, kj, f, l: (b, qi[s], 0)),
            scratch_shapes=[pltpu.VMEM((1, C_HEADS * tq), F32), pltpu.VMEM((1, C_HEADS * tq), F32),
                            pltpu.VMEM((LANES, C_HEADS * tq), F32)]),
        compiler_params=_params(("parallel", "arbitrary")),
        name="latent_attn",
    )(*tabs, qc.reshape(C_HEADS, B, S, LANES), kc.reshape(B, S, C_HEADS * LANES), vc.reshape(B, S, C_W))
    return o.reshape(B * S, C_W)


def _gelu_tanh(x):
    return 0.5 * x * (1.0 + jnp.tanh(0.7978845608028654 * (x + 0.044715 * x * x * x)))


def _compress_kernel(u_ref, pos_ref, w1_ref, b1_ref, w2_ref, b2_ref, cos_ref, sin_ref, o_ref, *, rotate):
    n = u_ref.shape[0]
    half = CMP_STRIDE * HEAD
    u = u_ref[...]
    ua = (u + pos_ref[:, :half]).astype(BF16)
    ub = (u + pos_ref[:, half:]).astype(BF16)
    pa = jnp.dot(ua, w1_ref[:half, :].astype(BF16), preferred_element_type=F32)
    pb = jnp.dot(ub, w1_ref[half:, :].astype(BF16), preferred_element_type=F32)
    hid = _gelu_tanh(pa + pltpu.roll(pb, n - 1, 0) + b1_ref[...])
    y = jnp.dot(hid.astype(BF16), w2_ref[...].astype(BF16), preferred_element_type=F32) + b2_ref[...]
    if rotate:
        lane = lax.broadcasted_iota(I32, y.shape, 1)
        y = y * cos_ref[...] + _swap32(y, lane) * sin_ref[...]
    o_ref[...] = y


def _compress(t, pos_emb, w1, b1, w2, b2, cos_e, sin_e, B, S, rotate):
    nch = S // CMP_STRIDE
    u = t.reshape(B, S, B_GROUPS, HEAD).transpose(0, 2, 1, 3).reshape(B, B_GROUPS, nch, CMP_STRIDE * HEAD)
    w2d = jnp.concatenate([w2, w2], axis=1)
    b2d = jnp.concatenate([b2, b2]).reshape(1, LANES)
    kern = functools.partial(_compress_kernel, rotate=rotate)
    tab = pl.BlockSpec((None, nch, LANES), lambda b, g: (b, 0, 0))
    return pl.pallas_call(
        kern,
        out_shape=jax.ShapeDtypeStruct((B, B_GROUPS, nch, LANES), F32),
        grid=(B, B_GROUPS),
        in_specs=[pl.BlockSpec((None, None, nch, CMP_STRIDE * HEAD), lambda b, g: (b, g, 0, 0)),
                  pl.BlockSpec((1, CMP_LEN * HEAD), lambda b, g: (0, 0)),
                  pl.BlockSpec((CMP_LEN * HEAD, CMP_HIDDEN), lambda b, g: (0, 0)),
                  pl.BlockSpec((1, CMP_HIDDEN), lambda b, g: (0, 0)),
                  pl.BlockSpec((CMP_HIDDEN, LANES), lambda b, g: (0, 0)),
                  pl.BlockSpec((1, LANES), lambda b, g: (0, 0)),
                  tab, tab],
        out_specs=pl.BlockSpec((None, None, nch, LANES), lambda b, g: (b, g, 0, 0)),
        compiler_params=_params(("parallel", "parallel")),
        name="nsa_compress",
    )(u, pos_emb.reshape(1, CMP_LEN * HEAD), w1, b1.reshape(1, CMP_HIDDEN), w2d, b2d, cos_e, sin_e)


def _cmp_attn_kernel(q_ref, kc_ref, vc_ref, ov_ref, o_ref, sel_ref, *, tq, n_cmp, n_slc):
    qi = pl.program_id(1)
    nk = kc_ref.shape[0]
    q = q_ref[...].reshape(B_HEADS * tq, LANES)
    s = lax.dot_general(kc_ref[...], q, _NT, preferred_element_type=F32)
    t_idx = qi * tq + lax.broadcasted_iota(I32, (nk, tq), 1)
    c_idx = lax.broadcasted_iota(I32, (nk, tq), 0)
    mask = ((c_idx * CMP_STRIDE + (CMP_LEN - 1)) <= t_idx) & (c_idx < n_cmp)
    masks = [mask] * B_HEADS
    s = _mask_heads(s, masks, tq, NEG)
    m = s.max(axis=0, keepdims=True)
    e = _mask_heads(jnp.exp(s - m), masks, tq, 0.0)
    den = e.sum(axis=0, keepdims=True)
    p = e / jnp.where(den > 0, den, 1.0)
    o_t = lax.dot_general(vc_ref[...], p.astype(BF16), _TN, preferred_element_type=F32)
    hpg = B_HEADS // B_GROUPS
    _emit_heads(o_ref, o_t, tuple(h // hpg for h in range(B_HEADS)), tq)

    j_idx = lax.broadcasted_iota(I32, (n_slc, tq), 0)
    t_col = qi * tq + lax.broadcasted_iota(I32, (n_slc, tq), 1)
    cur = t_col // SLC_BLOCK
    forced = (j_idx == 0) | (j_idx == cur) | (j_idx == cur - 1)
    valid = j_idx * SLC_BLOCK <= t_col
    for g in range(B_GROUPS):
        ps = p[:, g * hpg * tq:(g * hpg + 1) * tq]
        for h in range(g * hpg + 1, (g + 1) * hpg):
            ps = ps + p[:, h * tq:(h + 1) * tq]
        hi = ps.astype(BF16)
        lo = (ps - hi.astype(F32)).astype(BF16)
        imp = (jnp.dot(ov_ref[...], hi, preferred_element_type=F32)
               + jnp.dot(ov_ref[...], lo, preferred_element_type=F32))
        score = jnp.where(forced, imp + FORCE_BONUS, jnp.where(valid, imp, -1.0))
        sel = jnp.zeros((n_slc, tq), F32)
        j_f = j_idx.astype(F32)
        for _ in range(min(N_SELECT, n_slc)):
            mx = score.max(axis=0, keepdims=True)
            idx = jnp.where(score == mx, j_f, float(n_slc)).min(axis=0, keepdims=True)
            pick = j_f == idx
            sel = jnp.where(pick, 1.0, sel)
            score = jnp.where(pick, LOWEST, score)
        sel_ref[g] = sel


def _cmp_attention(qb, kcc, vcc, B, S):
    tq = min(256, S)
    nk = S // CMP_STRIDE
    n_cmp = (S - CMP_LEN) // CMP_STRIDE + 1
    n_slc = S // SLC_BLOCK
    cmp_tok = np.arange(n_cmp)[:, None] * CMP_STRIDE + np.arange(CMP_LEN)[None, :]
    overlap = ((cmp_tok[:, :, None] // SLC_BLOCK) == np.arange(n_slc)[None, None, :]).mean(axis=1)
    ov_t = np.zeros((n_slc, nk), np.float32)
    ov_t[:, :n_cmp] = overlap.T
    kern = functools.partial(_cmp_attn_kernel, tq=tq, n_cmp=n_cmp, n_slc=n_slc)
    o, sel = pl.pallas_call(
        kern,
        out_shape=[jax.ShapeDtypeStruct((B, S, B_W), F32),
                   jax.ShapeDtypeStruct((B, B_GROUPS, n_slc, S), F32)],
        grid=(B, S // tq),
        in_specs=[pl.BlockSpec((B_HEADS, None, tq, LANES), lambda b, qi: (0, b, qi, 0)),
                  pl.BlockSpec((None, nk, LANES), lambda b, qi: (b, 0, 0)),
                  pl.BlockSpec((None, nk, LANES), lambda b, qi: (b, 0, 0)),
                  pl.BlockSpec((n_slc, nk), lambda b, qi: (0, 0))],
        out_specs=[pl.BlockSpec((None, tq, B_W), lambda b, qi: (b, qi, 0)),
                   pl.BlockSpec((None, B_GROUPS, n_slc, tq), lambda b, qi: (b, 0, 0, qi))],
        compiler_params=_params(("parallel", "parallel")),
        name="cmp_attn",
    )(qb.reshape(B_HEADS, B, S, LANES), kcc, vcc, jnp.asarray(ov_t, BF16))
    return o.reshape(B * S, B_W), sel


def _mla_prep_kernel(cq_ref, ckv_ref, misc_ref, gq_ref, gkv_ref, wq_ref, wk_ref, wv_ref, cc_ref, sc_ref,
                     q_ref, k_ref, v_ref):
    tm = cq_ref.shape[0]
    lane = lax.broadcasted_iota(I32, (tm, LANES), 1)
    cos = cc_ref[...]
    sin = sc_ref[...]

    def rope(t):
        return t * cos + _swap16(t, lane) * sin

    nq = _rms(cq_ref[...], gq_ref[...]).astype(BF16)
    q = jnp.dot(nq, wq_ref[...], preferred_element_type=F32)
    scale = (QK_NOPE + QK_ROPE) ** -0.5
    for h in range(C_HEADS):
        q_ref[h] = (rope(q[:, h * LANES:(h + 1) * LANES]) * scale).astype(BF16)
    nkv = _rms(ckv_ref[...], gkv_ref[...]).astype(BF16)
    k = jnp.dot(nkv, wk_ref[...], preferred_element_type=F32)
    in_rope = (lane >= KR_LANE) & (lane < KR_LANE + QK_ROPE)
    kr = jnp.where(in_rope, rope(misc_ref[...]), 0.0)
    for h in range(C_HEADS):
        k_ref[:, h * LANES:(h + 1) * LANES] = (k[:, h * LANES:(h + 1) * LANES] + kr).astype(BF16)
    v_ref[...] = jnp.dot(nkv, wv_ref[...], preferred_element_type=F32).astype(BF16)


def _mla_prep(cq, ckv, misc, g_q, g_kv, wq_p, wk_p, wv_p, cc, sc):
    T = cq.shape[0]
    tm = min(512, T)
    row = lambda i: (i, 0)
    fix = lambda i: (0, 0)
    t128 = pl.BlockSpec((tm, LANES), row)
    return pl.pallas_call(
        _mla_prep_kernel,
        out_shape=[jax.ShapeDtypeStruct((C_HEADS, T, LANES), BF16),
                   jax.ShapeDtypeStruct((T, C_HEADS * LANES), BF16),
                   jax.ShapeDtypeStruct((T, C_W), BF16)],
        grid=(T // tm,),
        in_specs=[pl.BlockSpec((tm, Q_LORA), row), t128, t128,
                  pl.BlockSpec((1, Q_LORA), fix), pl.BlockSpec((1, KV_LORA), fix),
                  pl.BlockSpec((Q_LORA, C_HEADS * LANES), fix),
                  pl.BlockSpec((KV_LORA, C_HEADS * LANES), fix),
                  pl.BlockSpec((KV_LORA, C_W), fix), t128, t128],
        out_specs=[pl.BlockSpec((C_HEADS, tm, LANES), lambda i: (0, i, 0)),
                   pl.BlockSpec((tm, C_HEADS * LANES), row),
                   pl.BlockSpec((tm, C_W), row)],
        compiler_params=_params(("parallel",)),
        name="mla_prep",
    )(cq, ckv, misc, g_q, g_kv, wq_p, wk_p, wv_p, cc, sc)


def _outproj_kernel(x_ref, mod_ref, o1_ref, o2_ref, o3_ref, l1_ref, l2_ref, l3_ref,
                    ocmp_ref, oslc_ref, owin_ref, misc_ref, ge_ref, oc_ref,
                    ga_ref, gb_ref, gc_ref, w_ref, out_ref):
    l1, l2, l3 = l1_ref[...], l2_ref[...], l3_ref[...]
    mx = jnp.maximum(jnp.maximum(l1, l2), l3)
    e1, e2, e3 = jnp.exp(l1 - mx), jnp.exp(l2 - mx), jnp.exp(l3 - mx)
    oa = (e1 * o1_ref[...] + e2 * o2_ref[...] + e3 * o3_ref[...]) / (e1 + e2 + e3)

    sg = jax.nn.sigmoid(misc_ref[...])
    hi = sg.astype(BF16)
    lo = (sg - hi.astype(F32)).astype(BF16)

    def gate(br):
        return (jnp.dot(hi, ge_ref[br], preferred_element_type=F32)
                + jnp.dot(lo, ge_ref[br], preferred_element_type=F32))

    ob = gate(0) * ocmp_ref[...] + gate(1) * oslc_ref[...] + gate(2) * owin_ref[...]
    na = _rms(oa, ga_ref[...]).astype(BF16)
    nb = _rms(ob, gb_ref[...]).astype(BF16)
    nc = _rms(oc_ref[...], gc_ref[...]).astype(BF16)
    mixed = (jnp.dot(na, w_ref[0:A_W, :], preferred_element_type=F32)
             + jnp.dot(nb, w_ref[A_W:A_W + B_W, :], preferred_element_type=F32)
             + jnp.dot(nc, w_ref[A_W + B_W:, :], preferred_element_type=F32))
    out_ref[...] = x_ref[...] + mod_ref[0, 2:3, :] * mixed


def _gate_expand():
    ge = np.zeros((3, LANES, B_W), np.float32)
    for h in range(B_HEADS):
        for br in range(3):
            ge[br, h * 3 + br, h * HEAD:(h + 1) * HEAD] = 1.0
    return jnp.asarray(ge, BF16)


def _out_proj(x2, mods_l, oa_parts, lse_parts, ocmp, oslc, owin, misc, oc, g_a, g_b, g_c, w_out_b, S):
    T, D = x2.shape
    tm = min(256, S)
    spb = S // tm
    row = lambda i: (i, 0)
    fix = lambda i: (0, 0)
    t128 = pl.BlockSpec((tm, LANES), row)
    t256 = pl.BlockSpec((tm, 256), row)
    t512 = pl.BlockSpec((tm, 512), row)
    return pl.pallas_call(
        _outproj_kernel,
        out_shape=jax.ShapeDtypeStruct((T, D), F32),
        grid=(T // tm,),
        in_specs=[pl.BlockSpec((tm, D), row),
                  pl.BlockSpec((1, 6, D), lambda i: (i // spb, 0, 0)),
                  t256, t256, t256, t256, t256, t256, t512, t512, t512, t128,
                  pl.BlockSpec((3, LANES, B_W), lambda i: (0, 0, 0)),
                  t256,
                  pl.BlockSpec((1, A_W), fix), pl.BlockSpec((1, B_W), fix), pl.BlockSpec((1, C_W), fix),
                  pl.BlockSpec((D, D), fix)],
        out_specs=pl.BlockSpec((tm, D), row),
        compiler_params=_params(("parallel",)),
        name="out_proj",
    )(x2, mods_l, *oa_parts, *lse_parts, ocmp, oslc, owin, misc, _gate_expand(), oc, g_a, g_b, g_c, w_out_b)


def _route_kernel(x_ref, mod_ref, g_ref, wh_ref, wl_ref, b_ref, hp_ref, e_ref, r_ref, gt_ref, cnt_ref, carry):
    tm = x_ref.shape[0]

    @pl.when(pl.program_id(0) == 0)
    def _():
        carry[...] = jnp.zeros(carry.shape, F32)

    h = _rms(x_ref[...], g_ref[...]) * (1.0 + mod_ref[0, 4:5, :]) + mod_ref[0, 3:4, :]
    hb = h.astype(BF16)
    hlo = (h - hb.astype(F32)).astype(BF16)
    bits = lax.bitcast_convert_type(hb.astype(F32), U32)
    half = h.shape[1] // 2
    hp_ref[...] = (bits[:, half:] & jnp.uint32(0xFFFF0000)) | (bits[:, :half] >> 16)

    logits = (jnp.dot(hb, wh_ref[...], preferred_element_type=F32)
              + jnp.dot(hlo, wh_ref[...], preferred_element_type=F32)
              + jnp.dot(hb, wl_ref[...], preferred_element_type=F32)) + b_ref[...]
    lane = lax.broadcasted_iota(I32, (tm, LANES), 1)
    lane_f = lane.astype(F32)
    vals, idxs, picks = [], [], []
    lg = logits
    for _ in range(TOP_K):
        mx = lg.max(axis=-1, keepdims=True)
        idx = jnp.where(lg == mx, lane_f, float(LANES)).min(axis=-1, keepdims=True)
        pick = lane_f == idx
        vals.append(mx)
        idxs.append(idx.astype(I32))
        picks.append(pick)
        lg = jnp.where(pick, LOWEST, lg)
    es = [jnp.exp(v - vals[0]) for v in vals]
    den = es[0] + es[1] + es[2] + es[3]
    onehot = jnp.zeros((tm, LANES), F32)
    for pk in picks:
        onehot = jnp.where(pk, 1.0, onehot)
    tri = (lax.broadcasted_iota(I32, (tm, tm), 0) >= lax.broadcasted_iota(I32, (tm, tm), 1)).astype(BF16)
    cs = jnp.dot(tri, onehot.astype(BF16), preferred_element_type=F32)
    rank = carry[...] + cs - 1.0
    e_out = jnp.zeros((tm, LANES), I32)
    r_out = jnp.zeros((tm, LANES), I32)
    g_out = jnp.zeros((tm, LANES), F32)
    for k in range(TOP_K):
        rk = jnp.where(picks[k], rank, 0.0).sum(axis=-1, keepdims=True)
        e_out = jnp.where(lane == k, idxs[k], e_out)
        r_out = jnp.where(lane == k, rk.astype(I32), r_out)
        g_out = jnp.where(lane == k, es[k] / den, g_out)
    e_ref[...] = e_out
    r_ref[...] = r_out
    gt_ref[...] = g_out
    new = carry[...] + onehot.sum(axis=0, keepdims=True)
    carry[...] = new
    cnt_ref[...] = new


def _route(x2, mods_l, g2, wr_hi, wr_lo, br_p, S):
    T, D = x2.shape
    tm = min(256, S)
    spb = S // tm
    row = lambda i: (i, 0)
    fix = lambda i: (0, 0)
    t128 = pl.BlockSpec((tm, LANES), row)
    return pl.pallas_call(
        _route_kernel,
        out_shape=[jax.ShapeDtypeStruct((T, D // 2), U32),
                   jax.ShapeDtypeStruct((T, LANES), I32), jax.ShapeDtypeStruct((T, LANES), I32),
                   jax.ShapeDtypeStruct((T, LANES), F32), jax.ShapeDtypeStruct((1, LANES), F32)],
        grid=(T // tm,),
        in_specs=[pl.BlockSpec((tm, D), row),
                  pl.BlockSpec((1, 6, D), lambda i: (i // spb, 0, 0)),
                  pl.BlockSpec((1, D), fix),
                  pl.BlockSpec((D, LANES), fix), pl.BlockSpec((D, LANES), fix), pl.BlockSpec((1, LANES), fix)],
        out_specs=[pl.BlockSpec((tm, D // 2), row), t128, t128, t128, pl.BlockSpec((1, LANES), fix)],
        scratch_shapes=[pltpu.VMEM((1, LANES), F32)],
        compiler_params=_params(("arbitrary",)),
        name="moe_route",
    )(x2, mods_l, g2, wr_hi, wr_lo, br_p)


def _dispatch_kernel(dest_ref, hp_ref, xs_in_ref, xs_ref, d_s, sem_i, sem):
    del xs_in_ref
    td = hp_ref.shape[0]
    cp = pltpu.make_async_copy(dest_ref.at[0, 0], d_s, sem_i)
    cp.start()
    cp.wait()

    def body(i, c):
        t = i // TOP_K
        pltpu.make_async_copy(hp_ref.at[pl.ds(t, 1)], xs_ref.at[pl.ds(d_s[i], 1)], sem).start()
        return c

    lax.fori_loop(0, TOP_K * td, body, 0, unroll=8)
    for _ in range(TOP_K):
        pltpu.make_async_copy(hp_ref, xs_ref.at[pl.ds(0, td)], sem).wait()


def _dispatch(dest3, hp, n_rows):
    T, W = hp.shape
    td = dest3.shape[2] // TOP_K
    xs0 = jnp.zeros((n_rows, W), U32)
    return pl.pallas_call(
        _dispatch_kernel,
        out_shape=jax.ShapeDtypeStruct((n_rows, W), U32),
        grid=(T // td,),
        in_specs=[pl.BlockSpec((1, 1, TOP_K * td), lambda i: (i, 0, 0)),
                  pl.BlockSpec((td, W), lambda i: (i, 0)),
                  pl.BlockSpec(memory_space=pl.ANY)],
        out_specs=pl.BlockSpec(memory_space=pl.ANY),
        scratch_shapes=[pltpu.SMEM((TOP_K * td,), I32), pltpu.SemaphoreType.DMA, pltpu.SemaphoreType.DMA],
        input_output_aliases={2: 0},
        compiler_params=_params(("arbitrary",)),
        name="moe_dispatch",
    )(dest3, hp, xs0)


def _ffn_kernel(te_ref, nv_ref, x_ref, w1_ref, b1_ref, w2_ref, b2_ref, y_ref):
    @pl.when(pl.program_id(0) >= nv_ref[0])
    def _():
        y_ref[...] = jnp.zeros(y_ref.shape, F32)

    @pl.when(pl.program_id(0) < nv_ref[0])
    def _():
        xw = x_ref[...]
        lo = lax.bitcast_convert_type(xw << 16, F32)
        hi = lax.bitcast_convert_type(xw & jnp.uint32(0xFFFF0000), F32)
        xb = jnp.concatenate([lo, hi], axis=1).astype(BF16)
        D = xb.shape[1]
        glu = lax.dot_general(xb, w1_ref[0, :, :D], _NT, preferred_element_type=F32) + b1_ref[0, :, :D_FF]
        lin = lax.dot_general(xb, w1_ref[0, :, D:], _NT, preferred_element_type=F32) + b1_ref[0, :, D_FF:]
        glu = jnp.minimum(glu, SWIGLU_LIMIT)
        lin = jnp.clip(lin, -SWIGLU_LIMIT, SWIGLU_LIMIT)
        a = glu * jax.nn.sigmoid(SWIGLU_ALPHA * glu) * (lin + 1.0)
        y_ref[...] = jnp.dot(a.astype(BF16), w2_ref[0], preferred_element_type=F32) + b2_ref[0]


def _expert_ffn(tile_e, n_valid, xs, w1p, b1p, w2b, b2, tmf):
    n_rows, W = xs.shape
    D = 2 * W
    ntiles = n_rows // tmf
    return pl.pallas_call(
        _ffn_kernel,
        out_shape=jax.ShapeDtypeStruct((n_rows, D), F32),
        grid_spec=pltpu.PrefetchScalarGridSpec(
            num_scalar_prefetch=2,
            grid=(ntiles,),
            in_specs=[pl.BlockSpec((tmf, W), lambda j, te, nv: (jnp.minimum(j, nv[0] - 1), 0)),
                      pl.BlockSpec((1, D_FF, 2 * D), lambda j, te, nv: (te[j], 0, 0)),
                      pl.BlockSpec((1, 1, 2 * D_FF), lambda j, te, nv: (te[j], 0, 0)),
                      pl.BlockSpec((1, D_FF, D), lambda j, te, nv: (te[j], 0, 0)),
                      pl.BlockSpec((1, 1, D), lambda j, te, nv: (te[j], 0, 0))],
            out_specs=pl.BlockSpec((tmf, D), lambda j, te, nv: (j, 0))),
        compiler_params=_params(("arbitrary",)),
        name="moe_ffn",
    )(tile_e, n_valid, xs, w1p, b1p, w2b, b2)


def _combine_kernel(dest_ref, gt_ref, x_ref, mod_ref, ys_ref, o_ref, d_s, buf, sem_i, sem):
    tc = x_ref.shape[0]
    cp = pltpu.make_async_copy(dest_ref.at[0, 0], d_s, sem_i)
    cp.start()
    cp.wait()

    def body(i, c):
        t = i // TOP_K
        k = i % TOP_K
        pltpu.make_async_copy(ys_ref.at[pl.ds(d_s[i], 1)], buf.at[k, pl.ds(t, 1)], sem).start()
        return c

    lax.fori_loop(0, TOP_K * tc, body, 0, unroll=8)
    for k in range(TOP_K):
        pltpu.make_async_copy(ys_ref.at[pl.ds(0, tc)], buf.at[k], sem).wait()
    g = gt_ref[...]
    y = g[:, 0:1] * buf[0]
    for k in range(1, TOP_K):
        y = y + g[:, k:k + 1] * buf[k]
    o_ref[...] = x_ref[...] + mod_ref[0, 5:6, :] * y


def _combine(dest3, gates, x2, mods_l, ys, S):
    T, D = x2.shape
    tc = dest3.shape[2] // TOP_K
    spb = S // tc
    return pl.pallas_call(
        _combine_kernel,
        out_shape=jax.ShapeDtypeStruct((T, D), F32),
        grid=(T // tc,),
        in_specs=[pl.BlockSpec((1, 1, TOP_K * tc), lambda i: (i, 0, 0)),
                  pl.BlockSpec((tc, LANES), lambda i: (i, 0)),
                  pl.BlockSpec((tc, D), lambda i: (i, 0)),
                  pl.BlockSpec((1, 6, D), lambda i: (i // spb, 0, 0)),
                  pl.BlockSpec(memory_space=pl.ANY)],
        out_specs=pl.BlockSpec((tc, D), lambda i: (i, 0)),
        scratch_shapes=[pltpu.SMEM((TOP_K * tc,), I32), pltpu.VMEM((TOP_K, tc, D), F32),
                        pltpu.SemaphoreType.DMA, pltpu.SemaphoreType.DMA],
        compiler_params=_params(("arbitrary",)),
        name="moe_combine",
    )(dest3, gates, x2, mods_l, ys)


def _moe(x2, mods_l, g2, w_router, b_router, w1p, b1p, w2b, b2, S):
    T, D = x2.shape
    tmf = 256
    wr = jnp.zeros((D, LANES), F32).at[:, :N_EXPERTS].set(w_router)
    wr_hi = wr.astype(BF16)
    wr_lo = (wr - wr_hi.astype(F32)).astype(BF16)
    br_p = jnp.full((1, LANES), NEG, F32).at[0, :N_EXPERTS].set(b_router)
    hp, e_idx, ranks, gates, cnt = _route(x2, mods_l, g2, wr_hi, wr_lo, br_p, S)

    counts = cnt[0, :N_EXPERTS].astype(I32)
    padded = ((counts + tmf - 1) // tmf) * tmf
    ends = jnp.cumsum(padded)
    offs = ends - padded
    dest = offs[e_idx[:, :TOP_K]] + ranks[:, :TOP_K]
    n_rows = TOP_K * T + N_EXPERTS * tmf
    ntiles = n_rows // tmf
    n_valid = (ends[-1] // tmf).astype(I32).reshape(1)
    tile_start = jnp.minimum(jnp.arange(ntiles, dtype=I32), n_valid[0] - 1) * tmf
    tile_e = jnp.sum(tile_start[:, None] >= ends[None, :], axis=1).astype(I32)
    td = min(1024, S)
    dest3 = dest.reshape(T // td, 1, TOP_K * td)

    xs = _dispatch(dest3, hp, n_rows)
    ys = _expert_ffn(tile_e, n_valid, xs, w1p, b1p, w2b, b2, tmf)
    return _combine(dest3, gates, x2, mods_l, ys, S)


def _final_kernel(x_ref, g_ref, o_ref):
    o_ref[...] = _rms(x_ref[...], g_ref[...])


def _final_norm(x2, g):
    T, D = x2.shape
    tm = min(512, T)
    return pl.pallas_call(
        _final_kernel,
        out_shape=jax.ShapeDtypeStruct((T, D), F32),
        grid=(T // tm,),
        in_specs=[pl.BlockSpec((tm, D), lambda i: (i, 0)), pl.BlockSpec((1, D), lambda i: (0, 0))],
        out_specs=pl.BlockSpec((tm, D), lambda i: (i, 0)),
        compiler_params=_params(("parallel",)),
        name="final_norm",
    )(x2, g.reshape(1, D))


def _rearrange_w_in(w):
    D = w.shape[0]
    main = w[:, :2048]
    gates = w[:, 2048:2072]
    cq = w[:, 2072:2328]
    ckv = w[:, 2328:2456]
    kr = w[:, 2456:2488]
    z = lambda n: jnp.zeros((D, n), w.dtype)
    misc = jnp.concatenate([gates, z(KR_LANE - 24), kr, z(LANES - KR_LANE - QK_ROPE)], axis=1)
    return jnp.concatenate([main, cq, ckv, misc], axis=1).astype(BF16)


def _mla_weights(w_q_up, w_kv_up):
    dq = QK_NOPE + QK_ROPE
    wq = jnp.zeros((Q_LORA, C_HEADS * LANES), F32)
    wk = jnp.zeros((KV_LORA, C_HEADS * LANES), F32)
    wv = []
    for h in range(C_HEADS):
        wq = wq.at[:, h * LANES:h * LANES + dq].set(w_q_up[:, h * dq:(h + 1) * dq])
        kv = w_kv_up[:, h * (QK_NOPE + HEAD):(h + 1) * (QK_NOPE + HEAD)]
        wk = wk.at[:, h * LANES:h * LANES + QK_NOPE].set(kv[:, :QK_NOPE])
        wv.append(kv[:, QK_NOPE:])
    return wq.astype(BF16), wk.astype(BF16), jnp.concatenate(wv, axis=1).astype(BF16)


def kernel(x, c, positions, w_ada, b_ada, g_norm1, g_norm2, w_in, nsa_pos_k, nsa_pos_v, nsa_w1_k, nsa_b1_k, nsa_w2_k, nsa_b2_k, nsa_w1_v, nsa_b1_v, nsa_w2_v, nsa_b2_v, mla_g_q, mla_g_kv, mla_w_q_up, mla_w_kv_up, g_out_a, g_out_b, g_out_c, w_out, w_router, b_router, w_exp1, b_exp1, w_exp2, b_exp2, g_final):
    B, S, D = x.shape
    T = B * S
    depth = w_ada.shape[0]
    x2 = x.reshape(T, D)

    cab, sab, cc, sc = _rope_tables(positions.reshape(T, 1))
    nch = S // CMP_STRIDE
    n_cmp = (S - CMP_LEN) // CMP_STRIDE + 1
    end_idx = np.minimum(np.arange(nch) * CMP_STRIDE + CMP_LEN - 1, S - 1)
    cos_e = cab.reshape(B, S, LANES)[:, end_idx]
    sin_e = sab.reshape(B, S, LANES)[:, end_idx]
    mods = _ada_mods(c, w_ada, b_ada).reshape(depth, B, 6, D)
    del n_cmp

    for l in range(depth):
        mods_l = mods[l]
        (qa, ka, va, qb, kcmp, vcmp, kslc, vslc, kwin, vwin, cq, ckv, misc) = _in_proj(
            x2, mods_l, g_norm1[l].reshape(1, D), _rearrange_w_in(w_in[l]), cab, sab, S)

        oa_parts, lse_parts = [], []
        for window, d in A_PAIRS:
            o, lse = _dilated_attention(qa, ka, va, B, S, d, window // d)
            oa_parts.append(o)
            lse_parts.append(lse)

        kc = _compress(kcmp, nsa_pos_k[l], nsa_w1_k[l], nsa_b1_k[l], nsa_w2_k[l], nsa_b2_k[l],
                       cos_e, sin_e, B, S, True)
        vc = _compress(vcmp, nsa_pos_v[l], nsa_w1_v[l], nsa_b1_v[l], nsa_w2_v[l], nsa_b2_v[l],
                       cos_e, sin_e, B, S, False)
        kcc = jnp.concatenate([kc[:, 0, :, :HEAD], kc[:, 1, :, :HEAD]], axis=-1).astype(BF16)
        vcc = jnp.concatenate([vc[:, 0, :, :HEAD], vc[:, 1, :, :HEAD]], axis=-1).astype(BF16)
        ocmp, sel_t = _cmp_attention(qb, kcc, vcc, B, S)
        oslc = _select_attention(qb, kslc, vslc, sel_t, B, S)
        owin = _window_attention(qb, kwin, vwin, B, S)

        wq_p, wk_p, wv_p = _mla_weights(mla_w_q_up[l], mla_w_kv_up[l])
        qc, kcl, vcl = _mla_prep(cq, ckv, misc, mla_g_q[l].reshape(1, Q_LORA), mla_g_kv[l].reshape(1, KV_LORA),
                                 wq_p, wk_p, wv_p, cc, sc)
        oc = _latent_attention(qc, kcl, vcl, B, S)

        x2 = _out_proj(x2, mods_l, oa_parts, lse_parts, ocmp, oslc, owin, misc, oc,
                       g_out_a[l].reshape(1, A_W), g_out_b[l].reshape(1, B_W), g_out_c[l].reshape(1, C_W),
                       w_out[l].astype(BF16), S)

        w1p = jnp.transpose(w_exp1[l], (0, 2, 1)).astype(BF16).reshape(N_EXPERTS, D_FF, 2 * D)
        b1r = b_exp1[l].reshape(N_EXPERTS, D_FF, 2)
        b1p = jnp.concatenate([b1r[..., 0], b1r[..., 1]], axis=-1).reshape(N_EXPERTS, 1, 2 * D_FF)
        x2 = _moe(x2, mods_l, g_norm2[l].reshape(1, D), w_router[l], b_router[l],
                  w1p, b1p, w_exp2[l].astype(BF16), b_exp2[l].reshape(N_EXPERTS, 1, D), S)

    return _final_norm(x2, g_final).reshape(B, S, D)
```

```python
import functools

import numpy as np
import jax
import jax.numpy as jnp
from jax import lax
from jax.experimental import pallas as pl
from jax.experimental.pallas import tpu as pltpu
from jax.experimental.pallas import tpu_sc as plsc

F32 = jnp.float32
BF16 = jnp.bfloat16
I32 = jnp.int32
U32 = jnp.uint32

LANES = 128
HEAD = 64
NEG = -1e30
LOWEST = -3.0e38
NORM_EPS = 1e-6
ROPE_THETA = 10000.0

A_HEADS, B_HEADS, B_GROUPS, C_HEADS = 4, 8, 2, 4
A_PAIRS = ((128, 1), (512, 4), (2048, 16))
A_W, B_W, C_W = 256, 512, 256
CMP_LEN, CMP_STRIDE, CMP_HIDDEN = 32, 16, 256
SLC_BLOCK, N_SELECT, WINDOW = 64, 16, 512
FORCE_BONUS = 1e4
Q_LORA, KV_LORA, QK_NOPE, QK_ROPE = 256, 128, 64, 32
N_EXPERTS, TOP_K, D_FF = 32, 4, 1024
SWIGLU_ALPHA, SWIGLU_LIMIT = 1.702, 7.0
IN_COLS = 2560
KR_LANE = 64

VMEM_LIMIT = 56 * 1024 * 1024


def _params(sem, vmem=VMEM_LIMIT):
    return pltpu.CompilerParams(dimension_semantics=sem, vmem_limit_bytes=vmem)


def _rms(x, g):
    return x * lax.rsqrt(jnp.mean(x * x, axis=-1, keepdims=True) + NORM_EPS) * g


def _rope_rows():
    inv64 = ROPE_THETA ** (-np.arange(32, dtype=np.float32) * (2.0 / 64))
    inv32 = ROPE_THETA ** (-np.arange(16, dtype=np.float32) * (2.0 / 32))
    inv_ab = np.tile(np.concatenate([inv64, inv64]), 2)
    sgn_ab = np.tile(np.concatenate([-np.ones(32), np.ones(32)]), 2)
    inv_c = np.zeros(LANES, np.float32)
    sgn_c = np.zeros(LANES, np.float32)
    inv_c[KR_LANE:KR_LANE + 16] = inv32
    inv_c[KR_LANE + 16:KR_LANE + 32] = inv32
    sgn_c[KR_LANE:KR_LANE + 16] = -1.0
    sgn_c[KR_LANE + 16:KR_LANE + 32] = 1.0
    rows = np.stack([inv_ab, sgn_ab, inv_c, sgn_c]).astype(np.float32)
    return jnp.asarray(rows)


def _rope_tab_kernel(pos_ref, rows_ref, cab_ref, sab_ref, cc_ref, sc_ref):
    pos = pos_ref[...].astype(F32)
    ang = pos * rows_ref[0:1, :]
    cab_ref[...] = jnp.cos(ang)
    sab_ref[...] = jnp.sin(ang) * rows_ref[1:2, :]
    ang = pos * rows_ref[2:3, :]
    cc_ref[...] = jnp.cos(ang)
    sc_ref[...] = jnp.sin(ang) * rows_ref[3:4, :]


def _rope_tables(pos_col):
    R = pos_col.shape[0]
    tm = min(R, 1024)
    spec = pl.BlockSpec((tm, LANES), lambda i: (i, 0))
    return pl.pallas_call(
        _rope_tab_kernel,
        out_shape=[jax.ShapeDtypeStruct((R, LANES), F32)] * 4,
        grid=(R // tm,),
        in_specs=[pl.BlockSpec((tm, 1), lambda i: (i, 0)), pl.BlockSpec((4, LANES), lambda i: (0, 0))],
        out_specs=[spec] * 4,
        compiler_params=_params(("parallel",)),
        name="rope_tables",
    )(pos_col, _rope_rows())


def _swap32(t, lane):
    return jnp.where((lane & 32) == 0, pltpu.roll(t, 96, 1), pltpu.roll(t, 32, 1))


def _swap16(t, lane):
    return jnp.where(lane < KR_LANE + 16, pltpu.roll(t, 112, 1), pltpu.roll(t, 16, 1))


def _mods_kernel(c_ref, w_ref, b_ref, o_ref):
    c = c_ref[...]
    ca = (c * jax.nn.sigmoid(c)).astype(BF16)
    o_ref[0] = jnp.dot(ca, w_ref[0].astype(BF16), preferred_element_type=F32) + b_ref[0]


def _ada_mods(c, w_ada, b_ada):
    L, D, N = w_ada.shape
    B = c.shape[0]
    tn = 1536
    return pl.pallas_call(
        _mods_kernel,
        out_shape=jax.ShapeDtypeStruct((L, B, N), F32),
        grid=(L, N // tn),
        in_specs=[pl.BlockSpec((B, D), lambda l, j: (0, 0)),
                  pl.BlockSpec((1, D, tn), lambda l, j: (l, 0, j)),
                  pl.BlockSpec((1, 1, tn), lambda l, j: (l, 0, j))],
        out_specs=pl.BlockSpec((1, B, tn), lambda l, j: (l, 0, j)),
        compiler_params=_params(("parallel", "parallel")),
        name="ada_mods",
    )(c, w_ada, b_ada.reshape(L, 1, N))


def _inproj_kernel(x_ref, mod_ref, g_ref, w_ref, cab_ref, sab_ref,
                   qa_ref, ka_ref, va_ref, qb_ref, kcmp_ref, vcmp_ref, kslc_ref, vslc_ref,
                   kwin_ref, vwin_ref, cq_ref, ckv_ref, misc_ref):
    tm = x_ref.shape[0]
    x = x_ref[...]
    h = (_rms(x, g_ref[...]) * (1.0 + mod_ref[0, 1:2, :]) + mod_ref[0, 0:1, :]).astype(BF16)
    cos = cab_ref[...]
    sin = sab_ref[...]
    lane = lax.broadcasted_iota(I32, (tm, LANES), 1)
    lo64 = lane < HEAD

    def proj(c0, n):
        return jnp.dot(h, w_ref[:, c0:c0 + n], preferred_element_type=F32)

    def rope(t):
        return t * cos + _swap32(t, lane) * sin

    def put_q(ref, h0, t, dst):
        t = rope(t) * (HEAD ** -0.5)
        tr = pltpu.roll(t, HEAD, 1)
        for i in range(2):
            src = t if dst[i] == i else tr
            ref[h0 + i] = jnp.where(lo64 if dst[i] == 0 else ~lo64, src, 0.0).astype(BF16)

    y = proj(0, 256)
    put_q(qa_ref, 0, y[:, :128], (0, 1))
    put_q(qa_ref, 2, y[:, 128:], (0, 1))
    y = proj(256, 256)
    ka_ref[:, 0:128] = rope(y[:, :128]).astype(BF16)
    ka_ref[:, 128:256] = rope(y[:, 128:]).astype(BF16)
    va_ref[...] = proj(512, 256).astype(BF16)
    for j in range(2):
        y = proj(768 + 256 * j, 256)
        put_q(qb_ref, 4 * j, y[:, :128], (j, j))
        put_q(qb_ref, 4 * j + 2, y[:, 128:], (j, j))
    y = proj(1280, 256)
    kcmp_ref[...] = y[:, :128]
    vcmp_ref[...] = y[:, 128:]
    y = proj(1536, 256)
    kslc_ref[...] = rope(y[:, :128]).astype(BF16)
    vslc_ref[...] = y[:, 128:].astype(BF16)
    y = proj(1792, 256)
    kwin_ref[...] = rope(y[:, :128]).astype(BF16)
    vwin_ref[...] = y[:, 128:].astype(BF16)
    cq_ref[...] = proj(2048, 256)
    y = proj(2304, 256)
    ckv_ref[...] = y[:, :128]
    misc_ref[...] = y[:, 128:]


def _in_proj(x2, mods_l, g1, w_in_r, cab, sab, S):
    T, D = x2.shape
    tm = min(512, S)
    spb = S // tm
    row = lambda i: (i, 0)
    shp = jax.ShapeDtypeStruct
    outs = [shp((A_HEADS, T, LANES), BF16), shp((T, A_W), BF16), shp((T, A_W), BF16),
            shp((B_HEADS, T, LANES), BF16),
            shp((T, LANES), F32), shp((T, LANES), F32), shp((T, LANES), BF16), shp((T, LANES), BF16),
            shp((T, LANES), BF16), shp((T, LANES), BF16),
            shp((T, Q_LORA), F32), shp((T, LANES), F32), shp((T, LANES), F32)]
    t128 = pl.BlockSpec((tm, LANES), row)
    t256 = pl.BlockSpec((tm, 256), row)
    out_specs = [pl.BlockSpec((A_HEADS, tm, LANES), lambda i: (0, i, 0)), t256, t256,
                 pl.BlockSpec((B_HEADS, tm, LANES), lambda i: (0, i, 0)),
                 t128, t128, t128, t128, t128, t128, t256, t128, t128]
    return pl.pallas_call(
        _inproj_kernel,
        out_shape=outs,
        grid=(T // tm,),
        in_specs=[pl.BlockSpec((tm, D), row),
                  pl.BlockSpec((1, 6, D), lambda i: (i // spb, 0, 0)),
                  pl.BlockSpec((1, D), lambda i: (0, 0)),
                  pl.BlockSpec((D, IN_COLS), lambda i: (0, 0)),
                  t128, t128],
        out_specs=out_specs,
        compiler_params=_params(("parallel",)),
        name="in_proj",
    )(x2, mods_l, g1, w_in_r, cab, sab)


_NT = (((1,), (1,)), ((), ()))
_TN = (((0,), (0,)), ((), ()))


KEY_SUB = 128
QUERY_SUB = 256


def _online_chunk(k, v, q, keep, bias, m, l, acc):
    s = lax.dot_general(k, q, _NT, preferred_element_type=F32)
    if bias is not None:
        s = s + bias
    s = jnp.where(keep, s, NEG)
    m_next = jnp.maximum(m, s.max(axis=0, keepdims=True))
    alpha = jnp.exp(m - m_next)
    p = jnp.exp(s - m_next)
    l = alpha * l + p.sum(axis=0, keepdims=True)
    acc = acc * alpha + lax.dot_general(v, p.astype(BF16), _TN, preferred_element_type=F32)
    return m_next, l, acc


def _mask_heads(s, masks, tq, fill):
    return jnp.concatenate([jnp.where(masks[h], s[:, h * tq:(h + 1) * tq], fill)
                            for h in range(len(masks))], axis=1)


def _emit_heads(o_ref, acc_t, halves, tq):
    nh = len(halves)
    lo64 = lax.broadcasted_iota(I32, (tq, LANES), 1) < HEAD
    for c in range(nh // 2):
        parts = []
        for h in (2 * c, 2 * c + 1):
            o = acc_t[:, h * tq:(h + 1) * tq].T
            if halves[h] != h % 2:
                o = pltpu.roll(o, HEAD, 1)
            parts.append(o)
        o_ref[:, c * LANES:(c + 1) * LANES] = jnp.where(lo64, parts[0], parts[1])


def _band_kernel(*refs, nblk, tq, tk, max_dist, stacks, halves, with_lse, qi_axis):
    q_ref = refs[0]
    k_refs = refs[1:1 + nblk]
    v_refs = refs[1 + nblk:1 + 2 * nblk]
    o_ref = refs[1 + 2 * nblk]
    lse_ref = refs[2 + 2 * nblk] if with_lse else None
    nh = len(halves)
    qi = pl.program_id(qi_axis)
    ks = min(KEY_SUB, tk)
    cq = min(QUERY_SUB, tq)
    rc = lax.broadcasted_iota(I32, (ks, cq), 0) - lax.broadcasted_iota(I32, (ks, cq), 1)
    head_blk = {}
    for (h0, n, blk) in stacks:
        for h in range(h0, h0 + n):
            head_blk[h] = blk
    outs, lses = [], []
    for h in range(nh):
        lanes = slice(head_blk[h] * LANES, (head_blk[h] + 1) * LANES)
        for c in range(tq // cq):
            q = q_ref[h, c * cq:(c + 1) * cq, :]
            m = jnp.full((1, cq), NEG, F32)
            l = jnp.zeros((1, cq), F32)
            acc = jnp.zeros((LANES, cq), F32)
            for i in range(nblk):
                kt = qi * (tq // tk) - (nblk - tq // tk) + i
                for kk in range(tk // ks):
                    base = qi * tq + c * cq - (kt * tk + kk * ks)
                    base = jnp.where(kt >= 0, base, -(1 << 20))
                    keep = (rc <= base) & (rc >= base - max_dist)
                    m, l, acc = _online_chunk(k_refs[i][kk * ks:(kk + 1) * ks, lanes],
                                              v_refs[i][kk * ks:(kk + 1) * ks, lanes], q, keep, None, m, l, acc)
            outs.append(acc / l)
            lses.append(jnp.broadcast_to(m + jnp.log(l), (LANES, cq)))
    _emit_heads(o_ref, jnp.concatenate(outs, axis=1), halves, tq)
    if with_lse:
        _emit_heads(lse_ref, jnp.concatenate(lses, axis=1), tuple(h % 2 for h in range(nh)), tq)


def _dilated_attention(qa, ka, va, B, S, d, max_dist):
    L = S // d
    tq = min(256, L)
    tk = 128
    nprev = -(-max_dist // tk)
    nblk = nprev + tq // tk
    q_v = qa.reshape(A_HEADS, B, L, d * LANES)
    k_v = ka.reshape(B, L, d * A_W)
    v_v = va.reshape(B, L, d * A_W)
    kv_specs = []
    for i in range(nblk):
        kv_specs.append(pl.BlockSpec(
            (None, tk, A_W),
            functools.partial(lambda b, r, qi, i: (b, jnp.maximum(qi * (tq // tk) - nprev + i, 0), r), i=i)))
    out_spec = pl.BlockSpec((None, tq, A_W), lambda b, r, qi: (b, qi, r))
    kern = functools.partial(_band_kernel, nblk=nblk, tq=tq, tk=tk, max_dist=max_dist,
                             stacks=((0, 2, 0), (2, 2, 1)), halves=(0, 1, 0, 1), with_lse=True, qi_axis=2)
    o, lse = pl.pallas_call(
        kern,
        out_shape=[jax.ShapeDtypeStruct((B, L, d * A_W), F32)] * 2,
        grid=(B, d, L // tq),
        in_specs=[pl.BlockSpec((A_HEADS, None, tq, LANES), lambda b, r, qi: (0, b, qi, r))]
                 + kv_specs + kv_specs,
        out_specs=[out_spec, out_spec],
        compiler_params=_params(("parallel", "parallel", "parallel")),
        name=f"dilated_attn_d{d}",
    )(q_v, *([k_v] * nblk), *([v_v] * nblk))
    return o.reshape(B * S, A_W), lse.reshape(B * S, A_W)


def _window_attention(qb, kwin, vwin, B, S):
    tq = min(256, S)
    tk = tq
    max_dist = WINDOW - 1
    nprev = -(-max_dist // tk)
    nblk = nprev + 1
    q_v = qb.reshape(B_HEADS, B, S, LANES)
    k_v = kwin.reshape(B, S, LANES)
    v_v = vwin.reshape(B, S, LANES)
    kv_specs = [pl.BlockSpec((None, tk, LANES),
                             functools.partial(lambda b, qi, i: (b, jnp.maximum(qi - nprev + i, 0), 0), i=i))
                for i in range(nblk)]
    kern = functools.partial(_band_kernel, nblk=nblk, tq=tq, tk=tk, max_dist=max_dist,
                             stacks=((0, B_HEADS, 0),), halves=(0, 0, 0, 0, 1, 1, 1, 1),
                             with_lse=False, qi_axis=1)
    o = pl.pallas_call(
        kern,
        out_shape=jax.ShapeDtypeStruct((B, S, B_W), F32),
        grid=(B, S // tq),
        in_specs=[pl.BlockSpec((B_HEADS, None, tq, LANES), lambda b, qi: (0, b, qi, 0))]
                 + kv_specs + kv_specs,
        out_specs=pl.BlockSpec((None, tq, B_W), lambda b, qi: (b, qi, 0)),
        compiler_params=_params(("parallel", "parallel")),
        name="window_attn",
    )(q_v, *([k_v] * nblk), *([v_v] * nblk))
    return o.reshape(B * S, B_W)


def _flash_kernel(qi_ref, kj_ref, first_ref, last_ref, *refs, tq, tk, qk_groups, pv_groups, halves, select):
    if select:
        q_ref, k_ref, v_ref, sel_ref, o_ref, m_sc, l_sc, acc_sc = refs
    else:
        q_ref, k_ref, v_ref, o_ref, m_sc, l_sc, acc_sc = refs
    nh = len(halves)
    step = pl.program_id(1)
    qi = qi_ref[step]
    kj = kj_ref[step]

    @pl.when(first_ref[step] == 1)
    def _():
        m_sc[...] = jnp.full(m_sc.shape, NEG, F32)
        l_sc[...] = jnp.zeros(l_sc.shape, F32)
        acc_sc[...] = jnp.zeros(acc_sc.shape, F32)

    ks = min(KEY_SUB, tk)
    cq = min(QUERY_SUB, tq)
    rc = lax.broadcasted_iota(I32, (ks, cq), 0) - lax.broadcasted_iota(I32, (ks, cq), 1)
    head_kb, head_vb = {}, {}
    for (h0, n, kb) in qk_groups:
        for h in range(h0, h0 + n):
            head_kb[h] = kb
    for (h0, n, vb) in pv_groups:
        for h in range(h0, h0 + n):
            head_vb[h] = vb
    hpg = nh // B_GROUPS
    spb = ks // SLC_BLOCK if select else 1
    for h in range(nh):
        klanes = slice(head_kb[h] * LANES, (head_kb[h] + 1) * LANES)
        vlanes = slice(head_vb[h] * LANES, (head_vb[h] + 1) * LANES)
        for c in range(tq // cq):
            cols = slice(h * tq + c * cq, h * tq + (c + 1) * cq)
            q = q_ref[h, c * cq:(c + 1) * cq, :]
            m, l, acc = m_sc[:, cols], l_sc[:, cols], acc_sc[:, cols]
            for kk in range(tk // ks):
                keep = rc <= (qi * tq + c * cq) - (kj * tk + kk * ks)
                bias = None
                if select:
                    rows = [jnp.broadcast_to(
                        (sel_ref[h // hpg, pl.ds((kj * (tk // ks) + kk) * spb + j, 1), c * cq:(c + 1) * cq] - 1.0)
                        * (-NEG), (SLC_BLOCK, cq)) for j in range(spb)]
                    bias = rows[0] if spb == 1 else jnp.concatenate(rows, axis=0)
                m, l, acc = _online_chunk(k_ref[kk * ks:(kk + 1) * ks, klanes], v_ref[kk * ks:(kk + 1) * ks, vlanes],
                                          q, keep, bias, m, l, acc)
            m_sc[:, cols] = m
            l_sc[:, cols] = l
            acc_sc[:, cols] = acc

    @pl.when(last_ref[step] == 1)
    def _():
        _emit_heads(o_ref, acc_sc[...] / l_sc[...], halves, tq)


def _causal_schedule(S, tq, tk):
    qi, kj, first, last = [], [], [], []
    for i in range(S // tq):
        nk = (i * tq + tq - 1) // tk + 1
        for j in range(nk):
            qi.append(i)
            kj.append(j)
            first.append(1 if j == 0 else 0)
            last.append(1 if j == nk - 1 else 0)
    return [jnp.asarray(np.asarray(a, np.int32)) for a in (qi, kj, first, last)]


def _select_attention(qb, kslc, vslc, sel_t, B, S):
    tq = min(512, S)
    tk = min(256, S)
    n_slc = S // SLC_BLOCK
    tabs = _causal_schedule(S, tq, tk)
    nsteps = tabs[0].shape[0]
    kern = functools.partial(_flash_kernel, tq=tq, tk=tk, qk_groups=((0, B_HEADS, 0),),
                             pv_groups=((0, B_HEADS, 0),), halves=(0, 0, 0, 0, 1, 1, 1, 1), select=True)
    o = pl.pallas_call(
        kern,
        out_shape=jax.ShapeDtypeStruct((B, S, B_W), F32),
        grid_spec=pltpu.PrefetchScalarGridSpec(
            num_scalar_prefetch=4,
            grid=(B, nsteps),
            in_specs=[pl.BlockSpec((B_HEADS, None, tq, LANES), lambda b, s, qi, kj, f, l: (0, b, qi[s], 0)),
                      pl.BlockSpec((None, tk, LANES), lambda b, s, qi, kj, f, l: (b, kj[s], 0)),
                      pl.BlockSpec((None, tk, LANES), lambda b, s, qi, kj, f, l: (b, kj[s], 0)),
                      pl.BlockSpec((None, B_GROUPS, n_slc, tq), lambda b, s, qi, kj, f, l: (b, 0, 0, qi[s]))],
            out_specs=pl.BlockSpec((None, tq, B_W), lambda b, s, qi, kj, f, l: (b, qi[s], 0)),
            scratch_shapes=[pltpu.VMEM((1, B_HEADS * tq), F32), pltpu.VMEM((1, B_HEADS * tq), F32),
                            pltpu.VMEM((LANES, B_HEADS * tq), F32)]),
        compiler_params=_params(("parallel", "arbitrary")),
        name="select_attn",
    )(*tabs, qb.reshape(B_HEADS, B, S, LANES), kslc.reshape(B, S, LANES), vslc.reshape(B, S, LANES), sel_t)
    return o.reshape(B * S, B_W)


def _latent_attention(qc, kc, vc, B, S):
    tq = min(512, S)
    tk = min(512, S)
    tabs = _causal_schedule(S, tq, tk)
    nsteps = tabs[0].shape[0]
    kern = functools.partial(_flash_kernel, tq=tq, tk=tk,
                             qk_groups=tuple((h, 1, h) for h in range(C_HEADS)),
                             pv_groups=((0, 2, 0), (2, 2, 1)), halves=(0, 1, 0, 1), select=False)
    o = pl.pallas_call(
        kern,
        out_shape=jax.ShapeDtypeStruct((B, S, C_W), F32),
        grid_spec=pltpu.PrefetchScalarGridSpec(
            num_scalar_prefetch=4,
            grid=(B, nsteps),
            in_specs=[pl.BlockSpec((C_HEADS, None, tq, LANES), lambda b, s, qi, kj, f, l: (0, b, qi[s], 0)),
                      pl.BlockSpec((None, tk, C_HEADS * LANES), lambda b, s, qi, kj, f, l: (b, kj[s], 0)),
                      pl.BlockSpec((None, tk, C_W), lambda b, s, qi, kj, f, l: (b, kj[s], 0))],
            out_specs=pl.BlockSpec((None, tq, C_W), lambda b, s, qi, kj, f, l: (b, qi[s], 0)),
            scratch_shapes=[pltpu.VMEM((1, C_HEADS * tq), F32), pltpu.VMEM((1, C_HEADS * tq), F32),
                            pltpu.VMEM((LANES, C_HEADS * tq), F32)]),
        compiler_params=_params(("parallel", "arbitrary")),
        name="latent_attn",
    )(*tabs, qc.reshape(C_HEADS, B, S, LANES), kc.reshape(B, S, C_HEADS * LANES), vc.reshape(B, S, C_W))
    return o.reshape(B * S, C_W)


def _gelu_tanh(x):
    return 0.5 * x * (1.0 + jnp.tanh(0.7978845608028654 * (x + 0.044715 * x * x * x)))


def _compress_kernel(u_ref, pos_ref, w1_ref, b1_ref, w2_ref, b2_ref, cos_ref, sin_ref, o_ref, *, rotate):
    n = u_ref.shape[0]
    half = CMP_STRIDE * HEAD
    u = u_ref[...]
    ua = (u + pos_ref[:, :half]).astype(BF16)
    ub = (u + pos_ref[:, half:]).astype(BF16)
    pa = jnp.dot(ua, w1_ref[:half, :].astype(BF16), preferred_element_type=F32)
    pb = jnp.dot(ub, w1_ref[half:, :].astype(BF16), preferred_element_type=F32)
    hid = _gelu_tanh(pa + pltpu.roll(pb, n - 1, 0) + b1_ref[...])
    y = jnp.dot(hid.astype(BF16), w2_ref[...].astype(BF16), preferred_element_type=F32) + b2_ref[...]
    if rotate:
        lane = lax.broadcasted_iota(I32, y.shape, 1)
        y = y * cos_ref[...] + _swap32(y, lane) * sin_ref[...]
    o_ref[...] = y


def _compress(t, pos_emb, w1, b1, w2, b2, cos_e, sin_e, B, S, rotate):
    nch = S // CMP_STRIDE
    u = t.reshape(B, S, B_GROUPS, HEAD).transpose(0, 2, 1, 3).reshape(B, B_GROUPS, nch, CMP_STRIDE * HEAD)
    w2d = jnp.concatenate([w2, w2], axis=1)
    b2d = jnp.concatenate([b2, b2]).reshape(1, LANES)
    kern = functools.partial(_compress_kernel, rotate=rotate)
    tab = pl.BlockSpec((None, nch, LANES), lambda b, g: (b, 0, 0))
    return pl.pallas_call(
        kern,
        out_shape=jax.ShapeDtypeStruct((B, B_GROUPS, nch, LANES), F32),
        grid=(B, B_GROUPS),
        in_specs=[pl.BlockSpec((None, None, nch, CMP_STRIDE * HEAD), lambda b, g: (b, g, 0, 0)),
                  pl.BlockSpec((1, CMP_LEN * HEAD), lambda b, g: (0, 0)),
                  pl.BlockSpec((CMP_LEN * HEAD, CMP_HIDDEN), lambda b, g: (0, 0)),
                  pl.BlockSpec((1, CMP_HIDDEN), lambda b, g: (0, 0)),
                  pl.BlockSpec((CMP_HIDDEN, LANES), lambda b, g: (0, 0)),
                  pl.BlockSpec((1, LANES), lambda b, g: (0, 0)),
                  tab, tab],
        out_specs=pl.BlockSpec((None, None, nch, LANES), lambda b, g: (b, g, 0, 0)),
        compiler_params=_params(("parallel", "parallel")),
        name="nsa_compress",
    )(u, pos_emb.reshape(1, CMP_LEN * HEAD), w1, b1.reshape(1, CMP_HIDDEN), w2d, b2d, cos_e, sin_e)


def _cmp_attn_kernel(q_ref, kc_ref, vc_ref, ov_ref, o_ref, sel_ref, *, tq, n_cmp, n_slc):
    qi = pl.program_id(1)
    nk = kc_ref.shape[0]
    q = q_ref[...].reshape(B_HEADS * tq, LANES)
    s = lax.dot_general(kc_ref[...], q, _NT, preferred_element_type=F32)
    t_idx = qi * tq + lax.broadcasted_iota(I32, (nk, tq), 1)
    c_idx = lax.broadcasted_iota(I32, (nk, tq), 0)
    mask = ((c_idx * CMP_STRIDE + (CMP_LEN - 1)) <= t_idx) & (c_idx < n_cmp)
    masks = [mask] * B_HEADS
    s = _mask_heads(s, masks, tq, NEG)
    m = s.max(axis=0, keepdims=True)
    e = _mask_heads(jnp.exp(s - m), masks, tq, 0.0)
    den = e.sum(axis=0, keepdims=True)
    p = e / jnp.where(den > 0, den, 1.0)
    o_t = lax.dot_general(vc_ref[...], p.astype(BF16), _TN, preferred_element_type=F32)
    hpg = B_HEADS // B_GROUPS
    _emit_heads(o_ref, o_t, tuple(h // hpg for h in range(B_HEADS)), tq)

    j_idx = lax.broadcasted_iota(I32, (n_slc, tq), 0)
    t_col = qi * tq + lax.broadcasted_iota(I32, (n_slc, tq), 1)
    cur = t_col // SLC_BLOCK
    forced = (j_idx == 0) | (j_idx == cur) | (j_idx == cur - 1)
    valid = j_idx * SLC_BLOCK <= t_col
    for g in range(B_GROUPS):
        ps = p[:, g * hpg * tq:(g * hpg + 1) * tq]
        for h in range(g * hpg + 1, (g + 1) * hpg):
            ps = ps + p[:, h * tq:(h + 1) * tq]
        hi = ps.astype(BF16)
        lo = (ps - hi.astype(F32)).astype(BF16)
        imp = (jnp.dot(ov_ref[...], hi, preferred_element_type=F32)
               + jnp.dot(ov_ref[...], lo, preferred_element_type=F32))
        score = jnp.where(forced, imp + FORCE_BONUS, jnp.where(valid, imp, -1.0))
        sel = jnp.zeros((n_slc, tq), F32)
        j_f = j_idx.astype(F32)
        for _ in range(min(N_SELECT, n_slc)):
            mx = score.max(axis=0, keepdims=True)
            idx = jnp.where(score == mx, j_f, float(n_slc)).min(axis=0, keepdims=True)
            pick = j_f == idx
            sel = jnp.where(pick, 1.0, sel)
            score = jnp.where(pick, LOWEST, score)
        sel_ref[g] = sel


def _cmp_attention(qb, kcc, vcc, B, S):
    tq = min(256, S)
    nk = S // CMP_STRIDE
    n_cmp = (S - CMP_LEN) // CMP_STRIDE + 1
    n_slc = S // SLC_BLOCK
    cmp_tok = np.arange(n_cmp)[:, None] * CMP_STRIDE + np.arange(CMP_LEN)[None, :]
    overlap = ((cmp_tok[:, :, None] // SLC_BLOCK) == np.arange(n_slc)[None, None, :]).mean(axis=1)
    ov_t = np.zeros((n_slc, nk), np.float32)
    ov_t[:, :n_cmp] = overlap.T
    kern = functools.partial(_cmp_attn_kernel, tq=tq, n_cmp=n_cmp, n_slc=n_slc)
    o, sel = pl.pallas_call(
        kern,
        out_shape=[jax.ShapeDtypeStruct((B, S, B_W), F32),
                   jax.ShapeDtypeStruct((B, B_GROUPS, n_slc, S), F32)],
        grid=(B, S // tq),
        in_specs=[pl.BlockSpec((B_HEADS, None, tq, LANES), lambda b, qi: (0, b, qi, 0)),
                  pl.BlockSpec((None, nk, LANES), lambda b, qi: (b, 0, 0)),
                  pl.BlockSpec((None, nk, LANES), lambda b, qi: (b, 0, 0)),
                  pl.BlockSpec((n_slc, nk), lambda b, qi: (0, 0))],
        out_specs=[pl.BlockSpec((None, tq, B_W), lambda b, qi: (b, qi, 0)),
                   pl.BlockSpec((None, B_GROUPS, n_slc, tq), lambda b, qi: (b, 0, 0, qi))],
        compiler_params=_params(("parallel", "parallel")),
        name="cmp_attn",
    )(qb.reshape(B_HEADS, B, S, LANES), kcc, vcc, jnp.asarray(ov_t, BF16))
    return o.reshape(B * S, B_W), sel


def _mla_prep_kernel(cq_ref, ckv_ref, misc_ref, gq_ref, gkv_ref, wq_ref, wk_ref, wv_ref, cc_ref, sc_ref,
                     q_ref, k_ref, v_ref):
    tm = cq_ref.shape[0]
    lane = lax.broadcasted_iota(I32, (tm, LANES), 1)
    cos = cc_ref[...]
    sin = sc_ref[...]

    def rope(t):
        return t * cos + _swap16(t, lane) * sin

    nq = _rms(cq_ref[...], gq_ref[...]).astype(BF16)
    q = jnp.dot(nq, wq_ref[...], preferred_element_type=F32)
    scale = (QK_NOPE + QK_ROPE) ** -0.5
    for h in range(C_HEADS):
        q_ref[h] = (rope(q[:, h * LANES:(h + 1) * LANES]) * scale).astype(BF16)
    nkv = _rms(ckv_ref[...], gkv_ref[...]).astype(BF16)
    k = jnp.dot(nkv, wk_ref[...], preferred_element_type=F32)
    in_rope = (lane >= KR_LANE) & (lane < KR_LANE + QK_ROPE)
    kr = jnp.where(in_rope, rope(misc_ref[...]), 0.0)
    for h in range(C_HEADS):
        k_ref[:, h * LANES:(h + 1) * LANES] = (k[:, h * LANES:(h + 1) * LANES] + kr).astype(BF16)
    v_ref[...] = jnp.dot(nkv, wv_ref[...], preferred_element_type=F32).astype(BF16)


def _mla_prep(cq, ckv, misc, g_q, g_kv, wq_p, wk_p, wv_p, cc, sc):
    T = cq.shape[0]
    tm = min(512, T)
    row = lambda i: (i, 0)
    fix = lambda i: (0, 0)
    t128 = pl.BlockSpec((tm, LANES), row)
    return pl.pallas_call(
        _mla_prep_kernel,
        out_shape=[jax.ShapeDtypeStruct((C_HEADS, T, LANES), BF16),
                   jax.ShapeDtypeStruct((T, C_HEADS * LANES), BF16),
                   jax.ShapeDtypeStruct((T, C_W), BF16)],
        grid=(T // tm,),
        in_specs=[pl.BlockSpec((tm, Q_LORA), row), t128, t128,
                  pl.BlockSpec((1, Q_LORA), fix), pl.BlockSpec((1, KV_LORA), fix),
                  pl.BlockSpec((Q_LORA, C_HEADS * LANES), fix),
                  pl.BlockSpec((KV_LORA, C_HEADS * LANES), fix),
                  pl.BlockSpec((KV_LORA, C_W), fix), t128, t128],
        out_specs=[pl.BlockSpec((C_HEADS, tm, LANES), lambda i: (0, i, 0)),
                   pl.BlockSpec((tm, C_HEADS * LANES), row),
                   pl.BlockSpec((tm, C_W), row)],
        compiler_params=_params(("parallel",)),
        name="mla_prep",
    )(cq, ckv, misc, g_q, g_kv, wq_p, wk_p, wv_p, cc, sc)


def _outproj_kernel(x_ref, mod_ref, o1_ref, o2_ref, o3_ref, l1_ref, l2_ref, l3_ref,
                    ocmp_ref, oslc_ref, owin_ref, misc_ref, ge_ref, oc_ref,
                    ga_ref, gb_ref, gc_ref, w_ref, out_ref):
    l1, l2, l3 = l1_ref[...], l2_ref[...], l3_ref[...]
    mx = jnp.maximum(jnp.maximum(l1, l2), l3)
    e1, e2, e3 = jnp.exp(l1 - mx), jnp.exp(l2 - mx), jnp.exp(l3 - mx)
    oa = (e1 * o1_ref[...] + e2 * o2_ref[...] + e3 * o3_ref[...]) / (e1 + e2 + e3)

    sg = jax.nn.sigmoid(misc_ref[...])
    hi = sg.astype(BF16)
    lo = (sg - hi.astype(F32)).astype(BF16)

    def gate(br):
        return (jnp.dot(hi, ge_ref[br], preferred_element_type=F32)
                + jnp.dot(lo, ge_ref[br], preferred_element_type=F32))

    ob = gate(0) * ocmp_ref[...] + gate(1) * oslc_ref[...] + gate(2) * owin_ref[...]
    na = _rms(oa, ga_ref[...]).astype(BF16)
    nb = _rms(ob, gb_ref[...]).astype(BF16)
    nc = _rms(oc_ref[...], gc_ref[...]).astype(BF16)
    mixed = (jnp.dot(na, w_ref[0:A_W, :], preferred_element_type=F32)
             + jnp.dot(nb, w_ref[A_W:A_W + B_W, :], preferred_element_type=F32)
             + jnp.dot(nc, w_ref[A_W + B_W:, :], preferred_element_type=F32))
    out_ref[...] = x_ref[...] + mod_ref[0, 2:3, :] * mixed


def _gate_expand():
    ge = np.zeros((3, LANES, B_W), np.float32)
    for h in range(B_HEADS):
        for br in range(3):
            ge[br, h * 3 + br, h * HEAD:(h + 1) * HEAD] = 1.0
    return jnp.asarray(ge, BF16)


def _out_proj(x2, mods_l, oa_parts, lse_parts, ocmp, oslc, owin, misc, oc, g_a, g_b, g_c, w_out_b, S):
    T, D = x2.shape
    tm = min(256, S)
    spb = S // tm
    row = lambda i: (i, 0)
    fix = lambda i: (0, 0)
    t128 = pl.BlockSpec((tm, LANES), row)
    t256 = pl.BlockSpec((tm, 256), row)
    t512 = pl.BlockSpec((tm, 512), row)
    return pl.pallas_call(
        _outproj_kernel,
        out_shape=jax.ShapeDtypeStruct((T, D), F32),
        grid=(T // tm,),
        in_specs=[pl.BlockSpec((tm, D), row),
                  pl.BlockSpec((1, 6, D), lambda i: (i // spb, 0, 0)),
                  t256, t256, t256, t256, t256, t256, t512, t512, t512, t128,
                  pl.BlockSpec((3, LANES, B_W), lambda i: (0, 0, 0)),
                  t256,
                  pl.BlockSpec((1, A_W), fix), pl.BlockSpec((1, B_W), fix), pl.BlockSpec((1, C_W), fix),
                  pl.BlockSpec((D, D), fix)],
        out_specs=pl.BlockSpec((tm, D), row),
        compiler_params=_params(("parallel",)),
        name="out_proj",
    )(x2, mods_l, *oa_parts, *lse_parts, ocmp, oslc, owin, misc, _gate_expand(), oc, g_a, g_b, g_c, w_out_b)


def _pack_halves(v):
    bits = lax.bitcast_convert_type(v.astype(BF16).astype(F32), U32)
    half = v.shape[1] // 2
    return (bits[:, half:] & jnp.uint32(0xFFFF0000)) | (bits[:, :half] >> 16)


def _unpack_halves(w):
    lo = lax.bitcast_convert_type(w << 16, F32)
    hi = lax.bitcast_convert_type(w & jnp.uint32(0xFFFF0000), F32)
    return jnp.concatenate([lo, hi], axis=1)


def _put_planes(ref, w):
    for c in range(ref.shape[0]):
        ref[c] = w[:, c * LANES:(c + 1) * LANES]


def _get_planes(ref):
    return jnp.concatenate([ref[c] for c in range(ref.shape[0])], axis=1)


def _route_kernel(x_ref, mod_ref, g_ref, wh_ref, wl_ref, b_ref, hp_ref, e_ref, r_ref, gt_ref, cnt_ref, carry):
    tm = x_ref.shape[0]

    @pl.when(pl.program_id(0) == 0)
    def _():
        carry[...] = jnp.zeros(carry.shape, F32)

    h = _rms(x_ref[...], g_ref[...]) * (1.0 + mod_ref[0, 4:5, :]) + mod_ref[0, 3:4, :]
    hb = h.astype(BF16)
    hlo = (h - hb.astype(F32)).astype(BF16)
    _put_planes(hp_ref, _pack_halves(h))

    logits = (jnp.dot(hb, wh_ref[...], preferred_element_type=F32)
              + jnp.dot(hlo, wh_ref[...], preferred_element_type=F32)
              + jnp.dot(hb, wl_ref[...], preferred_element_type=F32)) + b_ref[...]
    lane = lax.broadcasted_iota(I32, (tm, LANES), 1)
    lane_f = lane.astype(F32)
    vals, idxs, picks = [], [], []
    lg = logits
    for _ in range(TOP_K):
        mx = lg.max(axis=-1, keepdims=True)
        idx = jnp.where(lg == mx, lane_f, float(LANES)).min(axis=-1, keepdims=True)
        pick = lane_f == idx
        vals.append(mx)
        idxs.append(idx.astype(I32))
        picks.append(pick)
        lg = jnp.where(pick, LOWEST, lg)
    es = [jnp.exp(v - vals[0]) for v in vals]
    den = es[0] + es[1] + es[2] + es[3]
    onehot = jnp.zeros((tm, LANES), F32)
    for pk in picks:
        onehot = jnp.where(pk, 1.0, onehot)
    tri = (lax.broadcasted_iota(I32, (tm, tm), 0) >= lax.broadcasted_iota(I32, (tm, tm), 1)).astype(BF16)
    cs = jnp.dot(tri, onehot.astype(BF16), preferred_element_type=F32)
    rank = carry[...] + cs - 1.0
    e_out = jnp.zeros((tm, LANES), I32)
    r_out = jnp.zeros((tm, LANES), I32)
    g_out = jnp.zeros((tm, LANES), F32)
    for k in range(TOP_K):
        rk = jnp.where(picks[k], rank, 0.0).sum(axis=-1, keepdims=True)
        e_out = jnp.where(lane == k, idxs[k], e_out)
        r_out = jnp.where(lane == k, rk.astype(I32), r_out)
        g_out = jnp.where(lane == k, es[k] / den, g_out)
    e_ref[...] = e_out
    r_ref[...] = r_out
    gt_ref[...] = g_out
    new = carry[...] + onehot.sum(axis=0, keepdims=True)
    carry[...] = new
    cnt_ref[...] = new


def _route(x2, mods_l, g2, wr_hi, wr_lo, br_p, S):
    T, D = x2.shape
    tm = min(256, S)
    spb = S // tm
    row = lambda i: (i, 0)
    fix = lambda i: (0, 0)
    t128 = pl.BlockSpec((tm, LANES), row)
    return pl.pallas_call(
        _route_kernel,
        out_shape=[jax.ShapeDtypeStruct((D // 2 // LANES, T, LANES), U32),
                   jax.ShapeDtypeStruct((T, LANES), I32), jax.ShapeDtypeStruct((T, LANES), I32),
                   jax.ShapeDtypeStruct((T, LANES), F32), jax.ShapeDtypeStruct((1, LANES), F32)],
        grid=(T // tm,),
        in_specs=[pl.BlockSpec((tm, D), row),
                  pl.BlockSpec((1, 6, D), lambda i: (i // spb, 0, 0)),
                  pl.BlockSpec((1, D), fix),
                  pl.BlockSpec((D, LANES), fix), pl.BlockSpec((D, LANES), fix), pl.BlockSpec((1, LANES), fix)],
        out_specs=[pl.BlockSpec((D // 2 // LANES, tm, LANES), lambda i: (0, i, 0)), t128, t128, t128,
                   pl.BlockSpec((1, LANES), fix)],
        scratch_shapes=[pltpu.VMEM((1, LANES), F32)],
        compiler_params=_params(("arbitrary",)),
        name="moe_route",
    )(x2, mods_l, g2, wr_hi, wr_lo, br_p)


SC_GATHER_ROWS = 128


def _sc_gather_rows(x, idx, name):
    n = idx.shape[0]
    W = x.shape[1]
    mesh = plsc.VectorSubcoreMesh(core_axis_name="core", subcore_axis_name="subcore")

    def body(x_hbm, i_hbm, o_hbm):
        def step(i_vmem, o_vmem):
            pltpu.sync_copy(x_hbm.at[i_vmem.at[0]], o_vmem)

        pltpu.emit_pipeline(
            step,
            grid=(n // SC_GATHER_ROWS,),
            in_specs=[pl.BlockSpec((1, SC_GATHER_ROWS), lambda i: (0, i))],
            out_specs=[pl.BlockSpec((SC_GATHER_ROWS, W), lambda i: (i, 0))],
            core_axis_name=("core", "subcore"),
            dimension_semantics=(pltpu.PARALLEL,),
        )(i_hbm, o_hbm)

    return pl.kernel(body, out_type=jax.ShapeDtypeStruct((n, W), x.dtype), mesh=mesh,
                     scratch_types=[], name=name)(x, idx.reshape(1, n))


def _ffn_kernel(te_ref, nv_ref, x_ref, w1_ref, b1_ref, w2_ref, b2_ref, y_ref):
    @pl.when(pl.program_id(0) >= nv_ref[0])
    def _():
        y_ref[...] = jnp.zeros(y_ref.shape, U32)

    @pl.when(pl.program_id(0) < nv_ref[0])
    def _():
        xb = _unpack_halves(_get_planes(x_ref)).astype(BF16)
        D = xb.shape[1]
        glu = lax.dot_general(xb, w1_ref[0, :, :D], _NT, preferred_element_type=F32) + b1_ref[0, :, :D_FF]
        lin = lax.dot_general(xb, w1_ref[0, :, D:], _NT, preferred_element_type=F32) + b1_ref[0, :, D_FF:]
        glu = jnp.minimum(glu, SWIGLU_LIMIT)
        lin = jnp.clip(lin, -SWIGLU_LIMIT, SWIGLU_LIMIT)
        a = glu * jax.nn.sigmoid(SWIGLU_ALPHA * glu) * (lin + 1.0)
        y = jnp.dot(a.astype(BF16), w2_ref[0], preferred_element_type=F32) + b2_ref[0]
        _put_planes(y_ref, _pack_halves(y))


def _expert_ffn(tile_e, n_valid, xs, w1p, b1p, w2b, b2, tmf):
    P, n_rows, _ = xs.shape
    D = 2 * P * LANES
    ntiles = n_rows // tmf
    return pl.pallas_call(
        _ffn_kernel,
        out_shape=jax.ShapeDtypeStruct((P, n_rows, LANES), U32),
        grid_spec=pltpu.PrefetchScalarGridSpec(
            num_scalar_prefetch=2,
            grid=(ntiles,),
            in_specs=[pl.BlockSpec((P, tmf, LANES), lambda j, te, nv: (0, jnp.minimum(j, nv[0] - 1), 0)),
                      pl.BlockSpec((1, D_FF, 2 * D), lambda j, te, nv: (te[j], 0, 0)),
                      pl.BlockSpec((1, 1, 2 * D_FF), lambda j, te, nv: (te[j], 0, 0)),
                      pl.BlockSpec((1, D_FF, D), lambda j, te, nv: (te[j], 0, 0)),
                      pl.BlockSpec((1, 1, D), lambda j, te, nv: (te[j], 0, 0))],
            out_specs=pl.BlockSpec((P, tmf, LANES), lambda j, te, nv: (0, j, 0))),
        compiler_params=_params(("arbitrary",)),
        name="moe_ffn",
    )(tile_e, n_valid, xs, w1p, b1p, w2b, b2)


def _combine_kernel(yg_ref, gt_ref, x_ref, mod_ref, o_ref):
    g = gt_ref[...]
    y = None
    for k in range(TOP_K):
        yk = _unpack_halves(jnp.concatenate([yg_ref[c, k] for c in range(yg_ref.shape[0])], axis=1))
        y = g[:, k:k + 1] * yk if y is None else y + g[:, k:k + 1] * yk
    o_ref[...] = x_ref[...] + mod_ref[0, 5:6, :] * y


def _combine(yg, gates, x2, mods_l, S):
    T, D = x2.shape
    P = yg.shape[0]
    tc = min(512, S)
    spb = S // tc
    return pl.pallas_call(
        _combine_kernel,
        out_shape=jax.ShapeDtypeStruct((T, D), F32),
        grid=(T // tc,),
        in_specs=[pl.BlockSpec((P, TOP_K, tc, LANES), lambda i: (0, 0, i, 0)),
                  pl.BlockSpec((tc, LANES), lambda i: (i, 0)),
                  pl.BlockSpec((tc, D), lambda i: (i, 0)),
                  pl.BlockSpec((1, 6, D), lambda i: (i // spb, 0, 0))],
        out_specs=pl.BlockSpec((tc, D), lambda i: (i, 0)),
        compiler_params=_params(("parallel",)),
        name="moe_combine",
    )(yg, gates, x2, mods_l)


def _moe(x2, mods_l, g2, w_router, b_router, w1p, b1p, w2b, b2, S):
    T, D = x2.shape
    tmf = 256
    wr = jnp.zeros((D, LANES), F32).at[:, :N_EXPERTS].set(w_router)
    wr_hi = wr.astype(BF16)
    wr_lo = (wr - wr_hi.astype(F32)).astype(BF16)
    br_p = jnp.full((1, LANES), NEG, F32).at[0, :N_EXPERTS].set(b_router)
    hp, e_idx, ranks, gates, cnt = _route(x2, mods_l, g2, wr_hi, wr_lo, br_p, S)

    counts = cnt[0, :N_EXPERTS].astype(I32)
    padded = ((counts + tmf - 1) // tmf) * tmf
    ends = jnp.cumsum(padded)
    offs = ends - padded
    dest = offs[e_idx[:, :TOP_K]] + ranks[:, :TOP_K]
    n_rows = TOP_K * T + N_EXPERTS * tmf
    ntiles = n_rows // tmf
    n_valid = (ends[-1] // tmf).astype(I32).reshape(1)
    tile_start = jnp.minimum(jnp.arange(ntiles, dtype=I32), n_valid[0] - 1) * tmf
    tile_e = jnp.sum(tile_start[:, None] >= ends[None, :], axis=1).astype(I32)
    src = jnp.zeros((n_rows,), I32).at[dest.reshape(TOP_K * T)].set(jnp.arange(TOP_K * T, dtype=I32) // TOP_K)
    P = hp.shape[0]
    plane = jnp.arange(P, dtype=I32)[:, None]
    src_rows = (src[None, :] + plane * T).reshape(P * n_rows)
    dest_rows = (dest.T.reshape(1, TOP_K * T) + plane * n_rows).reshape(P * TOP_K * T)

    xs = _sc_gather_rows(hp.reshape(P * T, LANES), src_rows, "moe_dispatch_sc").reshape(P, n_rows, LANES)
    ys = _expert_ffn(tile_e, n_valid, xs, w1p, b1p, w2b, b2, tmf)
    yg = _sc_gather_rows(ys.reshape(P * n_rows, LANES), dest_rows, "moe_collect_sc").reshape(P, TOP_K, T, LANES)
    return _combine(yg, gates, x2, mods_l, S)


def _final_kernel(x_ref, g_ref, o_ref):
    o_ref[...] = _rms(x_ref[...], g_ref[...])


def _final_norm(x2, g):
    T, D = x2.shape
    tm = min(512, T)
    return pl.pallas_call(
        _final_kernel,
        out_shape=jax.ShapeDtypeStruct((T, D), F32),
        grid=(T // tm,),
        in_specs=[pl.BlockSpec((tm, D), lambda i: (i, 0)), pl.BlockSpec((1, D), lambda i: (0, 0))],
        out_specs=pl.BlockSpec((tm, D), lambda i: (i, 0)),
        compiler_params=_params(("parallel",)),
        name="final_norm",
    )(x2, g.reshape(1, D))


def _rearrange_w_in(w):
    D = w.shape[0]
    main = w[:, :2048]
    gates = w[:, 2048:2072]
    cq = w[:, 2072:2328]
    ckv = w[:, 2328:2456]
    kr = w[:, 2456:2488]
    z = lambda n: jnp.zeros((D, n), w.dtype)
    misc = jnp.concatenate([gates, z(KR_LANE - 24), kr, z(LANES - KR_LANE - QK_ROPE)], axis=1)
    return jnp.concatenate([main, cq, ckv, misc], axis=1).astype(BF16)


def _mla_weights(w_q_up, w_kv_up):
    dq = QK_NOPE + QK_ROPE
    wq = jnp.zeros((Q_LORA, C_HEADS * LANES), F32)
    wk = jnp.zeros((KV_LORA, C_HEADS * LANES), F32)
    wv = []
    for h in range(C_HEADS):
        wq = wq.at[:, h * LANES:h * LANES + dq].set(w_q_up[:, h * dq:(h + 1) * dq])
        kv = w_kv_up[:, h * (QK_NOPE + HEAD):(h + 1) * (QK_NOPE + HEAD)]
        wk = wk.at[:, h * LANES:h * LANES + QK_NOPE].set(kv[:, :QK_NOPE])
        wv.append(kv[:, QK_NOPE:])
    return wq.astype(BF16), wk.astype(BF16), jnp.concatenate(wv, axis=1).astype(BF16)


def kernel(x, c, positions, w_ada, b_ada, g_norm1, g_norm2, w_in, nsa_pos_k, nsa_pos_v, nsa_w1_k, nsa_b1_k, nsa_w2_k, nsa_b2_k, nsa_w1_v, nsa_b1_v, nsa_w2_v, nsa_b2_v, mla_g_q, mla_g_kv, mla_w_q_up, mla_w_kv_up, g_out_a, g_out_b, g_out_c, w_out, w_router, b_router, w_exp1, b_exp1, w_exp2, b_exp2, g_final):
    B, S, D = x.shape
    T = B * S
    depth = w_ada.shape[0]
    x2 = x.reshape(T, D)

    cab, sab, cc, sc = _rope_tables(positions.reshape(T, 1))
    nch = S // CMP_STRIDE
    n_cmp = (S - CMP_LEN) // CMP_STRIDE + 1
    end_idx = np.minimum(np.arange(nch) * CMP_STRIDE + CMP_LEN - 1, S - 1)
    cos_e = cab.reshape(B, S, LANES)[:, end_idx]
    sin_e = sab.reshape(B, S, LANES)[:, end_idx]
    mods = _ada_mods(c, w_ada, b_ada).reshape(depth, B, 6, D)
    del n_cmp

    for l in range(depth):
        mods_l = mods[l]
        (qa, ka, va, qb, kcmp, vcmp, kslc, vslc, kwin, vwin, cq, ckv, misc) = _in_proj(
            x2, mods_l, g_norm1[l].reshape(1, D), _rearrange_w_in(w_in[l]), cab, sab, S)

        oa_parts, lse_parts = [], []
        for window, d in A_PAIRS:
            o, lse = _dilated_attention(qa, ka, va, B, S, d, window // d)
            oa_parts.append(o)
            lse_parts.append(lse)

        kc = _compress(kcmp, nsa_pos_k[l], nsa_w1_k[l], nsa_b1_k[l], nsa_w2_k[l], nsa_b2_k[l],
                       cos_e, sin_e, B, S, True)
        vc = _compress(vcmp, nsa_pos_v[l], nsa_w1_v[l], nsa_b1_v[l], nsa_w2_v[l], nsa_b2_v[l],
                       cos_e, sin_e, B, S, False)
        kcc = jnp.concatenate([kc[:, 0, :, :HEAD], kc[:, 1, :, :HEAD]], axis=-1).astype(BF16)
        vcc = jnp.concatenate([vc[:, 0, :, :HEAD], vc[:, 1, :, :HEAD]], axis=-1).astype(BF16)
        ocmp, sel_t = _cmp_attention(qb, kcc, vcc, B, S)
        oslc = _select_attention(qb, kslc, vslc, sel_t, B, S)
        owin = _window_attention(qb, kwin, vwin, B, S)

        wq_p, wk_p, wv_p = _mla_weights(mla_w_q_up[l], mla_w_kv_up[l])
        qc, kcl, vcl = _mla_prep(cq, ckv, misc, mla_g_q[l].reshape(1, Q_LORA), mla_g_kv[l].reshape(1, KV_LORA),
                                 wq_p, wk_p, wv_p, cc, sc)
        oc = _latent_attention(qc, kcl, vcl, B, S)

        x2 = _out_proj(x2, mods_l, oa_parts, lse_parts, ocmp, oslc, owin, misc, oc,
                       g_out_a[l].reshape(1, A_W), g_out_b[l].reshape(1, B_W), g_out_c[l].reshape(1, C_W),
                       w_out[l].astype(BF16), S)

        w1p = jnp.transpose(w_exp1[l], (0, 2, 1)).astype(BF16).reshape(N_EXPERTS, D_FF, 2 * D)
        b1r = b_exp1[l].reshape(N_EXPERTS, D_FF, 2)
        b1p = jnp.concatenate([b1r[..., 0], b1r[..., 1]], axis=-1).reshape(N_EXPERTS, 1, 2 * D_FF)
        x2 = _moe(x2, mods_l, g_norm2[l].reshape(1, D), w_router[l], b_router[l],
                  w1p, b1p, w_exp2[l].astype(BF16), b_exp2[l].reshape(N_EXPERTS, 1, D), S)

    return _final_norm(x2, g_final).reshape(B, S, D)
```

```python
import functools

import numpy as np
import jax
import jax.numpy as jnp
from jax import lax
from jax.experimental import pallas as pl
from jax.experimental.pallas import tpu as pltpu
from jax.experimental.pallas import tpu_sc as plsc

F32 = jnp.float32
BF16 = jnp.bfloat16
I32 = jnp.int32
U32 = jnp.uint32

LANES = 128
HEAD = 64
NEG = -1e30
LOWEST = -3.0e38
NORM_EPS = 1e-6
ROPE_THETA = 10000.0

A_HEADS, B_HEADS, B_GROUPS, C_HEADS = 4, 8, 2, 4
A_PAIRS = ((128, 1), (512, 4), (2048, 16))
A_W, B_W, C_W = 256, 512, 256
CMP_LEN, CMP_STRIDE, CMP_HIDDEN = 32, 16, 256
SLC_BLOCK, N_SELECT, WINDOW = 64, 16, 512
FORCE_BONUS = 1e4
Q_LORA, KV_LORA, QK_NOPE, QK_ROPE = 256, 128, 64, 32
N_EXPERTS, TOP_K, D_FF = 32, 4, 1024
SWIGLU_ALPHA, SWIGLU_LIMIT = 1.702, 7.0
IN_COLS = 2560
KR_LANE = 64

VMEM_LIMIT = 56 * 1024 * 1024


def _params(sem, vmem=VMEM_LIMIT):
    return pltpu.CompilerParams(dimension_semantics=sem, vmem_limit_bytes=vmem)


def _rms(x, g):
    return x * lax.rsqrt(jnp.mean(x * x, axis=-1, keepdims=True) + NORM_EPS) * g


def _rope_rows():
    inv64 = ROPE_THETA ** (-np.arange(32, dtype=np.float32) * (2.0 / 64))
    inv32 = ROPE_THETA ** (-np.arange(16, dtype=np.float32) * (2.0 / 32))
    inv_ab = np.tile(np.concatenate([inv64, inv64]), 2)
    sgn_ab = np.tile(np.concatenate([-np.ones(32), np.ones(32)]), 2)
    inv_c = np.zeros(LANES, np.float32)
    sgn_c = np.zeros(LANES, np.float32)
    inv_c[KR_LANE:KR_LANE + 16] = inv32
    inv_c[KR_LANE + 16:KR_LANE + 32] = inv32
    sgn_c[KR_LANE:KR_LANE + 16] = -1.0
    sgn_c[KR_LANE + 16:KR_LANE + 32] = 1.0
    rows = np.stack([inv_ab, sgn_ab, inv_c, sgn_c]).astype(np.float32)
    return jnp.asarray(rows)


def _rope_tab_kernel(pos_ref, rows_ref, cab_ref, sab_ref, cc_ref, sc_ref):
    pos = pos_ref[...].astype(F32)
    ang = pos * rows_ref[0:1, :]
    cab_ref[...] = jnp.cos(ang)
    sab_ref[...] = jnp.sin(ang) * rows_ref[1:2, :]
    ang = pos * rows_ref[2:3, :]
    cc_ref[...] = jnp.cos(ang)
    sc_ref[...] = jnp.sin(ang) * rows_ref[3:4, :]


def _rope_tables(pos_col):
    R = pos_col.shape[0]
    tm = min(R, 1024)
    spec = pl.BlockSpec((tm, LANES), lambda i: (i, 0))
    return pl.pallas_call(
        _rope_tab_kernel,
        out_shape=[jax.ShapeDtypeStruct((R, LANES), F32)] * 4,
        grid=(R // tm,),
        in_specs=[pl.BlockSpec((tm, 1), lambda i: (i, 0)), pl.BlockSpec((4, LANES), lambda i: (0, 0))],
        out_specs=[spec] * 4,
        compiler_params=_params(("parallel",)),
        name="rope_tables",
    )(pos_col, _rope_rows())


def _swap32(t, lane):
    return jnp.where((lane & 32) == 0, pltpu.roll(t, 96, 1), pltpu.roll(t, 32, 1))


def _swap16(t, lane):
    return jnp.where(lane < KR_LANE + 16, pltpu.roll(t, 112, 1), pltpu.roll(t, 16, 1))


def _mods_kernel(c_ref, w_ref, b_ref, o_ref):
    c = c_ref[...]
    ca = (c * jax.nn.sigmoid(c)).astype(BF16)
    o_ref[0] = jnp.dot(ca, w_ref[0].astype(BF16), preferred_element_type=F32) + b_ref[0]


def _ada_mods(c, w_ada, b_ada):
    L, D, N = w_ada.shape
    B = c.shape[0]
    tn = 1536
    return pl.pallas_call(
        _mods_kernel,
        out_shape=jax.ShapeDtypeStruct((L, B, N), F32),
        grid=(L, N // tn),
        in_specs=[pl.BlockSpec((B, D), lambda l, j: (0, 0)),
                  pl.BlockSpec((1, D, tn), lambda l, j: (l, 0, j)),
                  pl.BlockSpec((1, 1, tn), lambda l, j: (l, 0, j))],
        out_specs=pl.BlockSpec((1, B, tn), lambda l, j: (l, 0, j)),
        compiler_params=_params(("parallel", "parallel")),
        name="ada_mods",
    )(c, w_ada, b_ada.reshape(L, 1, N))


A_DILS = tuple(d for _, d in A_PAIRS)


def _inproj_kernel(x_ref, mod_ref, g_ref, w_ref, cab_ref, sab_ref, *refs):
    nd = len(A_DILS)
    qa_refs, ka_refs, va_refs = refs[0:nd], refs[nd:2 * nd], refs[2 * nd:3 * nd]
    (qb_ref, kcmp_ref, vcmp_ref, kslc_ref, vslc_ref, kwin_ref, vwin_ref, cq_ref, ckv_ref, misc_ref,
     scr) = refs[3 * nd:]
    tm = x_ref.shape[0]

    def spread(val, store):
        W = val.shape[1]
        for p in range(W // LANES):
            scr[p] = val[:, p * LANES:(p + 1) * LANES]
        for i, d in enumerate(A_DILS):
            if d == 1:
                store(i, 0, val.astype(BF16))
            else:
                for r in range(d):
                    planes = [scr[p, pl.ds(r, tm // d, stride=d), :] for p in range(W // LANES)]
                    piece = planes[0] if len(planes) == 1 else jnp.concatenate(planes, axis=1)
                    store(i, r, piece.astype(BF16))
    x = x_ref[...]
    h = (_rms(x, g_ref[...]) * (1.0 + mod_ref[0, 1:2, :]) + mod_ref[0, 0:1, :]).astype(BF16)
    cos = cab_ref[...]
    sin = sab_ref[...]
    lane = lax.broadcasted_iota(I32, (tm, LANES), 1)
    lo64 = lane < HEAD

    def proj(c0, n):
        return jnp.dot(h, w_ref[:, c0:c0 + n], preferred_element_type=F32)

    def rope(t):
        return t * cos + _swap32(t, lane) * sin

    def pad_q(h0, t, dst):
        t = rope(t) * (HEAD ** -0.5)
        tr = pltpu.roll(t, HEAD, 1)
        return [jnp.where(lo64 if dst[i] == 0 else ~lo64, t if dst[i] == i else tr, 0.0) for i in range(2)]

    def put_a(arefs, width, lane0=0):
        def store(i, r, piece):
            arefs[i][:, r * A_W + lane0:r * A_W + lane0 + width] = piece
        return store

    y = proj(0, 256)
    for j in range(2):
        for i, qh in enumerate(pad_q(2 * j, y[:, 128 * j:128 * (j + 1)], (0, 1))):
            def store(di, r, piece, hd=2 * j + i):
                qa_refs[di][hd, :, r * LANES:(r + 1) * LANES] = piece

            spread(qh, store)
    y = proj(256, 256)
    spread(jnp.concatenate([rope(y[:, :128]), rope(y[:, 128:])], axis=1), put_a(ka_refs, A_W))
    spread(proj(512, 256), put_a(va_refs, A_W))
    for j in range(2):
        y = proj(768 + 256 * j, 256)
        for jj in range(2):
            for i, qh in enumerate(pad_q(4 * j + 2 * jj, y[:, 128 * jj:128 * (jj + 1)], (j, j))):
                qb_ref[4 * j + 2 * jj + i] = qh.astype(BF16)
    y = proj(1280, 256)
    kcmp_ref[...] = y[:, :128]
    vcmp_ref[...] = y[:, 128:]
    y = proj(1536, 256)
    kslc_ref[...] = rope(y[:, :128]).astype(BF16)
    vslc_ref[...] = y[:, 128:].astype(BF16)
    y = proj(1792, 256)
    kwin_ref[...] = rope(y[:, :128]).astype(BF16)
    vwin_ref[...] = y[:, 128:].astype(BF16)
    cq_ref[...] = proj(2048, 256)
    y = proj(2304, 256)
    ckv_ref[...] = y[:, :128]
    misc_ref[...] = y[:, 128:]


def _in_proj(x2, mods_l, g1, w_in_r, cab, sab, S):
    T, D = x2.shape
    tm = min(512, S)
    spb = S // tm
    row = lambda i: (i, 0)
    shp = jax.ShapeDtypeStruct
    outs = ([shp((A_HEADS, T // d, d * LANES), BF16) for d in A_DILS]
            + [shp((T // d, d * A_W), BF16) for d in A_DILS] * 2
            + [shp((B_HEADS, T, LANES), BF16),
               shp((T, LANES), F32), shp((T, LANES), F32), shp((T, LANES), BF16), shp((T, LANES), BF16),
               shp((T, LANES), BF16), shp((T, LANES), BF16),
               shp((T, Q_LORA), F32), shp((T, LANES), F32), shp((T, LANES), F32)])
    t128 = pl.BlockSpec((tm, LANES), row)
    t256 = pl.BlockSpec((tm, 256), row)
    out_specs = ([pl.BlockSpec((A_HEADS, tm // d, d * LANES), lambda i: (0, i, 0)) for d in A_DILS]
                 + [pl.BlockSpec((tm // d, d * A_W), row) for d in A_DILS] * 2
                 + [pl.BlockSpec((B_HEADS, tm, LANES), lambda i: (0, i, 0)),
                    t128, t128, t128, t128, t128, t128, t256, t128, t128])
    res = pl.pallas_call(
        _inproj_kernel,
        out_shape=outs,
        grid=(T // tm,),
        in_specs=[pl.BlockSpec((tm, D), row),
                  pl.BlockSpec((1, 6, D), lambda i: (i // spb, 0, 0)),
                  pl.BlockSpec((1, D), lambda i: (0, 0)),
                  pl.BlockSpec((D, IN_COLS), lambda i: (0, 0)),
                  t128, t128],
        out_specs=out_specs,
        scratch_shapes=[pltpu.VMEM((A_W // LANES, tm, LANES), F32)],
        compiler_params=_params(("parallel",)),
        name="in_proj",
    )(x2, mods_l, g1, w_in_r, cab, sab)
    nd = len(A_DILS)
    return (res[0:nd], res[nd:2 * nd], res[2 * nd:3 * nd]) + tuple(res[3 * nd:])


_NT = (((1,), (1,)), ((), ()))
_TN = (((0,), (0,)), ((), ()))


KEY_SUB = 128
QUERY_SUB = 256


def _online_chunk(k, v, q, keep, bias, m, l, acc):
    s = lax.dot_general(k, q, _NT, preferred_element_type=F32)
    if bias is not None:
        s = s + bias
    if keep is not None:
        s = jnp.where(keep, s, NEG)
    m_next = jnp.maximum(m, s.max(axis=0, keepdims=True))
    alpha = jnp.exp(m - m_next)
    p = jnp.exp(s - m_next)
    l = alpha * l + p.sum(axis=0, keepdims=True)
    acc = acc * alpha + lax.dot_general(v, p.astype(BF16), _TN, preferred_element_type=F32)
    return m_next, l, acc


def _mask_heads(s, masks, tq, fill):
    return jnp.concatenate([jnp.where(masks[h], s[:, h * tq:(h + 1) * tq], fill)
                            for h in range(len(masks))], axis=1)


def _emit_heads(o_ref, acc_t, halves, tq):
    nh = len(halves)
    lo64 = lax.broadcasted_iota(I32, (tq, LANES), 1) < HEAD
    for c in range(nh // 2):
        parts = []
        for h in (2 * c, 2 * c + 1):
            o = acc_t[:, h * tq:(h + 1) * tq].T
            if halves[h] != h % 2:
                o = pltpu.roll(o, HEAD, 1)
            parts.append(o)
        o_ref[:, c * LANES:(c + 1) * LANES] = jnp.where(lo64, parts[0], parts[1])


def _band_kernel(*refs, nblk, tq, tk, max_dist, stacks, halves, with_lse, qi_axis):
    q_ref = refs[0]
    k_refs = refs[1:1 + nblk]
    v_refs = refs[1 + nblk:1 + 2 * nblk]
    o_ref = refs[1 + 2 * nblk]
    lse_ref = refs[2 + 2 * nblk] if with_lse else None
    nh = len(halves)
    qi = pl.program_id(qi_axis)
    ks = min(KEY_SUB, tk)
    cq = min(QUERY_SUB, tq)
    rc = lax.broadcasted_iota(I32, (ks, cq), 0) - lax.broadcasted_iota(I32, (ks, cq), 1)
    head_blk = {}
    for (h0, n, blk) in stacks:
        for h in range(h0, h0 + n):
            head_blk[h] = blk
    outs, lses = [], []
    for h in range(nh):
        lanes = slice(head_blk[h] * LANES, (head_blk[h] + 1) * LANES)
        for c in range(tq // cq):
            q = q_ref[h, c * cq:(c + 1) * cq, :]
            m = jnp.full((1, cq), NEG, F32)
            l = jnp.zeros((1, cq), F32)
            acc = jnp.zeros((LANES, cq), F32)
            for i in range(nblk):
                kt = qi * (tq // tk) - (nblk - tq // tk) + i
                for kk in range(tk // ks):
                    base = qi * tq + c * cq - (kt * tk + kk * ks)
                    base = jnp.where(kt >= 0, base, -(1 << 20))
                    keep = (rc <= base) & (rc >= base - max_dist)
                    m, l, acc = _online_chunk(k_refs[i][kk * ks:(kk + 1) * ks, lanes],
                                              v_refs[i][kk * ks:(kk + 1) * ks, lanes], q, keep, None, m, l, acc)
            outs.append(acc / l)
            lses.append(jnp.broadcast_to(m + jnp.log(l), (LANES, cq)))
    _emit_heads(o_ref, jnp.concatenate(outs, axis=1), halves, tq)
    if with_lse:
        _emit_heads(lse_ref, jnp.concatenate(lses, axis=1), tuple(h % 2 for h in range(nh)), tq)


def _dilated_attention(qa, ka, va, B, S, d, max_dist):
    L = S // d
    tq = min(256, L)
    tk = 128
    nprev = -(-max_dist // tk)
    nblk = nprev + tq // tk
    q_v = qa.reshape(A_HEADS, B, L, d * LANES)
    k_v = ka.reshape(B, L, d * A_W)
    v_v = va.reshape(B, L, d * A_W)
    kv_specs = []
    for i in range(nblk):
        kv_specs.append(pl.BlockSpec(
            (None, tk, A_W),
            functools.partial(lambda b, r, qi, i: (b, jnp.maximum(qi * (tq // tk) - nprev + i, 0), r), i=i)))
    out_spec = pl.BlockSpec((None, tq, A_W), lambda b, r, qi: (b, qi, r))
    kern = functools.partial(_band_kernel, nblk=nblk, tq=tq, tk=tk, max_dist=max_dist,
                             stacks=((0, 2, 0), (2, 2, 1)), halves=(0, 1, 0, 1), with_lse=True, qi_axis=2)
    o, lse = pl.pallas_call(
        kern,
        out_shape=[jax.ShapeDtypeStruct((B, L, d * A_W), F32)] * 2,
        grid=(B, d, L // tq),
        in_specs=[pl.BlockSpec((A_HEADS, None, tq, LANES), lambda b, r, qi: (0, b, qi, r))]
                 + kv_specs + kv_specs,
        out_specs=[out_spec, out_spec],
        compiler_params=_params(("parallel", "parallel", "parallel")),
        name=f"dilated_attn_d{d}",
    )(q_v, *([k_v] * nblk), *([v_v] * nblk))
    return o.reshape(B * L, d * A_W), lse.reshape(B * L, d * A_W)


def _window_attention(qb, kwin, vwin, B, S):
    tq = min(256, S)
    tk = tq
    max_dist = WINDOW - 1
    nprev = -(-max_dist // tk)
    nblk = nprev + 1
    q_v = qb.reshape(B_HEADS, B, S, LANES)
    k_v = kwin.reshape(B, S, LANES)
    v_v = vwin.reshape(B, S, LANES)
    kv_specs = [pl.BlockSpec((None, tk, LANES),
                             functools.partial(lambda b, qi, i: (b, jnp.maximum(qi - nprev + i, 0), 0), i=i))
                for i in range(nblk)]
    kern = functools.partial(_band_kernel, nblk=nblk, tq=tq, tk=tk, max_dist=max_dist,
                             stacks=((0, B_HEADS, 0),), halves=(0, 0, 0, 0, 1, 1, 1, 1),
                             with_lse=False, qi_axis=1)
    o = pl.pallas_call(
        kern,
        out_shape=jax.ShapeDtypeStruct((B, S, B_W), F32),
        grid=(B, S // tq),
        in_specs=[pl.BlockSpec((B_HEADS, None, tq, LANES), lambda b, qi: (0, b, qi, 0))]
                 + kv_specs + kv_specs,
        out_specs=pl.BlockSpec((None, tq, B_W), lambda b, qi: (b, qi, 0)),
        compiler_params=_params(("parallel", "parallel")),
        name="window_attn",
    )(q_v, *([k_v] * nblk), *([v_v] * nblk))
    return o.reshape(B * S, B_W)


def _flash_kernel(qi_ref, kj_ref, first_ref, last_ref, *refs, tq, tk, qk_groups, pv_groups, halves, select):
    if select:
        q_ref, k_ref, v_ref, sel_ref, o_ref, m_sc, l_sc, acc_sc = refs
    else:
        q_ref, k_ref, v_ref, o_ref, m_sc, l_sc, acc_sc = refs
    nh = len(halves)
    step = pl.program_id(1)
    qi = qi_ref[step]
    kj = kj_ref[step]

    @pl.when(first_ref[step] == 1)
    def _():
        m_sc[...] = jnp.full(m_sc.shape, NEG, F32)
        l_sc[...] = jnp.zeros(l_sc.shape, F32)
        acc_sc[...] = jnp.zeros(acc_sc.shape, F32)

    ks = min(KEY_SUB, tk)
    cq = min(QUERY_SUB, tq)
    rc = lax.broadcasted_iota(I32, (ks, cq), 0) - lax.broadcasted_iota(I32, (ks, cq), 1)
    head_kb, head_vb = {}, {}
    for (h0, n, kb) in qk_groups:
        for h in range(h0, h0 + n):
            head_kb[h] = kb
    for (h0, n, vb) in pv_groups:
        for h in range(h0, h0 + n):
            head_vb[h] = vb
    hpg = nh // B_GROUPS
    spb = ks // SLC_BLOCK if select else 1

    def run(on_diagonal):
        for h in range(nh):
            klanes = slice(head_kb[h] * LANES, (head_kb[h] + 1) * LANES)
            vlanes = slice(head_vb[h] * LANES, (head_vb[h] + 1) * LANES)
            for c in range(tq // cq):
                cols = slice(h * tq + c * cq, h * tq + (c + 1) * cq)
                q = q_ref[h, c * cq:(c + 1) * cq, :]
                m, l, acc = m_sc[:, cols], l_sc[:, cols], acc_sc[:, cols]
                for kk in range(tk // ks):
                    keep = None
                    if on_diagonal:
                        keep = rc <= (qi * tq + c * cq) - (kj * tk + kk * ks)
                    bias = None
                    if select:
                        rows = [jnp.broadcast_to(
                            (sel_ref[h // hpg, pl.ds((kj * (tk // ks) + kk) * spb + j, 1), c * cq:(c + 1) * cq] - 1.0)
                            * (-NEG), (SLC_BLOCK, cq)) for j in range(spb)]
                        bias = rows[0] if spb == 1 else jnp.concatenate(rows, axis=0)
                    m, l, acc = _online_chunk(k_ref[kk * ks:(kk + 1) * ks, klanes],
                                              v_ref[kk * ks:(kk + 1) * ks, vlanes], q, keep, bias, m, l, acc)
                m_sc[:, cols] = m
                l_sc[:, cols] = l
                acc_sc[:, cols] = acc

    crosses = kj * tk + (tk - 1) > qi * tq

    @pl.when(crosses)
    def _():
        run(True)

    @pl.when(jnp.logical_not(crosses))
    def _():
        run(False)

    @pl.when(last_ref[step] == 1)
    def _():
        _emit_heads(o_ref, acc_sc[...] / l_sc[...], halves, tq)


def _causal_schedule(S, tq, tk):
    qi, kj, first, last = [], [], [], []
    for i in range(S // tq):
        nk = (i * tq + tq - 1) // tk + 1
        for j in range(nk):
            qi.append(i)
            kj.append(j)
            first.append(1 if j == 0 else 0)
            last.append(1 if j == nk - 1 else 0)
    return [jnp.asarray(np.asarray(a, np.int32)) for a in (qi, kj, first, last)]


def _select_attention(qb, kslc, vslc, sel_t, B, S):
    tq = min(512, S)
    tk = min(256, S)
    n_slc = S // SLC_BLOCK
    tabs = _causal_schedule(S, tq, tk)
    nsteps = tabs[0].shape[0]
    kern = functools.partial(_flash_kernel, tq=tq, tk=tk, qk_groups=((0, B_HEADS, 0),),
                             pv_groups=((0, B_HEADS, 0),), halves=(0, 0, 0, 0, 1, 1, 1, 1), select=True)
    o = pl.pallas_call(
        kern,
        out_shape=jax.ShapeDtypeStruct((B, S, B_W), F32),
        grid_spec=pltpu.PrefetchScalarGridSpec(
            num_scalar_prefetch=4,
            grid=(B, nsteps),
            in_specs=[pl.BlockSpec((B_HEADS, None, tq, LANES), lambda b, s, qi, kj, f, l: (0, b, qi[s], 0)),
                      pl.BlockSpec((None, tk, LANES), lambda b, s, qi, kj, f, l: (b, kj[s], 0)),
                      pl.BlockSpec((None, tk, LANES), lambda b, s, qi, kj, f, l: (b, kj[s], 0)),
                      pl.BlockSpec((None, B_GROUPS, n_slc, tq), lambda b, s, qi, kj, f, l: (b, 0, 0, qi[s]))],
            out_specs=pl.BlockSpec((None, tq, B_W), lambda b, s, qi, kj, f, l: (b, qi[s], 0)),
            scratch_shapes=[pltpu.VMEM((1, B_HEADS * tq), F32), pltpu.VMEM((1, B_HEADS * tq), F32),
                            pltpu.VMEM((LANES, B_HEADS * tq), F32)]),
        compiler_params=_params(("parallel", "arbitrary")),
        name="select_attn",
    )(*tabs, qb.reshape(B_HEADS, B, S, LANES), kslc.reshape(B, S, LANES), vslc.reshape(B, S, LANES), sel_t)
    return o.reshape(B * S, B_W)


def _latent_attention(qc, kc, vc, B, S):
    tq = min(512, S)
    tk = min(512, S)
    tabs = _causal_schedule(S, tq, tk)
    nsteps = tabs[0].shape[0]
    kern = functools.partial(_flash_kernel, tq=tq, tk=tk,
                             qk_groups=tuple((h, 1, h) for h in range(C_HEADS)),
                             pv_groups=((0, 2, 0), (2, 2, 1)), halves=(0, 1, 0, 1), select=False)
    o = pl.pallas_call(
        kern,
        out_shape=jax.ShapeDtypeStruct((B, S, C_W), F32),
        grid_spec=pltpu.PrefetchScalarGridSpec(
            num_scalar_prefetch=4,
            grid=(B, nsteps),
            in_specs=[pl.BlockSpec((C_HEADS, None, tq, LANES), lambda b, s, qi, kj, f, l: (0, b, qi[s], 0)),
                      pl.BlockSpec((None, tk, C_HEADS * LANES), lambda b, s, qi, kj, f, l: (b, kj[s], 0)),
                      pl.BlockSpec((None, tk, C_W), lambda b, s, qi, kj, f, l: (b, kj[s], 0))],
            out_specs=pl.BlockSpec((None, tq, C_W), lambda b, s, qi, kj, f, l: (b, qi[s], 0)),
            scratch_shapes=[pltpu.VMEM((1, C_HEADS * tq), F32), pltpu.VMEM((1, C_HEADS * tq), F32),
                            pltpu.VMEM((LANES, C_HEADS * tq), F32)]),
        compiler_params=_params(("parallel", "arbitrary")),
        name="latent_attn",
    )(*tabs, qc.reshape(C_HEADS, B, S, LANES), kc.reshape(B, S, C_HEADS * LANES), vc.reshape(B, S, C_W))
    return o.reshape(B * S, C_W)


def _gelu_tanh(x):
    return 0.5 * x * (1.0 + jnp.tanh(0.7978845608028654 * (x + 0.044715 * x * x * x)))


def _compress_kernel(u_ref, pos_ref, w1_ref, b1_ref, w2_ref, b2_ref, cos_ref, sin_ref, o_ref, *, rotate):
    n = u_ref.shape[0]
    half = CMP_STRIDE * HEAD
    u = u_ref[...]
    ua = (u + pos_ref[:, :half]).astype(BF16)
    ub = (u + pos_ref[:, half:]).astype(BF16)
    pa = jnp.dot(ua, w1_ref[:half, :].astype(BF16), preferred_element_type=F32)
    pb = jnp.dot(ub, w1_ref[half:, :].astype(BF16), preferred_element_type=F32)
    hid = _gelu_tanh(pa + pltpu.roll(pb, n - 1, 0) + b1_ref[...])
    y = jnp.dot(hid.astype(BF16), w2_ref[...].astype(BF16), preferred_element_type=F32) + b2_ref[...]
    if rotate:
        lane = lax.broadcasted_iota(I32, y.shape, 1)
        y = y * cos_ref[...] + _swap32(y, lane) * sin_ref[...]
    o_ref[...] = y


def _compress(t, pos_emb, w1, b1, w2, b2, cos_e, sin_e, B, S, rotate):
    nch = S // CMP_STRIDE
    u = t.reshape(B, S, B_GROUPS, HEAD).transpose(0, 2, 1, 3).reshape(B, B_GROUPS, nch, CMP_STRIDE * HEAD)
    w2d = jnp.concatenate([w2, w2], axis=1)
    b2d = jnp.concatenate([b2, b2]).reshape(1, LANES)
    kern = functools.partial(_compress_kernel, rotate=rotate)
    tab = pl.BlockSpec((None, nch, LANES), lambda b, g: (b, 0, 0))
    return pl.pallas_call(
        kern,
        out_shape=jax.ShapeDtypeStruct((B, B_GROUPS, nch, LANES), F32),
        grid=(B, B_GROUPS),
        in_specs=[pl.BlockSpec((None, None, nch, CMP_STRIDE * HEAD), lambda b, g: (b, g, 0, 0)),
                  pl.BlockSpec((1, CMP_LEN * HEAD), lambda b, g: (0, 0)),
                  pl.BlockSpec((CMP_LEN * HEAD, CMP_HIDDEN), lambda b, g: (0, 0)),
                  pl.BlockSpec((1, CMP_HIDDEN), lambda b, g: (0, 0)),
                  pl.BlockSpec((CMP_HIDDEN, LANES), lambda b, g: (0, 0)),
                  pl.BlockSpec((1, LANES), lambda b, g: (0, 0)),
                  tab, tab],
        out_specs=pl.BlockSpec((None, None, nch, LANES), lambda b, g: (b, g, 0, 0)),
        compiler_params=_params(("parallel", "parallel")),
        name="nsa_compress",
    )(u, pos_emb.reshape(1, CMP_LEN * HEAD), w1, b1.reshape(1, CMP_HIDDEN), w2d, b2d, cos_e, sin_e)


def _cmp_attn_kernel(q_ref, kc_ref, vc_ref, ov_ref, o_ref, sel_ref, *, tq, n_cmp, n_slc):
    qi = pl.program_id(1)
    nk = kc_ref.shape[0]
    q = q_ref[...].reshape(B_HEADS * tq, LANES)
    s = lax.dot_general(kc_ref[...], q, _NT, preferred_element_type=F32)
    t_idx = qi * tq + lax.broadcasted_iota(I32, (nk, tq), 1)
    c_idx = lax.broadcasted_iota(I32, (nk, tq), 0)
    mask = ((c_idx * CMP_STRIDE + (CMP_LEN - 1)) <= t_idx) & (c_idx < n_cmp)
    masks = [mask] * B_HEADS
    s = _mask_heads(s, masks, tq, NEG)
    m = s.max(axis=0, keepdims=True)
    e = _mask_heads(jnp.exp(s - m), masks, tq, 0.0)
    den = e.sum(axis=0, keepdims=True)
    p = e / jnp.where(den > 0, den, 1.0)
    o_t = lax.dot_general(vc_ref[...], p.astype(BF16), _TN, preferred_element_type=F32)
    hpg = B_HEADS // B_GROUPS
    _emit_heads(o_ref, o_t, tuple(h // hpg for h in range(B_HEADS)), tq)

    j_idx = lax.broadcasted_iota(I32, (n_slc, tq), 0)
    t_col = qi * tq + lax.broadcasted_iota(I32, (n_slc, tq), 1)
    cur = t_col // SLC_BLOCK
    forced = (j_idx == 0) | (j_idx == cur) | (j_idx == cur - 1)
    valid = j_idx * SLC_BLOCK <= t_col
    for g in range(B_GROUPS):
        ps = p[:, g * hpg * tq:(g * hpg + 1) * tq]
        for h in range(g * hpg + 1, (g + 1) * hpg):
            ps = ps + p[:, h * tq:(h + 1) * tq]
        hi = ps.astype(BF16)
        lo = (ps - hi.astype(F32)).astype(BF16)
        imp = (jnp.dot(ov_ref[...], hi, preferred_element_type=F32)
               + jnp.dot(ov_ref[...], lo, preferred_element_type=F32))
        score = jnp.where(forced, imp + FORCE_BONUS, jnp.where(valid, imp, -1.0))
        sel = jnp.zeros((n_slc, tq), F32)
        j_f = j_idx.astype(F32)
        for _ in range(min(N_SELECT, n_slc)):
            mx = score.max(axis=0, keepdims=True)
            idx = jnp.where(score == mx, j_f, float(n_slc)).min(axis=0, keepdims=True)
            pick = j_f == idx
            sel = jnp.where(pick, 1.0, sel)
            score = jnp.where(pick, LOWEST, score)
        sel_ref[g] = sel


def _cmp_attention(qb, kcc, vcc, B, S):
    tq = min(256, S)
    nk = S // CMP_STRIDE
    n_cmp = (S - CMP_LEN) // CMP_STRIDE + 1
    n_slc = S // SLC_BLOCK
    cmp_tok = np.arange(n_cmp)[:, None] * CMP_STRIDE + np.arange(CMP_LEN)[None, :]
    overlap = ((cmp_tok[:, :, None] // SLC_BLOCK) == np.arange(n_slc)[None, None, :]).mean(axis=1)
    ov_t = np.zeros((n_slc, nk), np.float32)
    ov_t[:, :n_cmp] = overlap.T
    kern = functools.partial(_cmp_attn_kernel, tq=tq, n_cmp=n_cmp, n_slc=n_slc)
    o, sel = pl.pallas_call(
        kern,
        out_shape=[jax.ShapeDtypeStruct((B, S, B_W), F32),
                   jax.ShapeDtypeStruct((B, B_GROUPS, n_slc, S), F32)],
        grid=(B, S // tq),
        in_specs=[pl.BlockSpec((B_HEADS, None, tq, LANES), lambda b, qi: (0, b, qi, 0)),
                  pl.BlockSpec((None, nk, LANES), lambda b, qi: (b, 0, 0)),
                  pl.BlockSpec((None, nk, LANES), lambda b, qi: (b, 0, 0)),
                  pl.BlockSpec((n_slc, nk), lambda b, qi: (0, 0))],
        out_specs=[pl.BlockSpec((None, tq, B_W), lambda b, qi: (b, qi, 0)),
                   pl.BlockSpec((None, B_GROUPS, n_slc, tq), lambda b, qi: (b, 0, 0, qi))],
        compiler_params=_params(("parallel", "parallel")),
        name="cmp_attn",
    )(qb.reshape(B_HEADS, B, S, LANES), kcc, vcc, jnp.asarray(ov_t, BF16))
    return o.reshape(B * S, B_W), sel


def _mla_prep_kernel(cq_ref, ckv_ref, misc_ref, gq_ref, gkv_ref, wq_ref, wk_ref, wv_ref, cc_ref, sc_ref,
                     q_ref, k_ref, v_ref):
    tm = cq_ref.shape[0]
    lane = lax.broadcasted_iota(I32, (tm, LANES), 1)
    cos = cc_ref[...]
    sin = sc_ref[...]

    def rope(t):
        return t * cos + _swap16(t, lane) * sin

    nq = _rms(cq_ref[...], gq_ref[...]).astype(BF16)
    q = jnp.dot(nq, wq_ref[...], preferred_element_type=F32)
    scale = (QK_NOPE + QK_ROPE) ** -0.5
    for h in range(C_HEADS):
        q_ref[h] = (rope(q[:, h * LANES:(h + 1) * LANES]) * scale).astype(BF16)
    nkv = _rms(ckv_ref[...], gkv_ref[...]).astype(BF16)
    k = jnp.dot(nkv, wk_ref[...], preferred_element_type=F32)
    in_rope = (lane >= KR_LANE) & (lane < KR_LANE + QK_ROPE)
    kr = jnp.where(in_rope, rope(misc_ref[...]), 0.0)
    for h in range(C_HEADS):
        k_ref[:, h * LANES:(h + 1) * LANES] = (k[:, h * LANES:(h + 1) * LANES] + kr).astype(BF16)
    v_ref[...] = jnp.dot(nkv, wv_ref[...], preferred_element_type=F32).astype(BF16)


def _mla_prep(cq, ckv, misc, g_q, g_kv, wq_p, wk_p, wv_p, cc, sc):
    T = cq.shape[0]
    tm = min(512, T)
    row = lambda i: (i, 0)
    fix = lambda i: (0, 0)
    t128 = pl.BlockSpec((tm, LANES), row)
    return pl.pallas_call(
        _mla_prep_kernel,
        out_shape=[jax.ShapeDtypeStruct((C_HEADS, T, LANES), BF16),
                   jax.ShapeDtypeStruct((T, C_HEADS * LANES), BF16),
                   jax.ShapeDtypeStruct((T, C_W), BF16)],
        grid=(T // tm,),
        in_specs=[pl.BlockSpec((tm, Q_LORA), row), t128, t128,
                  pl.BlockSpec((1, Q_LORA), fix), pl.BlockSpec((1, KV_LORA), fix),
                  pl.BlockSpec((Q_LORA, C_HEADS * LANES), fix),
                  pl.BlockSpec((KV_LORA, C_HEADS * LANES), fix),
                  pl.BlockSpec((KV_LORA, C_W), fix), t128, t128],
        out_specs=[pl.BlockSpec((C_HEADS, tm, LANES), lambda i: (0, i, 0)),
                   pl.BlockSpec((tm, C_HEADS * LANES), row),
                   pl.BlockSpec((tm, C_W), row)],
        compiler_params=_params(("parallel",)),
        name="mla_prep",
    )(cq, ckv, misc, g_q, g_kv, wq_p, wk_p, wv_p, cc, sc)


def _outproj_kernel(x_ref, mod_ref, o1_ref, o2_ref, o3_ref, l1_ref, l2_ref, l3_ref,
                    ocmp_ref, oslc_ref, owin_ref, misc_ref, ge_ref, oc_ref,
                    ga_ref, gb_ref, gc_ref, w_ref, out_ref, scr):
    tm = x_ref.shape[0]

    def token_major(ref, d):
        if d == 1:
            return ref[...]
        for r in range(d):
            for p in range(A_W // LANES):
                scr[p, pl.ds(r, tm // d, stride=d), :] = ref[:, r * A_W + p * LANES:r * A_W + (p + 1) * LANES]
        return jnp.concatenate([scr[p] for p in range(A_W // LANES)], axis=1)

    o1, o2, o3 = [token_major(r, d) for r, d in zip((o1_ref, o2_ref, o3_ref), A_DILS)]
    l1, l2, l3 = [token_major(r, d) for r, d in zip((l1_ref, l2_ref, l3_ref), A_DILS)]
    mx = jnp.maximum(jnp.maximum(l1, l2), l3)
    e1, e2, e3 = jnp.exp(l1 - mx), jnp.exp(l2 - mx), jnp.exp(l3 - mx)
    oa = (e1 * o1 + e2 * o2 + e3 * o3) / (e1 + e2 + e3)

    sg = jax.nn.sigmoid(misc_ref[...])
    hi = sg.astype(BF16)
    lo = (sg - hi.astype(F32)).astype(BF16)

    def gate(br):
        return (jnp.dot(hi, ge_ref[br], preferred_element_type=F32)
                + jnp.dot(lo, ge_ref[br], preferred_element_type=F32))

    ob = gate(0) * ocmp_ref[...] + gate(1) * oslc_ref[...] + gate(2) * owin_ref[...]
    na = _rms(oa, ga_ref[...]).astype(BF16)
    nb = _rms(ob, gb_ref[...]).astype(BF16)
    nc = _rms(oc_ref[...], gc_ref[...]).astype(BF16)
    mixed = (jnp.dot(na, w_ref[0:A_W, :], preferred_element_type=F32)
             + jnp.dot(nb, w_ref[A_W:A_W + B_W, :], preferred_element_type=F32)
             + jnp.dot(nc, w_ref[A_W + B_W:, :], preferred_element_type=F32))
    out_ref[...] = x_ref[...] + mod_ref[0, 2:3, :] * mixed


def _gate_expand():
    ge = np.zeros((3, LANES, B_W), np.float32)
    for h in range(B_HEADS):
        for br in range(3):
            ge[br, h * 3 + br, h * HEAD:(h + 1) * HEAD] = 1.0
    return jnp.asarray(ge, BF16)


def _out_proj(x2, mods_l, oa_parts, lse_parts, ocmp, oslc, owin, misc, oc, g_a, g_b, g_c, w_out_b, S):
    T, D = x2.shape
    tm = min(256, S)
    spb = S // tm
    row = lambda i: (i, 0)
    fix = lambda i: (0, 0)
    t128 = pl.BlockSpec((tm, LANES), row)
    t256 = pl.BlockSpec((tm, 256), row)
    t512 = pl.BlockSpec((tm, 512), row)
    dil = [pl.BlockSpec((tm // d, d * A_W), row) for d in A_DILS]
    return pl.pallas_call(
        _outproj_kernel,
        out_shape=jax.ShapeDtypeStruct((T, D), F32),
        grid=(T // tm,),
        in_specs=[pl.BlockSpec((tm, D), row),
                  pl.BlockSpec((1, 6, D), lambda i: (i // spb, 0, 0)),
                  *dil, *dil, t512, t512, t512, t128,
                  pl.BlockSpec((3, LANES, B_W), lambda i: (0, 0, 0)),
                  t256,
                  pl.BlockSpec((1, A_W), fix), pl.BlockSpec((1, B_W), fix), pl.BlockSpec((1, C_W), fix),
                  pl.BlockSpec((D, D), fix)],
        out_specs=pl.BlockSpec((tm, D), row),
        scratch_shapes=[pltpu.VMEM((A_W // LANES, tm, LANES), F32)],
        compiler_params=_params(("parallel",)),
        name="out_proj",
    )(x2, mods_l, *oa_parts, *lse_parts, ocmp, oslc, owin, misc, _gate_expand(), oc, g_a, g_b, g_c, w_out_b)


def _pack_halves(v):
    bits = lax.bitcast_convert_type(v.astype(BF16).astype(F32), U32)
    half = v.shape[1] // 2
    return (bits[:, half:] & jnp.uint32(0xFFFF0000)) | (bits[:, :half] >> 16)


def _unpack_halves(w):
    lo = lax.bitcast_convert_type(w << 16, F32)
    hi = lax.bitcast_convert_type(w & jnp.uint32(0xFFFF0000), F32)
    return jnp.concatenate([lo, hi], axis=1)


def _put_planes(ref, w):
    for c in range(ref.shape[0]):
        ref[c] = w[:, c * LANES:(c + 1) * LANES]


def _get_planes(ref):
    return jnp.concatenate([ref[c] for c in range(ref.shape[0])], axis=1)


def _route_kernel(x_ref, mod_ref, g_ref, wh_ref, wl_ref, b_ref, hp_ref, e_ref, r_ref, gt_ref, cnt_ref, carry):
    tm = x_ref.shape[0]

    @pl.when(pl.program_id(0) == 0)
    def _():
        carry[...] = jnp.zeros(carry.shape, F32)

    h = _rms(x_ref[...], g_ref[...]) * (1.0 + mod_ref[0, 4:5, :]) + mod_ref[0, 3:4, :]
    hb = h.astype(BF16)
    hlo = (h - hb.astype(F32)).astype(BF16)
    _put_planes(hp_ref, _pack_halves(h))

    logits = (jnp.dot(hb, wh_ref[...], preferred_element_type=F32)
              + jnp.dot(hlo, wh_ref[...], preferred_element_type=F32)
              + jnp.dot(hb, wl_ref[...], preferred_element_type=F32)) + b_ref[...]
    lane = lax.broadcasted_iota(I32, (tm, LANES), 1)
    lane_f = lane.astype(F32)
    vals, idxs, picks = [], [], []
    lg = logits
    for _ in range(TOP_K):
        mx = lg.max(axis=-1, keepdims=True)
        idx = jnp.where(lg == mx, lane_f, float(LANES)).min(axis=-1, keepdims=True)
        pick = lane_f == idx
        vals.append(mx)
        idxs.append(idx.astype(I32))
        picks.append(pick)
        lg = jnp.where(pick, LOWEST, lg)
    es = [jnp.exp(v - vals[0]) for v in vals]
    den = es[0] + es[1] + es[2] + es[3]
    onehot = jnp.zeros((tm, LANES), F32)
    for pk in picks:
        onehot = jnp.where(pk, 1.0, onehot)
    tri = (lax.broadcasted_iota(I32, (tm, tm), 0) >= lax.broadcasted_iota(I32, (tm, tm), 1)).astype(BF16)
    cs = jnp.dot(tri, onehot.astype(BF16), preferred_element_type=F32)
    rank = carry[...] + cs - 1.0
    e_out = jnp.zeros((tm, LANES), I32)
    r_out = jnp.zeros((tm, LANES), I32)
    g_out = jnp.zeros((tm, LANES), F32)
    for k in range(TOP_K):
        rk = jnp.where(picks[k], rank, 0.0).sum(axis=-1, keepdims=True)
        e_out = jnp.where(lane == k, idxs[k], e_out)
        r_out = jnp.where(lane == k, rk.astype(I32), r_out)
        g_out = jnp.where(lane == k, es[k] / den, g_out)
    e_ref[...] = e_out
    r_ref[...] = r_out
    gt_ref[...] = g_out
    new = carry[...] + onehot.sum(axis=0, keepdims=True)
    carry[...] = new
    cnt_ref[...] = new


def _route(x2, mods_l, g2, wr_hi, wr_lo, br_p, S):
    T, D = x2.shape
    tm = min(256, S)
    spb = S // tm
    row = lambda i: (i, 0)
    fix = lambda i: (0, 0)
    t128 = pl.BlockSpec((tm, LANES), row)
    return pl.pallas_call(
        _route_kernel,
        out_shape=[jax.ShapeDtypeStruct((D // 2 // LANES, T, LANES), U32),
                   jax.ShapeDtypeStruct((T, LANES), I32), jax.ShapeDtypeStruct((T, LANES), I32),
                   jax.ShapeDtypeStruct((T, LANES), F32), jax.ShapeDtypeStruct((1, LANES), F32)],
        grid=(T // tm,),
        in_specs=[pl.BlockSpec((tm, D), row),
                  pl.BlockSpec((1, 6, D), lambda i: (i // spb, 0, 0)),
                  pl.BlockSpec((1, D), fix),
                  pl.BlockSpec((D, LANES), fix), pl.BlockSpec((D, LANES), fix), pl.BlockSpec((1, LANES), fix)],
        out_specs=[pl.BlockSpec((D // 2 // LANES, tm, LANES), lambda i: (0, i, 0)), t128, t128, t128,
                   pl.BlockSpec((1, LANES), fix)],
        scratch_shapes=[pltpu.VMEM((1, LANES), F32)],
        compiler_params=_params(("arbitrary",)),
        name="moe_route",
    )(x2, mods_l, g2, wr_hi, wr_lo, br_p)


SC_GATHER_ROWS = 128


def _sc_gather_rows(x, idx, name):
    n = idx.shape[0]
    W = x.shape[1]
    mesh = plsc.VectorSubcoreMesh(core_axis_name="core", subcore_axis_name="subcore")

    def body(x_hbm, i_hbm, o_hbm):
        def step(i_vmem, o_vmem):
            pltpu.sync_copy(x_hbm.at[i_vmem.at[0]], o_vmem)

        pltpu.emit_pipeline(
            step,
            grid=(n // SC_GATHER_ROWS,),
            in_specs=[pl.BlockSpec((1, SC_GATHER_ROWS), lambda i: (0, i))],
            out_specs=[pl.BlockSpec((SC_GATHER_ROWS, W), lambda i: (i, 0))],
            core_axis_name=("core", "subcore"),
            dimension_semantics=(pltpu.PARALLEL,),
        )(i_hbm, o_hbm)

    return pl.kernel(body, out_type=jax.ShapeDtypeStruct((n, W), x.dtype), mesh=mesh,
                     scratch_types=[], name=name)(x, idx.reshape(1, n))


SPLIT_COLS = 256


def _ffn_kernel(te_ref, nv_ref, chg_ref, x_ref, w1_ref, b1_ref, w2_ref, b2_ref, perm_ref, y_ref, w1b, w2b):
    j = pl.program_id(0)

    @pl.when(chg_ref[j] == 1)
    def _():
        half = SPLIT_COLS // 2
        for b in range(2 * D_FF // SPLIT_COLS):
            t = jnp.dot(w1_ref[0, :, b * SPLIT_COLS:(b + 1) * SPLIT_COLS].astype(BF16), perm_ref[...],
                        preferred_element_type=F32)
            w1b[:, b * half:(b + 1) * half] = t[:, :half].astype(BF16)
            w1b[:, D_FF + b * half:D_FF + (b + 1) * half] = t[:, half:].astype(BF16)
        w2b[...] = w2_ref[0].astype(BF16)

    @pl.when(j >= nv_ref[0])
    def _():
        y_ref[...] = jnp.zeros(y_ref.shape, U32)

    @pl.when(j < nv_ref[0])
    def _():
        xb = _unpack_halves(_get_planes(x_ref)).astype(BF16)
        u = jnp.dot(xb, w1b[...], preferred_element_type=F32) + b1_ref[0]
        glu = jnp.minimum(u[:, :D_FF], SWIGLU_LIMIT)
        lin = jnp.clip(u[:, D_FF:], -SWIGLU_LIMIT, SWIGLU_LIMIT)
        a = glu * jax.nn.sigmoid(SWIGLU_ALPHA * glu) * (lin + 1.0)
        y = jnp.dot(a.astype(BF16), w2b[...], preferred_element_type=F32) + b2_ref[0]
        _put_planes(y_ref, _pack_halves(y))


def _split_perm():
    p = np.zeros((SPLIT_COLS, SPLIT_COLS), np.float32)
    i = np.arange(SPLIT_COLS // 2)
    p[2 * i, i] = 1.0
    p[2 * i + 1, SPLIT_COLS // 2 + i] = 1.0
    return jnp.asarray(p, BF16)


def _expert_ffn(tile_e, n_valid, changed, xs, w1, b1p, w2, b2, tmf):
    P, n_rows, _ = xs.shape
    D = 2 * P * LANES
    ntiles = n_rows // tmf
    return pl.pallas_call(
        _ffn_kernel,
        out_shape=jax.ShapeDtypeStruct((P, n_rows, LANES), U32),
        grid_spec=pltpu.PrefetchScalarGridSpec(
            num_scalar_prefetch=3,
            grid=(ntiles,),
            in_specs=[pl.BlockSpec((P, tmf, LANES), lambda j, te, nv, ch: (0, jnp.minimum(j, nv[0] - 1), 0)),
                      pl.BlockSpec((1, D, 2 * D_FF), lambda j, te, nv, ch: (te[j], 0, 0)),
                      pl.BlockSpec((1, 1, 2 * D_FF), lambda j, te, nv, ch: (te[j], 0, 0)),
                      pl.BlockSpec((1, D_FF, D), lambda j, te, nv, ch: (te[j], 0, 0)),
                      pl.BlockSpec((1, 1, D), lambda j, te, nv, ch: (te[j], 0, 0)),
                      pl.BlockSpec((SPLIT_COLS, SPLIT_COLS), lambda j, te, nv, ch: (0, 0))],
            out_specs=pl.BlockSpec((P, tmf, LANES), lambda j, te, nv, ch: (0, j, 0)),
            scratch_shapes=[pltpu.VMEM((D, 2 * D_FF), BF16), pltpu.VMEM((D_FF, D), BF16)]),
        compiler_params=_params(("arbitrary",)),
        name="moe_ffn",
    )(tile_e, n_valid, changed, xs, w1, b1p, w2, b2, _split_perm())


def _combine_kernel(yg_ref, gt_ref, x_ref, mod_ref, o_ref):
    g = gt_ref[...]
    y = None
    for k in range(TOP_K):
        yk = _unpack_halves(jnp.concatenate([yg_ref[c, k] for c in range(yg_ref.shape[0])], axis=1))
        y = g[:, k:k + 1] * yk if y is None else y + g[:, k:k + 1] * yk
    o_ref[...] = x_ref[...] + mod_ref[0, 5:6, :] * y


def _combine(yg, gates, x2, mods_l, S):
    T, D = x2.shape
    P = yg.shape[0]
    tc = min(512, S)
    spb = S // tc
    return pl.pallas_call(
        _combine_kernel,
        out_shape=jax.ShapeDtypeStruct((T, D), F32),
        grid=(T // tc,),
        in_specs=[pl.BlockSpec((P, TOP_K, tc, LANES), lambda i: (0, 0, i, 0)),
                  pl.BlockSpec((tc, LANES), lambda i: (i, 0)),
                  pl.BlockSpec((tc, D), lambda i: (i, 0)),
                  pl.BlockSpec((1, 6, D), lambda i: (i // spb, 0, 0))],
        out_specs=pl.BlockSpec((tc, D), lambda i: (i, 0)),
        compiler_params=_params(("parallel",)),
        name="moe_combine",
    )(yg, gates, x2, mods_l)


def _moe(x2, mods_l, g2, w_router, b_router, w1, b1p, w2, b2, S):
    T, D = x2.shape
    tmf = 256
    wr = jnp.zeros((D, LANES), F32).at[:, :N_EXPERTS].set(w_router)
    wr_hi = wr.astype(BF16)
    wr_lo = (wr - wr_hi.astype(F32)).astype(BF16)
    br_p = jnp.full((1, LANES), NEG, F32).at[0, :N_EXPERTS].set(b_router)
    hp, e_idx, ranks, gates, cnt = _route(x2, mods_l, g2, wr_hi, wr_lo, br_p, S)

    counts = cnt[0, :N_EXPERTS].astype(I32)
    padded = ((counts + tmf - 1) // tmf) * tmf
    ends = jnp.cumsum(padded)
    offs = ends - padded
    dest = offs[e_idx[:, :TOP_K]] + ranks[:, :TOP_K]
    n_rows = TOP_K * T + N_EXPERTS * tmf
    ntiles = n_rows // tmf
    n_valid = (ends[-1] // tmf).astype(I32).reshape(1)
    tile_start = jnp.minimum(jnp.arange(ntiles, dtype=I32), n_valid[0] - 1) * tmf
    tile_e = jnp.sum(tile_start[:, None] >= ends[None, :], axis=1).astype(I32)
    src = jnp.zeros((n_rows,), I32).at[dest.reshape(TOP_K * T)].set(
        jnp.arange(TOP_K * T, dtype=I32) // TOP_K, unique_indices=True)
    changed = jnp.concatenate([jnp.ones((1,), I32), (tile_e[1:] != tile_e[:-1]).astype(I32)])
    P = hp.shape[0]
    plane = jnp.arange(P, dtype=I32)[:, None]
    src_rows = (src[None, :] + plane * T).reshape(P * n_rows)
    dest_rows = (dest.T.reshape(1, TOP_K * T) + plane * n_rows).reshape(P * TOP_K * T)

    xs = _sc_gather_rows(hp.reshape(P * T, LANES), src_rows, "moe_dispatch_sc").reshape(P, n_rows, LANES)
    ys = _expert_ffn(tile_e, n_valid, changed, xs, w1, b1p, w2, b2, tmf)
    yg = _sc_gather_rows(ys.reshape(P * n_rows, LANES), dest_rows, "moe_collect_sc").reshape(P, TOP_K, T, LANES)
    return _combine(yg, gates, x2, mods_l, S)


def _final_kernel(x_ref, g_ref, o_ref):
    o_ref[...] = _rms(x_ref[...], g_ref[...])


def _final_norm(x2, g):
    T, D = x2.shape
    tm = min(512, T)
    return pl.pallas_call(
        _final_kernel,
        out_shape=jax.ShapeDtypeStruct((T, D), F32),
        grid=(T // tm,),
        in_specs=[pl.BlockSpec((tm, D), lambda i: (i, 0)), pl.BlockSpec((1, D), lambda i: (0, 0))],
        out_specs=pl.BlockSpec((tm, D), lambda i: (i, 0)),
        compiler_params=_params(("parallel",)),
        name="final_norm",
    )(x2, g.reshape(1, D))


def _rearrange_w_in(w):
    D = w.shape[0]
    main = w[:, :2048]
    gates = w[:, 2048:2072]
    cq = w[:, 2072:2328]
    ckv = w[:, 2328:2456]
    kr = w[:, 2456:2488]
    z = lambda n: jnp.zeros((D, n), w.dtype)
    misc = jnp.concatenate([gates, z(KR_LANE - 24), kr, z(LANES - KR_LANE - QK_ROPE)], axis=1)
    return jnp.concatenate([main, cq, ckv, misc], axis=1).astype(BF16)


def _mla_weights(w_q_up, w_kv_up):
    dq = QK_NOPE + QK_ROPE
    wq = jnp.zeros((Q_LORA, C_HEADS * LANES), F32)
    wk = jnp.zeros((KV_LORA, C_HEADS * LANES), F32)
    wv = []
    for h in range(C_HEADS):
        wq = wq.at[:, h * LANES:h * LANES + dq].set(w_q_up[:, h * dq:(h + 1) * dq])
        kv = w_kv_up[:, h * (QK_NOPE + HEAD):(h + 1) * (QK_NOPE + HEAD)]
        wk = wk.at[:, h * LANES:h * LANES + QK_NOPE].set(kv[:, :QK_NOPE])
        wv.append(kv[:, QK_NOPE:])
    return wq.astype(BF16), wk.astype(BF16), jnp.concatenate(wv, axis=1).astype(BF16)


def kernel(x, c, positions, w_ada, b_ada, g_norm1, g_norm2, w_in, nsa_pos_k, nsa_pos_v, nsa_w1_k, nsa_b1_k, nsa_w2_k, nsa_b2_k, nsa_w1_v, nsa_b1_v, nsa_w2_v, nsa_b2_v, mla_g_q, mla_g_kv, mla_w_q_up, mla_w_kv_up, g_out_a, g_out_b, g_out_c, w_out, w_router, b_router, w_exp1, b_exp1, w_exp2, b_exp2, g_final):
    B, S, D = x.shape
    T = B * S
    depth = w_ada.shape[0]
    x2 = x.reshape(T, D)

    cab, sab, cc, sc = _rope_tables(positions.reshape(T, 1))
    nch = S // CMP_STRIDE
    n_cmp = (S - CMP_LEN) // CMP_STRIDE + 1
    end_idx = np.minimum(np.arange(nch) * CMP_STRIDE + CMP_LEN - 1, S - 1)
    cos_e = cab.reshape(B, S, LANES)[:, end_idx]
    sin_e = sab.reshape(B, S, LANES)[:, end_idx]
    mods = _ada_mods(c, w_ada, b_ada).reshape(depth, B, 6, D)
    del n_cmp

    for l in range(depth):
        mods_l = mods[l]
        (qa, ka, va, qb, kcmp, vcmp, kslc, vslc, kwin, vwin, cq, ckv, misc) = _in_proj(
            x2, mods_l, g_norm1[l].reshape(1, D), _rearrange_w_in(w_in[l]), cab, sab, S)

        oa_parts, lse_parts = [], []
        for i, (window, d) in enumerate(A_PAIRS):
            o, lse = _dilated_attention(qa[i], ka[i], va[i], B, S, d, window // d)
            oa_parts.append(o)
            lse_parts.append(lse)

        kc = _compress(kcmp, nsa_pos_k[l], nsa_w1_k[l], nsa_b1_k[l], nsa_w2_k[l], nsa_b2_k[l],
                       cos_e, sin_e, B, S, True)
        vc = _compress(vcmp, nsa_pos_v[l], nsa_w1_v[l], nsa_b1_v[l], nsa_w2_v[l], nsa_b2_v[l],
                       cos_e, sin_e, B, S, False)
        kcc = jnp.concatenate([kc[:, 0, :, :HEAD], kc[:, 1, :, :HEAD]], axis=-1).astype(BF16)
        vcc = jnp.concatenate([vc[:, 0, :, :HEAD], vc[:, 1, :, :HEAD]], axis=-1).astype(BF16)
        ocmp, sel_t = _cmp_attention(qb, kcc, vcc, B, S)
        oslc = _select_attention(qb, kslc, vslc, sel_t, B, S)
        owin = _window_attention(qb, kwin, vwin, B, S)

        wq_p, wk_p, wv_p = _mla_weights(mla_w_q_up[l], mla_w_kv_up[l])
        qc, kcl, vcl = _mla_prep(cq, ckv, misc, mla_g_q[l].reshape(1, Q_LORA), mla_g_kv[l].reshape(1, KV_LORA),
                                 wq_p, wk_p, wv_p, cc, sc)
        oc = _latent_attention(qc, kcl, vcl, B, S)

        x2 = _out_proj(x2, mods_l, oa_parts, lse_parts, ocmp, oslc, owin, misc, oc,
                       g_out_a[l].reshape(1, A_W), g_out_b[l].reshape(1, B_W), g_out_c[l].reshape(1, C_W),
                       w_out[l].astype(BF16), S)

        b1r = b_exp1[l].reshape(N_EXPERTS, D_FF, 2)
        b1p = jnp.concatenate([b1r[..., 0], b1r[..., 1]], axis=-1).reshape(N_EXPERTS, 1, 2 * D_FF)
        x2 = _moe(x2, mods_l, g_norm2[l].reshape(1, D), w_router[l], b_router[l],
                  w_exp1[l], b1p, w_exp2[l], b_exp2[l].reshape(N_EXPERTS, 1, D), S)

    return _final_norm(x2, g_final).reshape(B, S, D)
```

```python
import functools

import numpy as np
import jax
import jax.numpy as jnp
from jax import lax
from jax.experimental import pallas as pl
from jax.experimental.pallas import tpu as pltpu
from jax.experimental.pallas import tpu_sc as plsc

F32 = jnp.float32
BF16 = jnp.bfloat16
I32 = jnp.int32
U32 = jnp.uint32

LANES = 128
HEAD = 64
NEG = -1e30
LOWEST = -3.0e38
LOG2E = 1.4426950408889634
LN2 = 0.6931471805599453
NORM_EPS = 1e-6
ROPE_THETA = 10000.0

A_HEADS, B_HEADS, B_GROUPS, C_HEADS = 4, 8, 2, 4
A_PAIRS = ((128, 1), (512, 4), (2048, 16))
A_W, B_W, C_W = 256, 512, 256
CMP_LEN, CMP_STRIDE, CMP_HIDDEN = 32, 16, 256
SLC_BLOCK, N_SELECT, WINDOW = 64, 16, 512
FORCE_BONUS = 1e4
Q_LORA, KV_LORA, QK_NOPE, QK_ROPE = 256, 128, 64, 32
N_EXPERTS, TOP_K, D_FF = 32, 4, 1024
SWIGLU_ALPHA, SWIGLU_LIMIT = 1.702, 7.0
IN_COLS = 2560
KR_LANE = 64

VMEM_LIMIT = 56 * 1024 * 1024


def _params(sem, vmem=VMEM_LIMIT):
    return pltpu.CompilerParams(dimension_semantics=sem, vmem_limit_bytes=vmem)


def _rms(x, g):
    return x * lax.rsqrt(jnp.mean(x * x, axis=-1, keepdims=True) + NORM_EPS) * g


def _rope_rows():
    inv64 = ROPE_THETA ** (-np.arange(32, dtype=np.float32) * (2.0 / 64))
    inv32 = ROPE_THETA ** (-np.arange(16, dtype=np.float32) * (2.0 / 32))
    inv_ab = np.tile(np.concatenate([inv64, inv64]), 2)
    sgn_ab = np.tile(np.concatenate([-np.ones(32), np.ones(32)]), 2)
    inv_c = np.zeros(LANES, np.float32)
    sgn_c = np.zeros(LANES, np.float32)
    inv_c[KR_LANE:KR_LANE + 16] = inv32
    inv_c[KR_LANE + 16:KR_LANE + 32] = inv32
    sgn_c[KR_LANE:KR_LANE + 16] = -1.0
    sgn_c[KR_LANE + 16:KR_LANE + 32] = 1.0
    rows = np.stack([inv_ab, sgn_ab, inv_c, sgn_c]).astype(np.float32)
    return jnp.asarray(rows)


def _rope_tab_kernel(pos_ref, rows_ref, cab_ref, sab_ref, cc_ref, sc_ref):
    pos = pos_ref[...].astype(F32)
    ang = pos * rows_ref[0:1, :]
    cab_ref[...] = jnp.cos(ang)
    sab_ref[...] = jnp.sin(ang) * rows_ref[1:2, :]
    ang = pos * rows_ref[2:3, :]
    cc_ref[...] = jnp.cos(ang)
    sc_ref[...] = jnp.sin(ang) * rows_ref[3:4, :]


def _rope_tables(pos_col):
    R = pos_col.shape[0]
    tm = min(R, 1024)
    spec = pl.BlockSpec((tm, LANES), lambda i: (i, 0))
    return pl.pallas_call(
        _rope_tab_kernel,
        out_shape=[jax.ShapeDtypeStruct((R, LANES), F32)] * 4,
        grid=(R // tm,),
        in_specs=[pl.BlockSpec((tm, 1), lambda i: (i, 0)), pl.BlockSpec((4, LANES), lambda i: (0, 0))],
        out_specs=[spec] * 4,
        compiler_params=_params(("parallel",)),
        name="rope_tables",
    )(pos_col, _rope_rows())


def _swap32(t, lane):
    return jnp.where((lane & 32) == 0, pltpu.roll(t, 96, 1), pltpu.roll(t, 32, 1))


def _swap16(t, lane):
    return jnp.where(lane < KR_LANE + 16, pltpu.roll(t, 112, 1), pltpu.roll(t, 16, 1))


def _mods_kernel(c_ref, w_ref, b_ref, o_ref):
    c = c_ref[...]
    ca = (c * jax.nn.sigmoid(c)).astype(BF16)
    o_ref[0] = jnp.dot(ca, w_ref[0].astype(BF16), preferred_element_type=F32) + b_ref[0]


def _ada_mods(c, w_ada, b_ada):
    L, D, N = w_ada.shape
    B = c.shape[0]
    tn = 1536
    return pl.pallas_call(
        _mods_kernel,
        out_shape=jax.ShapeDtypeStruct((L, B, N), F32),
        grid=(L, N // tn),
        in_specs=[pl.BlockSpec((B, D), lambda l, j: (0, 0)),
                  pl.BlockSpec((1, D, tn), lambda l, j: (l, 0, j)),
                  pl.BlockSpec((1, 1, tn), lambda l, j: (l, 0, j))],
        out_specs=pl.BlockSpec((1, B, tn), lambda l, j: (l, 0, j)),
        compiler_params=_params(("parallel", "parallel")),
        name="ada_mods",
    )(c, w_ada, b_ada.reshape(L, 1, N))


A_DILS = tuple(d for _, d in A_PAIRS)


def _inproj_kernel(x_ref, mod_ref, g_ref, w_ref, cab_ref, sab_ref, *refs):
    nd = len(A_DILS)
    qa_refs, ka_refs, va_refs = refs[0:nd], refs[nd:2 * nd], refs[2 * nd:3 * nd]
    (qb_ref, kcmp_ref, vcmp_ref, kslc_ref, vslc_ref, kwin_ref, vwin_ref, cq_ref, ckv_ref, misc_ref,
     scr) = refs[3 * nd:]
    tm = x_ref.shape[0]

    def spread(val, store):
        W = val.shape[1]
        for p in range(W // LANES):
            scr[p] = val[:, p * LANES:(p + 1) * LANES]
        for i, d in enumerate(A_DILS):
            if d == 1:
                store(i, 0, val.astype(BF16))
            else:
                for r in range(d):
                    planes = [scr[p, pl.ds(r, tm // d, stride=d), :] for p in range(W // LANES)]
                    piece = planes[0] if len(planes) == 1 else jnp.concatenate(planes, axis=1)
                    store(i, r, piece.astype(BF16))
    x = x_ref[...]
    h = (_rms(x, g_ref[...]) * (1.0 + mod_ref[0, 1:2, :]) + mod_ref[0, 0:1, :]).astype(BF16)
    cos = cab_ref[...]
    sin = sab_ref[...]
    lane = lax.broadcasted_iota(I32, (tm, LANES), 1)
    lo64 = lane < HEAD

    def proj(c0, n):
        return jnp.dot(h, w_ref[:, c0:c0 + n], preferred_element_type=F32)

    def rope(t):
        return t * cos + _swap32(t, lane) * sin

    def pad_q(h0, t, dst):
        t = rope(t) * (HEAD ** -0.5 * LOG2E)
        tr = pltpu.roll(t, HEAD, 1)
        return [jnp.where(lo64 if dst[i] == 0 else ~lo64, t if dst[i] == i else tr, 0.0) for i in range(2)]

    def put_a(arefs, width, lane0=0):
        def store(i, r, piece):
            arefs[i][:, r * A_W + lane0:r * A_W + lane0 + width] = piece
        return store

    y = proj(0, 256)
    for j in range(2):
        for i, qh in enumerate(pad_q(2 * j, y[:, 128 * j:128 * (j + 1)], (0, 1))):
            def store(di, r, piece, hd=2 * j + i):
                qa_refs[di][hd, :, r * LANES:(r + 1) * LANES] = piece

            spread(qh, store)
    y = proj(256, 256)
    spread(jnp.concatenate([rope(y[:, :128]), rope(y[:, 128:])], axis=1), put_a(ka_refs, A_W))
    spread(proj(512, 256), put_a(va_refs, A_W))
    for j in range(2):
        y = proj(768 + 256 * j, 256)
        for jj in range(2):
            for i, qh in enumerate(pad_q(4 * j + 2 * jj, y[:, 128 * jj:128 * (jj + 1)], (j, j))):
                qb_ref[4 * j + 2 * jj + i] = qh.astype(BF16)
    y = proj(1280, 256)
    kcmp_ref[...] = y[:, :128]
    vcmp_ref[...] = y[:, 128:]
    y = proj(1536, 256)
    kslc_ref[...] = rope(y[:, :128]).astype(BF16)
    vslc_ref[...] = y[:, 128:].astype(BF16)
    y = proj(1792, 256)
    kwin_ref[...] = rope(y[:, :128]).astype(BF16)
    vwin_ref[...] = y[:, 128:].astype(BF16)
    cq_ref[...] = proj(2048, 256)
    y = proj(2304, 256)
    ckv_ref[...] = y[:, :128]
    misc_ref[...] = y[:, 128:]


def _in_proj(x2, mods_l, g1, w_in_r, cab, sab, S):
    T, D = x2.shape
    tm = min(512, S)
    spb = S // tm
    row = lambda i: (i, 0)
    shp = jax.ShapeDtypeStruct
    outs = ([shp((A_HEADS, T // d, d * LANES), BF16) for d in A_DILS]
            + [shp((T // d, d * A_W), BF16) for d in A_DILS] * 2
            + [shp((B_HEADS, T, LANES), BF16),
               shp((T, LANES), F32), shp((T, LANES), F32), shp((T, LANES), BF16), shp((T, LANES), BF16),
               shp((T, LANES), BF16), shp((T, LANES), BF16),
               shp((T, Q_LORA), F32), shp((T, LANES), F32), shp((T, LANES), F32)])
    t128 = pl.BlockSpec((tm, LANES), row)
    t256 = pl.BlockSpec((tm, 256), row)
    out_specs = ([pl.BlockSpec((A_HEADS, tm // d, d * LANES), lambda i: (0, i, 0)) for d in A_DILS]
                 + [pl.BlockSpec((tm // d, d * A_W), row) for d in A_DILS] * 2
                 + [pl.BlockSpec((B_HEADS, tm, LANES), lambda i: (0, i, 0)),
                    t128, t128, t128, t128, t128, t128, t256, t128, t128])
    res = pl.pallas_call(
        _inproj_kernel,
        out_shape=outs,
        grid=(T // tm,),
        in_specs=[pl.BlockSpec((tm, D), row),
                  pl.BlockSpec((1, 6, D), lambda i: (i // spb, 0, 0)),
                  pl.BlockSpec((1, D), lambda i: (0, 0)),
                  pl.BlockSpec((D, IN_COLS), lambda i: (0, 0)),
                  t128, t128],
        out_specs=out_specs,
        scratch_shapes=[pltpu.VMEM((A_W // LANES, tm, LANES), F32)],
        compiler_params=_params(("parallel",)),
        name="in_proj",
    )(x2, mods_l, g1, w_in_r, cab, sab)
    nd = len(A_DILS)
    return (res[0:nd], res[nd:2 * nd], res[2 * nd:3 * nd]) + tuple(res[3 * nd:])


_NT = (((1,), (1,)), ((), ()))
_TN = (((0,), (0,)), ((), ()))


KEY_SUB = 128
QUERY_SUB = 256


def _online_chunk(k, v, q, keep, bias, m, l, acc):
    s = lax.dot_general(k, q, _NT, preferred_element_type=F32)
    if bias is not None:
        s = s + bias
    if keep is not None:
        s = jnp.where(keep, s, NEG)
    m_next = jnp.maximum(m, s.max(axis=0, keepdims=True))
    alpha = jnp.exp2(m - m_next)
    p = jnp.exp2(s - m_next)
    l = alpha * l + p.sum(axis=0, keepdims=True)
    acc = acc * alpha + lax.dot_general(v, p.astype(BF16), _TN, preferred_element_type=F32)
    return m_next, l, acc


def _mask_heads(s, masks, tq, fill):
    return jnp.concatenate([jnp.where(masks[h], s[:, h * tq:(h + 1) * tq], fill)
                            for h in range(len(masks))], axis=1)


def _emit_heads(o_ref, acc_t, halves, tq):
    nh = len(halves)
    lo64 = lax.broadcasted_iota(I32, (tq, LANES), 1) < HEAD
    for c in range(nh // 2):
        parts = []
        for h in (2 * c, 2 * c + 1):
            o = acc_t[:, h * tq:(h + 1) * tq].T
            if halves[h] != h % 2:
                o = pltpu.roll(o, HEAD, 1)
            parts.append(o)
        o_ref[:, c * LANES:(c + 1) * LANES] = jnp.where(lo64, parts[0], parts[1])


def _band_kernel(*refs, nblk, tq, tk, max_dist, stacks, halves, with_lse, qi_axis):
    q_ref = refs[0]
    k_refs = refs[1:1 + nblk]
    v_refs = refs[1 + nblk:1 + 2 * nblk]
    o_ref = refs[1 + 2 * nblk]
    lse_ref = refs[2 + 2 * nblk] if with_lse else None
    nh = len(halves)
    qi = pl.program_id(qi_axis)
    ks = min(KEY_SUB, tk)
    cq = min(QUERY_SUB, tq)
    rc = lax.broadcasted_iota(I32, (ks, cq), 0) - lax.broadcasted_iota(I32, (ks, cq), 1)
    head_blk = {}
    for (h0, n, blk) in stacks:
        for h in range(h0, h0 + n):
            head_blk[h] = blk
    outs, lses = [], []
    for h in range(nh):
        lanes = slice(head_blk[h] * LANES, (head_blk[h] + 1) * LANES)
        for c in range(tq // cq):
            q = q_ref[h, c * cq:(c + 1) * cq, :]
            m = jnp.full((1, cq), NEG, F32)
            l = jnp.zeros((1, cq), F32)
            acc = jnp.zeros((LANES, cq), F32)
            for i in range(nblk):
                kt = qi * (tq // tk) - (nblk - tq // tk) + i
                for kk in range(tk // ks):
                    base = qi * tq + c * cq - (kt * tk + kk * ks)
                    base = jnp.where(kt >= 0, base, -(1 << 20))
                    keep = (rc <= base) & (rc >= base - max_dist)
                    m, l, acc = _online_chunk(k_refs[i][kk * ks:(kk + 1) * ks, lanes],
                                              v_refs[i][kk * ks:(kk + 1) * ks, lanes], q, keep, None, m, l, acc)
            outs.append(acc / l)
            lses.append(jnp.broadcast_to(m * LN2 + jnp.log(l), (LANES, cq)))
    _emit_heads(o_ref, jnp.concatenate(outs, axis=1), halves, tq)
    if with_lse:
        _emit_heads(lse_ref, jnp.concatenate(lses, axis=1), tuple(h % 2 for h in range(nh)), tq)


def _dilated_attention(qa, ka, va, B, S, d, max_dist):
    L = S // d
    tq = min(256, L)
    tk = 128
    nprev = -(-max_dist // tk)
    nblk = nprev + tq // tk
    q_v = qa.reshape(A_HEADS, B, L, d * LANES)
    k_v = ka.reshape(B, L, d * A_W)
    v_v = va.reshape(B, L, d * A_W)
    kv_specs = []
    for i in range(nblk):
        kv_specs.append(pl.BlockSpec(
            (None, tk, A_W),
            functools.partial(lambda b, r, qi, i: (b, jnp.maximum(qi * (tq // tk) - nprev + i, 0), r), i=i)))
    out_spec = pl.BlockSpec((None, tq, A_W), lambda b, r, qi: (b, qi, r))
    kern = functools.partial(_band_kernel, nblk=nblk, tq=tq, tk=tk, max_dist=max_dist,
                             stacks=((0, 2, 0), (2, 2, 1)), halves=(0, 1, 0, 1), with_lse=True, qi_axis=2)
    o, lse = pl.pallas_call(
        kern,
        out_shape=[jax.ShapeDtypeStruct((B, L, d * A_W), F32)] * 2,
        grid=(B, d, L // tq),
        in_specs=[pl.BlockSpec((A_HEADS, None, tq, LANES), lambda b, r, qi: (0, b, qi, r))]
                 + kv_specs + kv_specs,
        out_specs=[out_spec, out_spec],
        compiler_params=_params(("parallel", "parallel", "parallel")),
        name=f"dilated_attn_d{d}",
    )(q_v, *([k_v] * nblk), *([v_v] * nblk))
    return o.reshape(B * L, d * A_W), lse.reshape(B * L, d * A_W)


def _window_attention(qb, kwin, vwin, B, S):
    tq = min(256, S)
    tk = tq
    max_dist = WINDOW - 1
    nprev = -(-max_dist // tk)
    nblk = nprev + 1
    q_v = qb.reshape(B_HEADS, B, S, LANES)
    k_v = kwin.reshape(B, S, LANES)
    v_v = vwin.reshape(B, S, LANES)
    kv_specs = [pl.BlockSpec((None, tk, LANES),
                             functools.partial(lambda b, qi, i: (b, jnp.maximum(qi - nprev + i, 0), 0), i=i))
                for i in range(nblk)]
    kern = functools.partial(_band_kernel, nblk=nblk, tq=tq, tk=tk, max_dist=max_dist,
                             stacks=((0, B_HEADS, 0),), halves=(0, 0, 0, 0, 1, 1, 1, 1),
                             with_lse=False, qi_axis=1)
    o = pl.pallas_call(
        kern,
        out_shape=jax.ShapeDtypeStruct((B, S, B_W), F32),
        grid=(B, S // tq),
        in_specs=[pl.BlockSpec((B_HEADS, None, tq, LANES), lambda b, qi: (0, b, qi, 0))]
                 + kv_specs + kv_specs,
        out_specs=pl.BlockSpec((None, tq, B_W), lambda b, qi: (b, qi, 0)),
        compiler_params=_params(("parallel", "parallel")),
        name="window_attn",
    )(q_v, *([k_v] * nblk), *([v_v] * nblk))
    return o.reshape(B * S, B_W)


def _flash_kernel(qi_ref, kj_ref, first_ref, last_ref, *refs, tq, tk, qk_groups, pv_groups, halves, select):
    if select:
        q_ref, k_ref, v_ref, sel_ref, o_ref, m_sc, l_sc, acc_sc = refs
    else:
        q_ref, k_ref, v_ref, o_ref, m_sc, l_sc, acc_sc = refs
    nh = len(halves)
    step = pl.program_id(1)
    qi = qi_ref[step]
    kj = kj_ref[step]

    @pl.when(first_ref[step] == 1)
    def _():
        m_sc[...] = jnp.full(m_sc.shape, NEG, F32)
        l_sc[...] = jnp.zeros(l_sc.shape, F32)
        acc_sc[...] = jnp.zeros(acc_sc.shape, F32)

    ks = min(KEY_SUB, tk)
    cq = min(QUERY_SUB, tq)
    rc = lax.broadcasted_iota(I32, (ks, cq), 0) - lax.broadcasted_iota(I32, (ks, cq), 1)
    head_kb, head_vb = {}, {}
    for (h0, n, kb) in qk_groups:
        for h in range(h0, h0 + n):
            head_kb[h] = kb
    for (h0, n, vb) in pv_groups:
        for h in range(h0, h0 + n):
            head_vb[h] = vb
    hpg = nh // B_GROUPS
    spb = ks // SLC_BLOCK if select else 1

    def run(on_diagonal):
        for h in range(nh):
            klanes = slice(head_kb[h] * LANES, (head_kb[h] + 1) * LANES)
            vlanes = slice(head_vb[h] * LANES, (head_vb[h] + 1) * LANES)
            for c in range(tq // cq):
                cols = slice(h * tq + c * cq, h * tq + (c + 1) * cq)
                q = q_ref[h, c * cq:(c + 1) * cq, :]
                m, l, acc = m_sc[:, cols], l_sc[:, cols], acc_sc[:, cols]
                for kk in range(tk // ks):
                    keep = None
                    if on_diagonal:
                        keep = rc <= (qi * tq + c * cq) - (kj * tk + kk * ks)
                    bias = None
                    if select:
                        rows = [jnp.broadcast_to(
                            (sel_ref[h // hpg, pl.ds((kj * (tk // ks) + kk) * spb + j, 1), c * cq:(c + 1) * cq] - 1.0)
                            * (-NEG), (SLC_BLOCK, cq)) for j in range(spb)]
                        bias = rows[0] if spb == 1 else jnp.concatenate(rows, axis=0)
                    m, l, acc = _online_chunk(k_ref[kk * ks:(kk + 1) * ks, klanes],
                                              v_ref[kk * ks:(kk + 1) * ks, vlanes], q, keep, bias, m, l, acc)
                m_sc[:, cols] = m
                l_sc[:, cols] = l
                acc_sc[:, cols] = acc

    crosses = kj * tk + (tk - 1) > qi * tq

    @pl.when(crosses)
    def _():
        run(True)

    @pl.when(jnp.logical_not(crosses))
    def _():
        run(False)

    @pl.when(last_ref[step] == 1)
    def _():
        _emit_heads(o_ref, acc_sc[...] / l_sc[...], halves, tq)


def _causal_schedule(S, tq, tk):
    qi, kj, first, last = [], [], [], []
    for i in range(S // tq):
        nk = (i * tq + tq - 1) // tk + 1
        for j in range(nk):
            qi.append(i)
            kj.append(j)
            first.append(1 if j == 0 else 0)
            last.append(1 if j == nk - 1 else 0)
    return [jnp.asarray(np.asarray(a, np.int32)) for a in (qi, kj, first, last)]


def _select_attention(qb, kslc, vslc, sel_t, B, S):
    tq = min(512, S)
    tk = min(256, S)
    n_slc = S // SLC_BLOCK
    tabs = _causal_schedule(S, tq, tk)
    nsteps = tabs[0].shape[0]
    kern = functools.partial(_flash_kernel, tq=tq, tk=tk, qk_groups=((0, B_HEADS, 0),),
                             pv_groups=((0, B_HEADS, 0),), halves=(0, 0, 0, 0, 1, 1, 1, 1), select=True)
    o = pl.pallas_call(
        kern,
        out_shape=jax.ShapeDtypeStruct((B, S, B_W), F32),
        grid_spec=pltpu.PrefetchScalarGridSpec(
            num_scalar_prefetch=4,
            grid=(B, nsteps),
            in_specs=[pl.BlockSpec((B_HEADS, None, tq, LANES), lambda b, s, qi, kj, f, l: (0, b, qi[s], 0)),
                      pl.BlockSpec((None, tk, LANES), lambda b, s, qi, kj, f, l: (b, kj[s], 0)),
                      pl.BlockSpec((None, tk, LANES), lambda b, s, qi, kj, f, l: (b, kj[s], 0)),
                      pl.BlockSpec((None, B_GROUPS, n_slc, tq), lambda b, s, qi, kj, f, l: (b, 0, 0, qi[s]))],
            out_specs=pl.BlockSpec((None, tq, B_W), lambda b, s, qi, kj, f, l: (b, qi[s], 0)),
            scratch_shapes=[pltpu.VMEM((1, B_HEADS * tq), F32), pltpu.VMEM((1, B_HEADS * tq), F32),
                            pltpu.VMEM((LANES, B_HEADS * tq), F32)]),
        compiler_params=_params(("parallel", "arbitrary")),
        name="select_attn",
    )(*tabs, qb.reshape(B_HEADS, B, S, LANES), kslc.reshape(B, S, LANES), vslc.reshape(B, S, LANES), sel_t)
    return o.reshape(B * S, B_W)


def _latent_attention(qc, kc, vc, B, S):
    tq = min(512, S)
    tk = min(512, S)
    tabs = _causal_schedule(S, tq, tk)
    nsteps = tabs[0].shape[0]
    kern = functools.partial(_flash_kernel, tq=tq, tk=tk,
                             qk_groups=tuple((h, 1, h) for h in range(C_HEADS)),
                             pv_groups=((0, 2, 0), (2, 2, 1)), halves=(0, 1, 0, 1), select=False)
    o = pl.pallas_call(
        kern,
        out_shape=jax.ShapeDtypeStruct((B, S, C_W), F32),
        grid_spec=pltpu.PrefetchScalarGridSpec(
            num_scalar_prefetch=4,
            grid=(B, nsteps),
            in_specs=[pl.BlockSpec((C_HEADS, None, tq, LANES), lambda b, s, qi, kj, f, l: (0, b, qi[s], 0)),
                      pl.BlockSpec((None, tk, C_HEADS * LANES), lambda b, s, qi, kj, f, l: (b, kj[s], 0)),
                      pl.BlockSpec((None, tk, C_W), lambda b, s, qi, kj, f, l: (b, kj[s], 0))],
            out_specs=pl.BlockSpec((None, tq, C_W), lambda b, s, qi, kj, f, l: (b, qi[s], 0)),
            scratch_shapes=[pltpu.VMEM((1, C_HEADS * tq), F32), pltpu.VMEM((1, C_HEADS * tq), F32),
                            pltpu.VMEM((LANES, C_HEADS * tq), F32)]),
        compiler_params=_params(("parallel", "arbitrary")),
        name="latent_attn",
    )(*tabs, qc.reshape(C_HEADS, B, S, LANES), kc.reshape(B, S, C_HEADS * LANES), vc.reshape(B, S, C_W))
    return o.reshape(B * S, C_W)


def _gelu_tanh(x):
    return 0.5 * x * (1.0 + jnp.tanh(0.7978845608028654 * (x + 0.044715 * x * x * x)))


def _compress_kernel(u_ref, pos_ref, w1_ref, b1_ref, w2_ref, b2_ref, cos_ref, sin_ref, o_ref, *, rotate):
    n = u_ref.shape[0]
    half = CMP_STRIDE * HEAD
    u = u_ref[...]
    ua = (u + pos_ref[:, :half]).astype(BF16)
    ub = (u + pos_ref[:, half:]).astype(BF16)
    pa = jnp.dot(ua, w1_ref[:half, :].astype(BF16), preferred_element_type=F32)
    pb = jnp.dot(ub, w1_ref[half:, :].astype(BF16), preferred_element_type=F32)
    hid = _gelu_tanh(pa + pltpu.roll(pb, n - 1, 0) + b1_ref[...])
    y = jnp.dot(hid.astype(BF16), w2_ref[...].astype(BF16), preferred_element_type=F32) + b2_ref[...]
    if rotate:
        lane = lax.broadcasted_iota(I32, y.shape, 1)
        y = y * cos_ref[...] + _swap32(y, lane) * sin_ref[...]
    o_ref[...] = y


def _compress(t, pos_emb, w1, b1, w2, b2, cos_e, sin_e, B, S, rotate):
    nch = S // CMP_STRIDE
    u = t.reshape(B, S, B_GROUPS, HEAD).transpose(0, 2, 1, 3).reshape(B, B_GROUPS, nch, CMP_STRIDE * HEAD)
    w2d = jnp.concatenate([w2, w2], axis=1)
    b2d = jnp.concatenate([b2, b2]).reshape(1, LANES)
    kern = functools.partial(_compress_kernel, rotate=rotate)
    tab = pl.BlockSpec((None, nch, LANES), lambda b, g: (b, 0, 0))
    return pl.pallas_call(
        kern,
        out_shape=jax.ShapeDtypeStruct((B, B_GROUPS, nch, LANES), F32),
        grid=(B, B_GROUPS),
        in_specs=[pl.BlockSpec((None, None, nch, CMP_STRIDE * HEAD), lambda b, g: (b, g, 0, 0)),
                  pl.BlockSpec((1, CMP_LEN * HEAD), lambda b, g: (0, 0)),
                  pl.BlockSpec((CMP_LEN * HEAD, CMP_HIDDEN), lambda b, g: (0, 0)),
                  pl.BlockSpec((1, CMP_HIDDEN), lambda b, g: (0, 0)),
                  pl.BlockSpec((CMP_HIDDEN, LANES), lambda b, g: (0, 0)),
                  pl.BlockSpec((1, LANES), lambda b, g: (0, 0)),
                  tab, tab],
        out_specs=pl.BlockSpec((None, None, nch, LANES), lambda b, g: (b, g, 0, 0)),
        compiler_params=_params(("parallel", "parallel")),
        name="nsa_compress",
    )(u, pos_emb.reshape(1, CMP_LEN * HEAD), w1, b1.reshape(1, CMP_HIDDEN), w2d, b2d, cos_e, sin_e)


def _cmp_attn_kernel(q_ref, kc_ref, vc_ref, ov_ref, o_ref, sel_ref, *, tq, n_cmp, n_slc):
    qi = pl.program_id(1)
    nk = kc_ref.shape[0]
    q = q_ref[...].reshape(B_HEADS * tq, LANES)
    s = lax.dot_general(kc_ref[...], q, _NT, preferred_element_type=F32)
    t_idx = qi * tq + lax.broadcasted_iota(I32, (nk, tq), 1)
    c_idx = lax.broadcasted_iota(I32, (nk, tq), 0)
    mask = ((c_idx * CMP_STRIDE + (CMP_LEN - 1)) <= t_idx) & (c_idx < n_cmp)
    masks = [mask] * B_HEADS
    s = _mask_heads(s, masks, tq, NEG)
    m = s.max(axis=0, keepdims=True)
    e = _mask_heads(jnp.exp2(s - m), masks, tq, 0.0)
    den = e.sum(axis=0, keepdims=True)
    p = e / jnp.where(den > 0, den, 1.0)
    o_t = lax.dot_general(vc_ref[...], p.astype(BF16), _TN, preferred_element_type=F32)
    hpg = B_HEADS // B_GROUPS
    _emit_heads(o_ref, o_t, tuple(h // hpg for h in range(B_HEADS)), tq)

    j_idx = lax.broadcasted_iota(I32, (n_slc, tq), 0)
    t_col = qi * tq + lax.broadcasted_iota(I32, (n_slc, tq), 1)
    cur = t_col // SLC_BLOCK
    forced = (j_idx == 0) | (j_idx == cur) | (j_idx == cur - 1)
    valid = j_idx * SLC_BLOCK <= t_col
    for g in range(B_GROUPS):
        ps = p[:, g * hpg * tq:(g * hpg + 1) * tq]
        for h in range(g * hpg + 1, (g + 1) * hpg):
            ps = ps + p[:, h * tq:(h + 1) * tq]
        hi = ps.astype(BF16)
        lo = (ps - hi.astype(F32)).astype(BF16)
        imp = (jnp.dot(ov_ref[...], hi, preferred_element_type=F32)
               + jnp.dot(ov_ref[...], lo, preferred_element_type=F32))
        score = jnp.where(forced, imp + FORCE_BONUS, jnp.where(valid, imp, -1.0))
        sel = jnp.zeros((n_slc, tq), F32)
        j_f = j_idx.astype(F32)
        for _ in range(min(N_SELECT, n_slc)):
            mx = score.max(axis=0, keepdims=True)
            idx = jnp.where(score == mx, j_f, float(n_slc)).min(axis=0, keepdims=True)
            pick = j_f == idx
            sel = jnp.where(pick, 1.0, sel)
            score = jnp.where(pick, LOWEST, score)
        sel_ref[g] = sel


def _cmp_attention(qb, kcc, vcc, B, S):
    tq = min(256, S)
    nk = S // CMP_STRIDE
    n_cmp = (S - CMP_LEN) // CMP_STRIDE + 1
    n_slc = S // SLC_BLOCK
    cmp_tok = np.arange(n_cmp)[:, None] * CMP_STRIDE + np.arange(CMP_LEN)[None, :]
    overlap = ((cmp_tok[:, :, None] // SLC_BLOCK) == np.arange(n_slc)[None, None, :]).mean(axis=1)
    ov_t = np.zeros((n_slc, nk), np.float32)
    ov_t[:, :n_cmp] = overlap.T
    kern = functools.partial(_cmp_attn_kernel, tq=tq, n_cmp=n_cmp, n_slc=n_slc)
    o, sel = pl.pallas_call(
        kern,
        out_shape=[jax.ShapeDtypeStruct((B, S, B_W), F32),
                   jax.ShapeDtypeStruct((B, B_GROUPS, n_slc, S), F32)],
        grid=(B, S // tq),
        in_specs=[pl.BlockSpec((B_HEADS, None, tq, LANES), lambda b, qi: (0, b, qi, 0)),
                  pl.BlockSpec((None, nk, LANES), lambda b, qi: (b, 0, 0)),
                  pl.BlockSpec((None, nk, LANES), lambda b, qi: (b, 0, 0)),
                  pl.BlockSpec((n_slc, nk), lambda b, qi: (0, 0))],
        out_specs=[pl.BlockSpec((None, tq, B_W), lambda b, qi: (b, qi, 0)),
                   pl.BlockSpec((None, B_GROUPS, n_slc, tq), lambda b, qi: (b, 0, 0, qi))],
        compiler_params=_params(("parallel", "parallel")),
        name="cmp_attn",
    )(qb.reshape(B_HEADS, B, S, LANES), kcc, vcc, jnp.asarray(ov_t, BF16))
    return o.reshape(B * S, B_W), sel


def _mla_prep_kernel(cq_ref, ckv_ref, misc_ref, gq_ref, gkv_ref, wq_ref, wk_ref, wv_ref, cc_ref, sc_ref,
                     q_ref, k_ref, v_ref):
    tm = cq_ref.shape[0]
    lane = lax.broadcasted_iota(I32, (tm, LANES), 1)
    cos = cc_ref[...]
    sin = sc_ref[...]

    def rope(t):
        return t * cos + _swap16(t, lane) * sin

    nq = _rms(cq_ref[...], gq_ref[...]).astype(BF16)
    q = jnp.dot(nq, wq_ref[...], preferred_element_type=F32)
    scale = (QK_NOPE + QK_ROPE) ** -0.5 * LOG2E
    for h in range(C_HEADS):
        q_ref[h] = (rope(q[:, h * LANES:(h + 1) * LANES]) * scale).astype(BF16)
    nkv = _rms(ckv_ref[...], gkv_ref[...]).astype(BF16)
    k = jnp.dot(nkv, wk_ref[...], preferred_element_type=F32)
    in_rope = (lane >= KR_LANE) & (lane < KR_LANE + QK_ROPE)
    kr = jnp.where(in_rope, rope(misc_ref[...]), 0.0)
    for h in range(C_HEADS):
        k_ref[:, h * LANES:(h + 1) * LANES] = (k[:, h * LANES:(h + 1) * LANES] + kr).astype(BF16)
    v_ref[...] = jnp.dot(nkv, wv_ref[...], preferred_element_type=F32).astype(BF16)


def _mla_prep(cq, ckv, misc, g_q, g_kv, wq_p, wk_p, wv_p, cc, sc):
    T = cq.shape[0]
    tm = min(512, T)
    row = lambda i: (i, 0)
    fix = lambda i: (0, 0)
    t128 = pl.BlockSpec((tm, LANES), row)
    return pl.pallas_call(
        _mla_prep_kernel,
        out_shape=[jax.ShapeDtypeStruct((C_HEADS, T, LANES), BF16),
                   jax.ShapeDtypeStruct((T, C_HEADS * LANES), BF16),
                   jax.ShapeDtypeStruct((T, C_W), BF16)],
        grid=(T // tm,),
        in_specs=[pl.BlockSpec((tm, Q_LORA), row), t128, t128,
                  pl.BlockSpec((1, Q_LORA), fix), pl.BlockSpec((1, KV_LORA), fix),
                  pl.BlockSpec((Q_LORA, C_HEADS * LANES), fix),
                  pl.BlockSpec((KV_LORA, C_HEADS * LANES), fix),
                  pl.BlockSpec((KV_LORA, C_W), fix), t128, t128],
        out_specs=[pl.BlockSpec((C_HEADS, tm, LANES), lambda i: (0, i, 0)),
                   pl.BlockSpec((tm, C_HEADS * LANES), row),
                   pl.BlockSpec((tm, C_W), row)],
        compiler_params=_params(("parallel",)),
        name="mla_prep",
    )(cq, ckv, misc, g_q, g_kv, wq_p, wk_p, wv_p, cc, sc)


def _outproj_kernel(x_ref, mod_ref, o1_ref, o2_ref, o3_ref, l1_ref, l2_ref, l3_ref,
                    ocmp_ref, oslc_ref, owin_ref, misc_ref, ge_ref, oc_ref,
                    ga_ref, gb_ref, gc_ref, w_ref, out_ref, scr):
    tm = x_ref.shape[0]

    def token_major(ref, d):
        if d == 1:
            return ref[...]
        for r in range(d):
            for p in range(A_W // LANES):
                scr[p, pl.ds(r, tm // d, stride=d), :] = ref[:, r * A_W + p * LANES:r * A_W + (p + 1) * LANES]
        return jnp.concatenate([scr[p] for p in range(A_W // LANES)], axis=1)

    o1, o2, o3 = [token_major(r, d) for r, d in zip((o1_ref, o2_ref, o3_ref), A_DILS)]
    l1, l2, l3 = [token_major(r, d) for r, d in zip((l1_ref, l2_ref, l3_ref), A_DILS)]
    mx = jnp.maximum(jnp.maximum(l1, l2), l3)
    e1, e2, e3 = jnp.exp(l1 - mx), jnp.exp(l2 - mx), jnp.exp(l3 - mx)
    oa = (e1 * o1 + e2 * o2 + e3 * o3) / (e1 + e2 + e3)

    sg = jax.nn.sigmoid(misc_ref[...])
    hi = sg.astype(BF16)
    lo = (sg - hi.astype(F32)).astype(BF16)

    def gate(br):
        return (jnp.dot(hi, ge_ref[br], preferred_element_type=F32)
                + jnp.dot(lo, ge_ref[br], preferred_element_type=F32))

    ob = gate(0) * ocmp_ref[...] + gate(1) * oslc_ref[...] + gate(2) * owin_ref[...]
    na = _rms(oa, ga_ref[...]).astype(BF16)
    nb = _rms(ob, gb_ref[...]).astype(BF16)
    nc = _rms(oc_ref[...], gc_ref[...]).astype(BF16)
    mixed = (jnp.dot(na, w_ref[0:A_W, :], preferred_element_type=F32)
             + jnp.dot(nb, w_ref[A_W:A_W + B_W, :], preferred_element_type=F32)
             + jnp.dot(nc, w_ref[A_W + B_W:, :], preferred_element_type=F32))
    out_ref[...] = x_ref[...] + mod_ref[0, 2:3, :] * mixed


def _gate_expand():
    ge = np.zeros((3, LANES, B_W), np.float32)
    for h in range(B_HEADS):
        for br in range(3):
            ge[br, h * 3 + br, h * HEAD:(h + 1) * HEAD] = 1.0
    return jnp.asarray(ge, BF16)


def _out_proj(x2, mods_l, oa_parts, lse_parts, ocmp, oslc, owin, misc, oc, g_a, g_b, g_c, w_out_b, S):
    T, D = x2.shape
    tm = min(256, S)
    spb = S // tm
    row = lambda i: (i, 0)
    fix = lambda i: (0, 0)
    t128 = pl.BlockSpec((tm, LANES), row)
    t256 = pl.BlockSpec((tm, 256), row)
    t512 = pl.BlockSpec((tm, 512), row)
    dil = [pl.BlockSpec((tm // d, d * A_W), row) for d in A_DILS]
    return pl.pallas_call(
        _outproj_kernel,
        out_shape=jax.ShapeDtypeStruct((T, D), F32),
        grid=(T // tm,),
        in_specs=[pl.BlockSpec((tm, D), row),
                  pl.BlockSpec((1, 6, D), lambda i: (i // spb, 0, 0)),
                  *dil, *dil, t512, t512, t512, t128,
                  pl.BlockSpec((3, LANES, B_W), lambda i: (0, 0, 0)),
                  t256,
                  pl.BlockSpec((1, A_W), fix), pl.BlockSpec((1, B_W), fix), pl.BlockSpec((1, C_W), fix),
                  pl.BlockSpec((D, D), fix)],
        out_specs=pl.BlockSpec((tm, D), row),
        scratch_shapes=[pltpu.VMEM((A_W // LANES, tm, LANES), F32)],
        compiler_params=_params(("parallel",)),
        name="out_proj",
    )(x2, mods_l, *oa_parts, *lse_parts, ocmp, oslc, owin, misc, _gate_expand(), oc, g_a, g_b, g_c, w_out_b)


def _pack_halves(v):
    bits = lax.bitcast_convert_type(v.astype(BF16).astype(F32), U32)
    half = v.shape[1] // 2
    return (bits[:, half:] & jnp.uint32(0xFFFF0000)) | (bits[:, :half] >> 16)


def _unpack_halves(w):
    lo = lax.bitcast_convert_type(w << 16, F32)
    hi = lax.bitcast_convert_type(w & jnp.uint32(0xFFFF0000), F32)
    return jnp.concatenate([lo, hi], axis=1)


def _put_planes(ref, w):
    for c in range(ref.shape[0]):
        ref[c] = w[:, c * LANES:(c + 1) * LANES]


def _get_planes(ref):
    return jnp.concatenate([ref[c] for c in range(ref.shape[0])], axis=1)


def _route_kernel(x_ref, mod_ref, g_ref, wh_ref, wl_ref, b_ref, hp_ref, e_ref, r_ref, gt_ref, cnt_ref, carry):
    tm = x_ref.shape[0]

    @pl.when(pl.program_id(0) == 0)
    def _():
        carry[...] = jnp.zeros(carry.shape, F32)

    h = _rms(x_ref[...], g_ref[...]) * (1.0 + mod_ref[0, 4:5, :]) + mod_ref[0, 3:4, :]
    hb = h.astype(BF16)
    hlo = (h - hb.astype(F32)).astype(BF16)
    _put_planes(hp_ref, _pack_halves(h))

    logits = (jnp.dot(hb, wh_ref[...], preferred_element_type=F32)
              + jnp.dot(hlo, wh_ref[...], preferred_element_type=F32)
              + jnp.dot(hb, wl_ref[...], preferred_element_type=F32)) + b_ref[...]
    lane = lax.broadcasted_iota(I32, (tm, LANES), 1)
    lane_f = lane.astype(F32)
    vals, idxs, picks = [], [], []
    lg = logits
    for _ in range(TOP_K):
        mx = lg.max(axis=-1, keepdims=True)
        idx = jnp.where(lg == mx, lane_f, float(LANES)).min(axis=-1, keepdims=True)
        pick = lane_f == idx
        vals.append(mx)
        idxs.append(idx.astype(I32))
        picks.append(pick)
        lg = jnp.where(pick, LOWEST, lg)
    es = [jnp.exp(v - vals[0]) for v in vals]
    den = es[0] + es[1] + es[2] + es[3]
    onehot = jnp.zeros((tm, LANES), F32)
    for pk in picks:
        onehot = jnp.where(pk, 1.0, onehot)
    tri = (lax.broadcasted_iota(I32, (tm, tm), 0) >= lax.broadcasted_iota(I32, (tm, tm), 1)).astype(BF16)
    cs = jnp.dot(tri, onehot.astype(BF16), preferred_element_type=F32)
    rank = carry[...] + cs - 1.0
    e_out = jnp.zeros((tm, LANES), I32)
    r_out = jnp.zeros((tm, LANES), I32)
    g_out = jnp.zeros((tm, LANES), F32)
    for k in range(TOP_K):
        rk = jnp.where(picks[k], rank, 0.0).sum(axis=-1, keepdims=True)
        e_out = jnp.where(lane == k, idxs[k], e_out)
        r_out = jnp.where(lane == k, rk.astype(I32), r_out)
        g_out = jnp.where(lane == k, es[k] / den, g_out)
    e_ref[...] = e_out
    r_ref[...] = r_out
    gt_ref[...] = g_out
    new = carry[...] + onehot.sum(axis=0, keepdims=True)
    carry[...] = new
    cnt_ref[...] = new


def _route(x2, mods_l, g2, wr_hi, wr_lo, br_p, S):
    T, D = x2.shape
    tm = min(256, S)
    spb = S // tm
    row = lambda i: (i, 0)
    fix = lambda i: (0, 0)
    t128 = pl.BlockSpec((tm, LANES), row)
    return pl.pallas_call(
        _route_kernel,
        out_shape=[jax.ShapeDtypeStruct((D // 2 // LANES, T, LANES), U32),
                   jax.ShapeDtypeStruct((T, LANES), I32), jax.ShapeDtypeStruct((T, LANES), I32),
                   jax.ShapeDtypeStruct((T, LANES), F32), jax.ShapeDtypeStruct((1, LANES), F32)],
        grid=(T // tm,),
        in_specs=[pl.BlockSpec((tm, D), row),
                  pl.BlockSpec((1, 6, D), lambda i: (i // spb, 0, 0)),
                  pl.BlockSpec((1, D), fix),
                  pl.BlockSpec((D, LANES), fix), pl.BlockSpec((D, LANES), fix), pl.BlockSpec((1, LANES), fix)],
        out_specs=[pl.BlockSpec((D // 2 // LANES, tm, LANES), lambda i: (0, i, 0)), t128, t128, t128,
                   pl.BlockSpec((1, LANES), fix)],
        scratch_shapes=[pltpu.VMEM((1, LANES), F32)],
        compiler_params=_params(("arbitrary",)),
        name="moe_route",
    )(x2, mods_l, g2, wr_hi, wr_lo, br_p)


SC_GATHER_ROWS = 128


def _sc_gather_rows(x, idx, name):
    n = idx.shape[0]
    W = x.shape[1]
    mesh = plsc.VectorSubcoreMesh(core_axis_name="core", subcore_axis_name="subcore")

    def body(x_hbm, i_hbm, o_hbm):
        def step(i_vmem, o_vmem):
            pltpu.sync_copy(x_hbm.at[i_vmem.at[0]], o_vmem)

        pltpu.emit_pipeline(
            step,
            grid=(n // SC_GATHER_ROWS,),
            in_specs=[pl.BlockSpec((1, SC_GATHER_ROWS), lambda i: (0, i))],
            out_specs=[pl.BlockSpec((SC_GATHER_ROWS, W), lambda i: (i, 0))],
            core_axis_name=("core", "subcore"),
            dimension_semantics=(pltpu.PARALLEL,),
        )(i_hbm, o_hbm)

    return pl.kernel(body, out_type=jax.ShapeDtypeStruct((n, W), x.dtype), mesh=mesh,
                     scratch_types=[], name=name)(x, idx.reshape(1, n))


def _sc_scatter_rows(x, idx, name):
    n, W = x.shape
    mesh = plsc.VectorSubcoreMesh(core_axis_name="core", subcore_axis_name="subcore")

    def body(x_hbm, i_hbm, o_hbm):
        def step(x_vmem, i_vmem):
            pltpu.sync_copy(x_vmem, o_hbm.at[i_vmem.at[0]])

        pltpu.emit_pipeline(
            step,
            grid=(n // SC_GATHER_ROWS,),
            in_specs=[pl.BlockSpec((SC_GATHER_ROWS, W), lambda i: (i, 0)),
                      pl.BlockSpec((1, SC_GATHER_ROWS), lambda i: (0, i))],
            out_specs=[],
            core_axis_name=("core", "subcore"),
            dimension_semantics=(pltpu.PARALLEL,),
        )(x_hbm, i_hbm)

    return pl.kernel(body, out_type=jax.ShapeDtypeStruct((n, W), x.dtype), mesh=mesh,
                     scratch_types=[], name=name)(x, idx.reshape(1, n))


SPLIT_COLS = 256


def _ffn_kernel(te_ref, nv_ref, chg_ref, x_ref, w1_ref, b1_ref, w2_ref, b2_ref, perm_ref, y_ref, w1b, w2b):
    j = pl.program_id(0)

    @pl.when(chg_ref[j] == 1)
    def _():
        half = SPLIT_COLS // 2
        for b in range(2 * D_FF // SPLIT_COLS):
            t = jnp.dot(w1_ref[:, b * SPLIT_COLS:(b + 1) * SPLIT_COLS].astype(BF16), perm_ref[...],
                        preferred_element_type=F32)
            w1b[:, b * half:(b + 1) * half] = t[:, :half].astype(BF16)
            w1b[:, D_FF + b * half:D_FF + (b + 1) * half] = t[:, half:].astype(BF16)
        w2b[...] = w2_ref[...].astype(BF16)

    @pl.when(j >= nv_ref[0])
    def _():
        y_ref[...] = jnp.zeros(y_ref.shape, U32)

    @pl.when(j < nv_ref[0])
    def _():
        xb = _unpack_halves(_get_planes(x_ref)).astype(BF16)
        u = jnp.dot(xb, w1b[...], preferred_element_type=F32) + b1_ref[0]
        glu = jnp.minimum(u[:, :D_FF], SWIGLU_LIMIT)
        lin = jnp.clip(u[:, D_FF:], -SWIGLU_LIMIT, SWIGLU_LIMIT)
        a = glu * jax.nn.sigmoid(SWIGLU_ALPHA * glu) * (lin + 1.0)
        y = jnp.dot(a.astype(BF16), w2b[...], preferred_element_type=F32) + b2_ref[0]
        _put_planes(y_ref, _pack_halves(y))


def _split_perm():
    p = np.zeros((SPLIT_COLS, SPLIT_COLS), np.float32)
    i = np.arange(SPLIT_COLS // 2)
    p[2 * i, i] = 1.0
    p[2 * i + 1, SPLIT_COLS // 2 + i] = 1.0
    return jnp.asarray(p, BF16)


def _expert_ffn(tile_e, n_valid, changed, xs, w1, b1p, w2, b2, tmf, layer):
    P, n_rows, _ = xs.shape
    D = 2 * P * LANES
    ntiles = n_rows // tmf
    return pl.pallas_call(
        _ffn_kernel,
        out_shape=jax.ShapeDtypeStruct((P, n_rows, LANES), U32),
        grid_spec=pltpu.PrefetchScalarGridSpec(
            num_scalar_prefetch=3,
            grid=(ntiles,),
            in_specs=[pl.BlockSpec((P, tmf, LANES), lambda j, te, nv, ch: (0, jnp.minimum(j, nv[0] - 1), 0)),
                      pl.BlockSpec((None, None, D, 2 * D_FF), lambda j, te, nv, ch: (layer, te[j], 0, 0)),
                      pl.BlockSpec((1, 1, 2 * D_FF), lambda j, te, nv, ch: (te[j], 0, 0)),
                      pl.BlockSpec((None, None, D_FF, D), lambda j, te, nv, ch: (layer, te[j], 0, 0)),
                      pl.BlockSpec((1, 1, D), lambda j, te, nv, ch: (te[j], 0, 0)),
                      pl.BlockSpec((SPLIT_COLS, SPLIT_COLS), lambda j, te, nv, ch: (0, 0))],
            out_specs=pl.BlockSpec((P, tmf, LANES), lambda j, te, nv, ch: (0, j, 0)),
            scratch_shapes=[pltpu.VMEM((D, 2 * D_FF), BF16), pltpu.VMEM((D_FF, D), BF16)]),
        compiler_params=_params(("arbitrary",)),
        name="moe_ffn",
    )(tile_e, n_valid, changed, xs, w1, b1p, w2, b2, _split_perm())


def _combine_kernel(yg_ref, gt_ref, x_ref, mod_ref, o_ref):
    g = gt_ref[...]
    y = None
    for k in range(TOP_K):
        yk = _unpack_halves(jnp.concatenate([yg_ref[c, k] for c in range(yg_ref.shape[0])], axis=1))
        y = g[:, k:k + 1] * yk if y is None else y + g[:, k:k + 1] * yk
    o_ref[...] = x_ref[...] + mod_ref[0, 5:6, :] * y


def _combine(yg, gates, x2, mods_l, S):
    T, D = x2.shape
    P = yg.shape[0]
    tc = min(512, S)
    spb = S // tc
    return pl.pallas_call(
        _combine_kernel,
        out_shape=jax.ShapeDtypeStruct((T, D), F32),
        grid=(T // tc,),
        in_specs=[pl.BlockSpec((P, TOP_K, tc, LANES), lambda i: (0, 0, i, 0)),
                  pl.BlockSpec((tc, LANES), lambda i: (i, 0)),
                  pl.BlockSpec((tc, D), lambda i: (i, 0)),
                  pl.BlockSpec((1, 6, D), lambda i: (i // spb, 0, 0))],
        out_specs=pl.BlockSpec((tc, D), lambda i: (i, 0)),
        compiler_params=_params(("parallel",)),
        name="moe_combine",
    )(yg, gates, x2, mods_l)


def _moe(x2, mods_l, g2, w_router, b_router, w1, b1p, w2, b2, S, layer):
    T, D = x2.shape
    tmf = 256
    wr = jnp.zeros((D, LANES), F32).at[:, :N_EXPERTS].set(w_router)
    wr_hi = wr.astype(BF16)
    wr_lo = (wr - wr_hi.astype(F32)).astype(BF16)
    br_p = jnp.full((1, LANES), NEG, F32).at[0, :N_EXPERTS].set(b_router)
    hp, e_idx, ranks, gates, cnt = _route(x2, mods_l, g2, wr_hi, wr_lo, br_p, S)

    counts = cnt[0, :N_EXPERTS].astype(I32)
    padded = ((counts + tmf - 1) // tmf) * tmf
    ends = jnp.cumsum(padded)
    offs = ends - padded
    dest = offs[e_idx[:, :TOP_K]] + ranks[:, :TOP_K]
    n_rows = TOP_K * T + N_EXPERTS * tmf
    ntiles = n_rows // tmf
    n_valid = (ends[-1] // tmf).astype(I32).reshape(1)
    tile_start = jnp.minimum(jnp.arange(ntiles, dtype=I32), n_valid[0] - 1) * tmf
    tile_e = jnp.sum(tile_start[:, None] >= ends[None, :], axis=1).astype(I32)
    n_pad = n_rows - TOP_K * T
    pad_cum = jnp.cumsum(padded - counts)
    p = jnp.arange(n_pad, dtype=I32)
    pe = jnp.sum(p[:, None] >= pad_cum[None, :], axis=1)
    pad_start = jnp.concatenate([offs + counts, ends[-1:]])
    pad_before = jnp.concatenate([jnp.zeros((1,), I32), pad_cum])
    pad_rows = pad_start[pe] + (p - pad_before[pe])
    perm = jnp.concatenate([dest.reshape(TOP_K * T), pad_rows.astype(I32)])
    vals = jnp.concatenate([jnp.arange(TOP_K * T, dtype=I32) // TOP_K, jnp.zeros((n_pad,), I32)])
    src = _sc_scatter_rows(jnp.broadcast_to(vals[:, None], (n_rows, LANES)), perm, "moe_invert_sc")[:, 0]
    changed = jnp.concatenate([jnp.ones((1,), I32), (tile_e[1:] != tile_e[:-1]).astype(I32)])
    P = hp.shape[0]
    plane = jnp.arange(P, dtype=I32)[:, None]
    src_rows = (src[None, :] + plane * T).reshape(P * n_rows)
    dest_rows = (dest.T.reshape(1, TOP_K * T) + plane * n_rows).reshape(P * TOP_K * T)

    xs = _sc_gather_rows(hp.reshape(P * T, LANES), src_rows, "moe_dispatch_sc").reshape(P, n_rows, LANES)
    ys = _expert_ffn(tile_e, n_valid, changed, xs, w1, b1p, w2, b2, tmf, layer)
    yg = _sc_gather_rows(ys.reshape(P * n_rows, LANES), dest_rows, "moe_collect_sc").reshape(P, TOP_K, T, LANES)
    return _combine(yg, gates, x2, mods_l, S)


def _final_kernel(x_ref, g_ref, o_ref):
    o_ref[...] = _rms(x_ref[...], g_ref[...])


def _final_norm(x2, g):
    T, D = x2.shape
    tm = min(512, T)
    return pl.pallas_call(
        _final_kernel,
        out_shape=jax.ShapeDtypeStruct((T, D), F32),
        grid=(T // tm,),
        in_specs=[pl.BlockSpec((tm, D), lambda i: (i, 0)), pl.BlockSpec((1, D), lambda i: (0, 0))],
        out_specs=pl.BlockSpec((tm, D), lambda i: (i, 0)),
        compiler_params=_params(("parallel",)),
        name="final_norm",
    )(x2, g.reshape(1, D))


def _rearrange_w_in(w):
    D = w.shape[0]
    main = w[:, :2048]
    gates = w[:, 2048:2072]
    cq = w[:, 2072:2328]
    ckv = w[:, 2328:2456]
    kr = w[:, 2456:2488]
    z = lambda n: jnp.zeros((D, n), w.dtype)
    misc = jnp.concatenate([gates, z(KR_LANE - 24), kr, z(LANES - KR_LANE - QK_ROPE)], axis=1)
    return jnp.concatenate([main, cq, ckv, misc], axis=1).astype(BF16)


def _mla_weights(w_q_up, w_kv_up):
    dq = QK_NOPE + QK_ROPE
    wq = jnp.zeros((Q_LORA, C_HEADS * LANES), F32)
    wk = jnp.zeros((KV_LORA, C_HEADS * LANES), F32)
    wv = []
    for h in range(C_HEADS):
        wq = wq.at[:, h * LANES:h * LANES + dq].set(w_q_up[:, h * dq:(h + 1) * dq])
        kv = w_kv_up[:, h * (QK_NOPE + HEAD):(h + 1) * (QK_NOPE + HEAD)]
        wk = wk.at[:, h * LANES:h * LANES + QK_NOPE].set(kv[:, :QK_NOPE])
        wv.append(kv[:, QK_NOPE:])
    return wq.astype(BF16), wk.astype(BF16), jnp.concatenate(wv, axis=1).astype(BF16)


def kernel(x, c, positions, w_ada, b_ada, g_norm1, g_norm2, w_in, nsa_pos_k, nsa_pos_v, nsa_w1_k, nsa_b1_k, nsa_w2_k, nsa_b2_k, nsa_w1_v, nsa_b1_v, nsa_w2_v, nsa_b2_v, mla_g_q, mla_g_kv, mla_w_q_up, mla_w_kv_up, g_out_a, g_out_b, g_out_c, w_out, w_router, b_router, w_exp1, b_exp1, w_exp2, b_exp2, g_final):
    B, S, D = x.shape
    T = B * S
    depth = w_ada.shape[0]
    x2 = x.reshape(T, D)

    cab, sab, cc, sc = _rope_tables(positions.reshape(T, 1))
    nch = S // CMP_STRIDE
    n_cmp = (S - CMP_LEN) // CMP_STRIDE + 1
    end_idx = np.minimum(np.arange(nch) * CMP_STRIDE + CMP_LEN - 1, S - 1)
    cos_e = cab.reshape(B, S, LANES)[:, end_idx]
    sin_e = sab.reshape(B, S, LANES)[:, end_idx]
    mods = _ada_mods(c, w_ada, b_ada).reshape(depth, B, 6, D)
    del n_cmp

    for l in range(depth):
        mods_l = mods[l]
        (qa, ka, va, qb, kcmp, vcmp, kslc, vslc, kwin, vwin, cq, ckv, misc) = _in_proj(
            x2, mods_l, g_norm1[l].reshape(1, D), _rearrange_w_in(w_in[l]), cab, sab, S)

        oa_parts, lse_parts = [], []
        for i, (window, d) in enumerate(A_PAIRS):
            o, lse = _dilated_attention(qa[i], ka[i], va[i], B, S, d, window // d)
            oa_parts.append(o)
            lse_parts.append(lse)

        kc = _compress(kcmp, nsa_pos_k[l], nsa_w1_k[l], nsa_b1_k[l], nsa_w2_k[l], nsa_b2_k[l],
                       cos_e, sin_e, B, S, True)
        vc = _compress(vcmp, nsa_pos_v[l], nsa_w1_v[l], nsa_b1_v[l], nsa_w2_v[l], nsa_b2_v[l],
                       cos_e, sin_e, B, S, False)
        kcc = jnp.concatenate([kc[:, 0, :, :HEAD], kc[:, 1, :, :HEAD]], axis=-1).astype(BF16)
        vcc = jnp.concatenate([vc[:, 0, :, :HEAD], vc[:, 1, :, :HEAD]], axis=-1).astype(BF16)
        ocmp, sel_t = _cmp_attention(qb, kcc, vcc, B, S)
        oslc = _select_attention(qb, kslc, vslc, sel_t, B, S)
        owin = _window_attention(qb, kwin, vwin, B, S)

        wq_p, wk_p, wv_p = _mla_weights(mla_w_q_up[l], mla_w_kv_up[l])
        qc, kcl, vcl = _mla_prep(cq, ckv, misc, mla_g_q[l].reshape(1, Q_LORA), mla_g_kv[l].reshape(1, KV_LORA),
                                 wq_p, wk_p, wv_p, cc, sc)
        oc = _latent_attention(qc, kcl, vcl, B, S)

        x2 = _out_proj(x2, mods_l, oa_parts, lse_parts, ocmp, oslc, owin, misc, oc,
                       g_out_a[l].reshape(1, A_W), g_out_b[l].reshape(1, B_W), g_out_c[l].reshape(1, C_W),
                       w_out[l].astype(BF16), S)

        b1r = b_exp1[l].reshape(N_EXPERTS, D_FF, 2)
        b1p = jnp.concatenate([b1r[..., 0], b1r[..., 1]], axis=-1).reshape(N_EXPERTS, 1, 2 * D_FF)
        x2 = _moe(x2, mods_l, g_norm2[l].reshape(1, D), w_router[l], b_router[l],
                  w_exp1, b1p, w_exp2, b_exp2[l].reshape(N_EXPERTS, 1, D), S, l)

    return _final_norm(x2, g_final).reshape(B, S, D)
```

```python
import functools

import numpy as np
import jax
import jax.numpy as jnp
from jax import lax
from jax.experimental import pallas as pl
from jax.experimental.pallas import tpu as pltpu
from jax.experimental.pallas import tpu_sc as plsc

F32 = jnp.float32
BF16 = jnp.bfloat16
I32 = jnp.int32
U32 = jnp.uint32

LANES = 128
HEAD = 64
NEG = -1e30
LOWEST = -3.0e38
LOG2E = 1.4426950408889634
LN2 = 0.6931471805599453
NORM_EPS = 1e-6
ROPE_THETA = 10000.0

A_HEADS, B_HEADS, B_GROUPS, C_HEADS = 4, 8, 2, 4
A_PAIRS = ((128, 1), (512, 4), (2048, 16))
A_W, B_W, C_W = 256, 512, 256
CMP_LEN, CMP_STRIDE, CMP_HIDDEN = 32, 16, 256
SLC_BLOCK, N_SELECT, WINDOW = 64, 16, 512
FORCE_BONUS = 1e4
Q_LORA, KV_LORA, QK_NOPE, QK_ROPE = 256, 128, 64, 32
N_EXPERTS, TOP_K, D_FF = 32, 4, 1024
SWIGLU_ALPHA, SWIGLU_LIMIT = 1.702, 7.0
IN_COLS = 2560
KR_LANE = 64

VMEM_LIMIT = 56 * 1024 * 1024


def _params(sem, vmem=VMEM_LIMIT):
    return pltpu.CompilerParams(dimension_semantics=sem, vmem_limit_bytes=vmem)


def _rms(x, g):
    return x * lax.rsqrt(jnp.mean(x * x, axis=-1, keepdims=True) + NORM_EPS) * g


def _rope_rows():
    inv64 = ROPE_THETA ** (-np.arange(32, dtype=np.float32) * (2.0 / 64))
    inv32 = ROPE_THETA ** (-np.arange(16, dtype=np.float32) * (2.0 / 32))
    inv_ab = np.tile(np.concatenate([inv64, inv64]), 2)
    sgn_ab = np.tile(np.concatenate([-np.ones(32), np.ones(32)]), 2)
    inv_c = np.zeros(LANES, np.float32)
    sgn_c = np.zeros(LANES, np.float32)
    inv_c[KR_LANE:KR_LANE + 16] = inv32
    inv_c[KR_LANE + 16:KR_LANE + 32] = inv32
    sgn_c[KR_LANE:KR_LANE + 16] = -1.0
    sgn_c[KR_LANE + 16:KR_LANE + 32] = 1.0
    rows = np.stack([inv_ab, sgn_ab, inv_c, sgn_c]).astype(np.float32)
    return jnp.asarray(rows)


def _rope_tab_kernel(pos_ref, rows_ref, cab_ref, sab_ref, cc_ref, sc_ref):
    pos = pos_ref[...].astype(F32)
    ang = pos * rows_ref[0:1, :]
    cab_ref[...] = jnp.cos(ang)
    sab_ref[...] = jnp.sin(ang) * rows_ref[1:2, :]
    ang = pos * rows_ref[2:3, :]
    cc_ref[...] = jnp.cos(ang)
    sc_ref[...] = jnp.sin(ang) * rows_ref[3:4, :]


def _rope_tables(pos_col):
    R = pos_col.shape[0]
    tm = min(R, 1024)
    spec = pl.BlockSpec((tm, LANES), lambda i: (i, 0))
    return pl.pallas_call(
        _rope_tab_kernel,
        out_shape=[jax.ShapeDtypeStruct((R, LANES), F32)] * 4,
        grid=(R // tm,),
        in_specs=[pl.BlockSpec((tm, 1), lambda i: (i, 0)), pl.BlockSpec((4, LANES), lambda i: (0, 0))],
        out_specs=[spec] * 4,
        compiler_params=_params(("parallel",)),
        name="rope_tables",
    )(pos_col, _rope_rows())


def _swap32(t, lane):
    return jnp.where((lane & 32) == 0, pltpu.roll(t, 96, 1), pltpu.roll(t, 32, 1))


def _swap16(t, lane):
    return jnp.where(lane < KR_LANE + 16, pltpu.roll(t, 112, 1), pltpu.roll(t, 16, 1))


def _mods_kernel(c_ref, w_ref, b_ref, o_ref):
    c = c_ref[...]
    ca = (c * jax.nn.sigmoid(c)).astype(BF16)
    o_ref[0] = jnp.dot(ca, w_ref[0].astype(BF16), preferred_element_type=F32) + b_ref[0]


def _ada_mods(c, w_ada, b_ada):
    L, D, N = w_ada.shape
    B = c.shape[0]
    tn = 1536
    return pl.pallas_call(
        _mods_kernel,
        out_shape=jax.ShapeDtypeStruct((L, B, N), F32),
        grid=(L, N // tn),
        in_specs=[pl.BlockSpec((B, D), lambda l, j: (0, 0)),
                  pl.BlockSpec((1, D, tn), lambda l, j: (l, 0, j)),
                  pl.BlockSpec((1, 1, tn), lambda l, j: (l, 0, j))],
        out_specs=pl.BlockSpec((1, B, tn), lambda l, j: (l, 0, j)),
        compiler_params=_params(("parallel", "parallel")),
        name="ada_mods",
    )(c, w_ada, b_ada.reshape(L, 1, N))


A_DILS = tuple(d for _, d in A_PAIRS)


def _inproj_kernel(x_ref, mod_ref, g_ref, w_ref, cab_ref, sab_ref, *refs):
    nd = len(A_DILS)
    qa_refs, ka_refs, va_refs = refs[0:nd], refs[nd:2 * nd], refs[2 * nd:3 * nd]
    (qb_ref, kcmp_ref, vcmp_ref, kslc_ref, vslc_ref, kwin_ref, vwin_ref, cq_ref, ckv_ref, misc_ref,
     scr) = refs[3 * nd:]
    tm = x_ref.shape[0]

    def spread(val, store):
        W = val.shape[1]
        for p in range(W // LANES):
            scr[p] = val[:, p * LANES:(p + 1) * LANES]
        for i, d in enumerate(A_DILS):
            if d == 1:
                store(i, 0, val.astype(BF16))
            else:
                for r in range(d):
                    planes = [scr[p, pl.ds(r, tm // d, stride=d), :] for p in range(W // LANES)]
                    piece = planes[0] if len(planes) == 1 else jnp.concatenate(planes, axis=1)
                    store(i, r, piece.astype(BF16))
    x = x_ref[...]
    h = (_rms(x, g_ref[...]) * (1.0 + mod_ref[0, 1:2, :]) + mod_ref[0, 0:1, :]).astype(BF16)
    cos = cab_ref[...]
    sin = sab_ref[...]
    lane = lax.broadcasted_iota(I32, (tm, LANES), 1)
    lo64 = lane < HEAD

    def proj(c0, n):
        return jnp.dot(h, w_ref[:, c0:c0 + n], preferred_element_type=F32)

    def rope(t):
        return t * cos + _swap32(t, lane) * sin

    def pad_q(h0, t, dst):
        t = rope(t) * (HEAD ** -0.5 * LOG2E)
        tr = pltpu.roll(t, HEAD, 1)
        return [jnp.where(lo64 if dst[i] == 0 else ~lo64, t if dst[i] == i else tr, 0.0) for i in range(2)]

    def put_a(arefs, width, lane0=0):
        def store(i, r, piece):
            arefs[i][:, r * A_W + lane0:r * A_W + lane0 + width] = piece
        return store

    y = proj(0, 256)
    for j in range(2):
        for i, qh in enumerate(pad_q(2 * j, y[:, 128 * j:128 * (j + 1)], (0, 1))):
            def store(di, r, piece, hd=2 * j + i):
                qa_refs[di][hd, :, r * LANES:(r + 1) * LANES] = piece

            spread(qh, store)
    y = proj(256, 256)
    spread(jnp.concatenate([rope(y[:, :128]), rope(y[:, 128:])], axis=1), put_a(ka_refs, A_W))
    spread(proj(512, 256), put_a(va_refs, A_W))
    for j in range(2):
        y = proj(768 + 256 * j, 256)
        for jj in range(2):
            for i, qh in enumerate(pad_q(4 * j + 2 * jj, y[:, 128 * jj:128 * (jj + 1)], (j, j))):
                qb_ref[4 * j + 2 * jj + i] = qh.astype(BF16)
    y = proj(1280, 256)
    kcmp_ref[...] = y[:, :128]
    vcmp_ref[...] = y[:, 128:]
    y = proj(1536, 256)
    kslc_ref[...] = rope(y[:, :128]).astype(BF16)
    vslc_ref[...] = y[:, 128:].astype(BF16)
    y = proj(1792, 256)
    kwin_ref[...] = rope(y[:, :128]).astype(BF16)
    vwin_ref[...] = y[:, 128:].astype(BF16)
    cq_ref[...] = proj(2048, 256)
    y = proj(2304, 256)
    ckv_ref[...] = y[:, :128]
    misc_ref[...] = y[:, 128:]


def _in_proj(x2, mods_l, g1, w_in_r, cab, sab, S):
    T, D = x2.shape
    tm = min(512, S)
    spb = S // tm
    row = lambda i: (i, 0)
    shp = jax.ShapeDtypeStruct
    outs = ([shp((A_HEADS, T // d, d * LANES), BF16) for d in A_DILS]
            + [shp((T // d, d * A_W), BF16) for d in A_DILS] * 2
            + [shp((B_HEADS, T, LANES), BF16),
               shp((T, LANES), F32), shp((T, LANES), F32), shp((T, LANES), BF16), shp((T, LANES), BF16),
               shp((T, LANES), BF16), shp((T, LANES), BF16),
               shp((T, Q_LORA), F32), shp((T, LANES), F32), shp((T, LANES), F32)])
    t128 = pl.BlockSpec((tm, LANES), row)
    t256 = pl.BlockSpec((tm, 256), row)
    out_specs = ([pl.BlockSpec((A_HEADS, tm // d, d * LANES), lambda i: (0, i, 0)) for d in A_DILS]
                 + [pl.BlockSpec((tm // d, d * A_W), row) for d in A_DILS] * 2
                 + [pl.BlockSpec((B_HEADS, tm, LANES), lambda i: (0, i, 0)),
                    t128, t128, t128, t128, t128, t128, t256, t128, t128])
    res = pl.pallas_call(
        _inproj_kernel,
        out_shape=outs,
        grid=(T // tm,),
        in_specs=[pl.BlockSpec((tm, D), row),
                  pl.BlockSpec((1, 6, D), lambda i: (i // spb, 0, 0)),
                  pl.BlockSpec((1, D), lambda i: (0, 0)),
                  pl.BlockSpec((D, IN_COLS), lambda i: (0, 0)),
                  t128, t128],
        out_specs=out_specs,
        scratch_shapes=[pltpu.VMEM((A_W // LANES, tm, LANES), F32)],
        compiler_params=_params(("parallel",)),
        name="in_proj",
    )(x2, mods_l, g1, w_in_r, cab, sab)
    nd = len(A_DILS)
    return (res[0:nd], res[nd:2 * nd], res[2 * nd:3 * nd]) + tuple(res[3 * nd:])


_NT = (((1,), (1,)), ((), ()))
_TN = (((0,), (0,)), ((), ()))


KEY_SUB = 128
QUERY_SUB = 256


def _online_chunk(k, v, q, keep, bias, m, l, acc):
    s = lax.dot_general(k, q, _NT, preferred_element_type=F32)
    if bias is not None:
        s = s + bias
    if keep is not None:
        s = jnp.where(keep, s, NEG)
    m_next = jnp.maximum(m, s.max(axis=0, keepdims=True))
    alpha = jnp.exp2(m - m_next)
    p = jnp.exp2(s - m_next)
    l = alpha * l + p.sum(axis=0, keepdims=True)
    acc = acc * alpha + lax.dot_general(v, p.astype(BF16), _TN, preferred_element_type=F32)
    return m_next, l, acc


def _mask_heads(s, masks, tq, fill):
    return jnp.concatenate([jnp.where(masks[h], s[:, h * tq:(h + 1) * tq], fill)
                            for h in range(len(masks))], axis=1)


def _emit_heads(o_ref, acc_t, halves, tq):
    nh = len(halves)
    lo64 = lax.broadcasted_iota(I32, (tq, LANES), 1) < HEAD
    for c in range(nh // 2):
        parts = []
        for h in (2 * c, 2 * c + 1):
            o = acc_t[:, h * tq:(h + 1) * tq].T
            if halves[h] != h % 2:
                o = pltpu.roll(o, HEAD, 1)
            parts.append(o)
        o_ref[:, c * LANES:(c + 1) * LANES] = jnp.where(lo64, parts[0], parts[1])


def _band_kernel(*refs, nblk, tq, tk, max_dist, stacks, halves, with_lse, qi_axis):
    q_ref = refs[0]
    k_refs = refs[1:1 + nblk]
    v_refs = refs[1 + nblk:1 + 2 * nblk]
    o_ref = refs[1 + 2 * nblk]
    lse_ref = refs[2 + 2 * nblk] if with_lse else None
    nh = len(halves)
    qi = pl.program_id(qi_axis)
    ks = min(KEY_SUB, tk)
    cq = min(QUERY_SUB, tq)
    rc = lax.broadcasted_iota(I32, (ks, cq), 0) - lax.broadcasted_iota(I32, (ks, cq), 1)
    head_blk = {}
    for (h0, n, blk) in stacks:
        for h in range(h0, h0 + n):
            head_blk[h] = blk
    outs, lses = [], []
    for h in range(nh):
        lanes = slice(head_blk[h] * LANES, (head_blk[h] + 1) * LANES)
        for c in range(tq // cq):
            q = q_ref[h, c * cq:(c + 1) * cq, :]
            m = jnp.full((1, cq), NEG, F32)
            l = jnp.zeros((1, cq), F32)
            acc = jnp.zeros((LANES, cq), F32)
            for i in range(nblk):
                kt = qi * (tq // tk) - (nblk - tq // tk) + i
                for kk in range(tk // ks):
                    base = qi * tq + c * cq - (kt * tk + kk * ks)
                    base = jnp.where(kt >= 0, base, -(1 << 20))
                    keep = (rc <= base) & (rc >= base - max_dist)
                    m, l, acc = _online_chunk(k_refs[i][kk * ks:(kk + 1) * ks, lanes],
                                              v_refs[i][kk * ks:(kk + 1) * ks, lanes], q, keep, None, m, l, acc)
            outs.append(acc / l)
            lses.append(jnp.broadcast_to(m * LN2 + jnp.log(l), (LANES, cq)))
    _emit_heads(o_ref, jnp.concatenate(outs, axis=1), halves, tq)
    if with_lse:
        _emit_heads(lse_ref, jnp.concatenate(lses, axis=1), tuple(h % 2 for h in range(nh)), tq)


def _dilated_attention(qa, ka, va, B, S, d, max_dist):
    L = S // d
    tq = min(256, L)
    tk = 128
    nprev = -(-max_dist // tk)
    nblk = nprev + tq // tk
    q_v = qa.reshape(A_HEADS, B, L, d * LANES)
    k_v = ka.reshape(B, L, d * A_W)
    v_v = va.reshape(B, L, d * A_W)
    kv_specs = []
    for i in range(nblk):
        kv_specs.append(pl.BlockSpec(
            (None, tk, A_W),
            functools.partial(lambda b, r, qi, i: (b, jnp.maximum(qi * (tq // tk) - nprev + i, 0), r), i=i)))
    out_spec = pl.BlockSpec((None, tq, A_W), lambda b, r, qi: (b, qi, r))
    kern = functools.partial(_band_kernel, nblk=nblk, tq=tq, tk=tk, max_dist=max_dist,
                             stacks=((0, 2, 0), (2, 2, 1)), halves=(0, 1, 0, 1), with_lse=True, qi_axis=2)
    o, lse = pl.pallas_call(
        kern,
        out_shape=[jax.ShapeDtypeStruct((B, L, d * A_W), F32)] * 2,
        grid=(B, d, L // tq),
        in_specs=[pl.BlockSpec((A_HEADS, None, tq, LANES), lambda b, r, qi: (0, b, qi, r))]
                 + kv_specs + kv_specs,
        out_specs=[out_spec, out_spec],
        compiler_params=_params(("parallel", "parallel", "parallel")),
        name=f"dilated_attn_d{d}",
    )(q_v, *([k_v] * nblk), *([v_v] * nblk))
    return o.reshape(B * L, d * A_W), lse.reshape(B * L, d * A_W)


def _window_attention(qb, kwin, vwin, B, S):
    tq = min(256, S)
    tk = tq
    max_dist = WINDOW - 1
    nprev = -(-max_dist // tk)
    nblk = nprev + 1
    q_v = qb.reshape(B_HEADS, B, S, LANES)
    k_v = kwin.reshape(B, S, LANES)
    v_v = vwin.reshape(B, S, LANES)
    kv_specs = [pl.BlockSpec((None, tk, LANES),
                             functools.partial(lambda b, qi, i: (b, jnp.maximum(qi - nprev + i, 0), 0), i=i))
                for i in range(nblk)]
    kern = functools.partial(_band_kernel, nblk=nblk, tq=tq, tk=tk, max_dist=max_dist,
                             stacks=((0, B_HEADS, 0),), halves=(0, 0, 0, 0, 1, 1, 1, 1),
                             with_lse=False, qi_axis=1)
    o = pl.pallas_call(
        kern,
        out_shape=jax.ShapeDtypeStruct((B, S, B_W), F32),
        grid=(B, S // tq),
        in_specs=[pl.BlockSpec((B_HEADS, None, tq, LANES), lambda b, qi: (0, b, qi, 0))]
                 + kv_specs + kv_specs,
        out_specs=pl.BlockSpec((None, tq, B_W), lambda b, qi: (b, qi, 0)),
        compiler_params=_params(("parallel", "parallel")),
        name="window_attn",
    )(q_v, *([k_v] * nblk), *([v_v] * nblk))
    return o.reshape(B * S, B_W)


def _flash_kernel(qi_ref, kj_ref, first_ref, last_ref, *refs, tq, tk, qk_groups, pv_groups, halves, select):
    if select:
        q_ref, k_ref, v_ref, sel_ref, o_ref, m_sc, l_sc, acc_sc = refs
    else:
        q_ref, k_ref, v_ref, o_ref, m_sc, l_sc, acc_sc = refs
    nh = len(halves)
    step = pl.program_id(1)
    qi = qi_ref[step]
    kj = kj_ref[step]

    @pl.when(first_ref[step] == 1)
    def _():
        m_sc[...] = jnp.full(m_sc.shape, NEG, F32)
        l_sc[...] = jnp.zeros(l_sc.shape, F32)
        acc_sc[...] = jnp.zeros(acc_sc.shape, F32)

    ks = min(KEY_SUB, tk)
    cq = min(QUERY_SUB, tq)
    rc = lax.broadcasted_iota(I32, (ks, cq), 0) - lax.broadcasted_iota(I32, (ks, cq), 1)
    head_kb, head_vb = {}, {}
    for (h0, n, kb) in qk_groups:
        for h in range(h0, h0 + n):
            head_kb[h] = kb
    for (h0, n, vb) in pv_groups:
        for h in range(h0, h0 + n):
            head_vb[h] = vb
    hpg = nh // B_GROUPS
    spb = ks // SLC_BLOCK if select else 1

    def run(on_diagonal):
        for h in range(nh):
            klanes = slice(head_kb[h] * LANES, (head_kb[h] + 1) * LANES)
            vlanes = slice(head_vb[h] * LANES, (head_vb[h] + 1) * LANES)
            for c in range(tq // cq):
                cols = slice(h * tq + c * cq, h * tq + (c + 1) * cq)
                q = q_ref[h, c * cq:(c + 1) * cq, :]
                m, l, acc = m_sc[:, cols], l_sc[:, cols], acc_sc[:, cols]
                for kk in range(tk // ks):
                    keep = None
                    if on_diagonal:
                        keep = rc <= (qi * tq + c * cq) - (kj * tk + kk * ks)
                    bias = None
                    if select:
                        rows = [jnp.broadcast_to(
                            (sel_ref[h // hpg, pl.ds((kj * (tk // ks) + kk) * spb + j, 1), c * cq:(c + 1) * cq] - 1.0)
                            * (-NEG), (SLC_BLOCK, cq)) for j in range(spb)]
                        bias = rows[0] if spb == 1 else jnp.concatenate(rows, axis=0)
                    m, l, acc = _online_chunk(k_ref[kk * ks:(kk + 1) * ks, klanes],
                                              v_ref[kk * ks:(kk + 1) * ks, vlanes], q, keep, bias, m, l, acc)
                m_sc[:, cols] = m
                l_sc[:, cols] = l
                acc_sc[:, cols] = acc

    crosses = kj * tk + (tk - 1) > qi * tq

    @pl.when(crosses)
    def _():
        run(True)

    @pl.when(jnp.logical_not(crosses))
    def _():
        run(False)

    @pl.when(last_ref[step] == 1)
    def _():
        _emit_heads(o_ref, acc_sc[...] / l_sc[...], halves, tq)


def _causal_schedule(S, tq, tk):
    qi, kj, first, last = [], [], [], []
    for i in range(S // tq):
        nk = (i * tq + tq - 1) // tk + 1
        for j in range(nk):
            qi.append(i)
            kj.append(j)
            first.append(1 if j == 0 else 0)
            last.append(1 if j == nk - 1 else 0)
    return [jnp.asarray(np.asarray(a, np.int32)) for a in (qi, kj, first, last)]


def _select_attention(qb, kslc, vslc, sel_t, B, S):
    tq = min(512, S)
    tk = min(256, S)
    n_slc = S // SLC_BLOCK
    tabs = _causal_schedule(S, tq, tk)
    nsteps = tabs[0].shape[0]
    kern = functools.partial(_flash_kernel, tq=tq, tk=tk, qk_groups=((0, B_HEADS, 0),),
                             pv_groups=((0, B_HEADS, 0),), halves=(0, 0, 0, 0, 1, 1, 1, 1), select=True)
    o = pl.pallas_call(
        kern,
        out_shape=jax.ShapeDtypeStruct((B, S, B_W), F32),
        grid_spec=pltpu.PrefetchScalarGridSpec(
            num_scalar_prefetch=4,
            grid=(B, nsteps),
            in_specs=[pl.BlockSpec((B_HEADS, None, tq, LANES), lambda b, s, qi, kj, f, l: (0, b, qi[s], 0)),
                      pl.BlockSpec((None, tk, LANES), lambda b, s, qi, kj, f, l: (b, kj[s], 0)),
                      pl.BlockSpec((None, tk, LANES), lambda b, s, qi, kj, f, l: (b, kj[s], 0)),
                      pl.BlockSpec((None, B_GROUPS, n_slc, tq), lambda b, s, qi, kj, f, l: (b, 0, 0, qi[s]))],
            out_specs=pl.BlockSpec((None, tq, B_W), lambda b, s, qi, kj, f, l: (b, qi[s], 0)),
            scratch_shapes=[pltpu.VMEM((1, B_HEADS * tq), F32), pltpu.VMEM((1, B_HEADS * tq), F32),
                            pltpu.VMEM((LANES, B_HEADS * tq), F32)]),
        compiler_params=_params(("parallel", "arbitrary")),
        name="select_attn",
    )(*tabs, qb.reshape(B_HEADS, B, S, LANES), kslc.reshape(B, S, LANES), vslc.reshape(B, S, LANES), sel_t)
    return o.reshape(B * S, B_W)


def _latent_attention(qc, kc, vc, B, S):
    tq = min(512, S)
    tk = min(512, S)
    tabs = _causal_schedule(S, tq, tk)
    nsteps = tabs[0].shape[0]
    kern = functools.partial(_flash_kernel, tq=tq, tk=tk,
                             qk_groups=tuple((h, 1, h) for h in range(C_HEADS)),
                             pv_groups=((0, 2, 0), (2, 2, 1)), halves=(0, 1, 0, 1), select=False)
    o = pl.pallas_call(
        kern,
        out_shape=jax.ShapeDtypeStruct((B, S, C_W), F32),
        grid_spec=pltpu.PrefetchScalarGridSpec(
            num_scalar_prefetch=4,
            grid=(B, nsteps),
            in_specs=[pl.BlockSpec((C_HEADS, None, tq, LANES), lambda b, s, qi, kj, f, l: (0, b, qi[s], 0)),
                      pl.BlockSpec((None, tk, C_HEADS * LANES), lambda b, s, qi, kj, f, l: (b, kj[s], 0)),
                      pl.BlockSpec((None, tk, C_W), lambda b, s, qi, kj, f, l: (b, kj[s], 0))],
            out_specs=pl.BlockSpec((None, tq, C_W), lambda b, s, qi, kj, f, l: (b, qi[s], 0)),
            scratch_shapes=[pltpu.VMEM((1, C_HEADS * tq), F32), pltpu.VMEM((1, C_HEADS * tq), F32),
                            pltpu.VMEM((LANES, C_HEADS * tq), F32)]),
        compiler_params=_params(("parallel", "arbitrary")),
        name="latent_attn",
    )(*tabs, qc.reshape(C_HEADS, B, S, LANES), kc.reshape(B, S, C_HEADS * LANES), vc.reshape(B, S, C_W))
    return o.reshape(B * S, C_W)


def _gelu_tanh(x):
    return 0.5 * x * (1.0 + jnp.tanh(0.7978845608028654 * (x + 0.044715 * x * x * x)))


def _compress_kernel(u_ref, pos_ref, w1_ref, b1_ref, w2_ref, b2_ref, cos_ref, sin_ref, o_ref, *, rotate):
    n = u_ref.shape[0]
    half = CMP_STRIDE * HEAD
    u = u_ref[...]
    ua = (u + pos_ref[:, :half]).astype(BF16)
    ub = (u + pos_ref[:, half:]).astype(BF16)
    pa = jnp.dot(ua, w1_ref[:half, :].astype(BF16), preferred_element_type=F32)
    pb = jnp.dot(ub, w1_ref[half:, :].astype(BF16), preferred_element_type=F32)
    hid = _gelu_tanh(pa + pltpu.roll(pb, n - 1, 0) + b1_ref[...])
    y = jnp.dot(hid.astype(BF16), w2_ref[...].astype(BF16), preferred_element_type=F32) + b2_ref[...]
    if rotate:
        lane = lax.broadcasted_iota(I32, y.shape, 1)
        y = y * cos_ref[...] + _swap32(y, lane) * sin_ref[...]
    o_ref[...] = y


def _compress(t, pos_emb, w1, b1, w2, b2, cos_e, sin_e, B, S, rotate):
    nch = S // CMP_STRIDE
    u = t.reshape(B, S, B_GROUPS, HEAD).transpose(0, 2, 1, 3).reshape(B, B_GROUPS, nch, CMP_STRIDE * HEAD)
    w2d = jnp.concatenate([w2, w2], axis=1)
    b2d = jnp.concatenate([b2, b2]).reshape(1, LANES)
    kern = functools.partial(_compress_kernel, rotate=rotate)
    tab = pl.BlockSpec((None, nch, LANES), lambda b, g: (b, 0, 0))
    return pl.pallas_call(
        kern,
        out_shape=jax.ShapeDtypeStruct((B, B_GROUPS, nch, LANES), F32),
        grid=(B, B_GROUPS),
        in_specs=[pl.BlockSpec((None, None, nch, CMP_STRIDE * HEAD), lambda b, g: (b, g, 0, 0)),
                  pl.BlockSpec((1, CMP_LEN * HEAD), lambda b, g: (0, 0)),
                  pl.BlockSpec((CMP_LEN * HEAD, CMP_HIDDEN), lambda b, g: (0, 0)),
                  pl.BlockSpec((1, CMP_HIDDEN), lambda b, g: (0, 0)),
                  pl.BlockSpec((CMP_HIDDEN, LANES), lambda b, g: (0, 0)),
                  pl.BlockSpec((1, LANES), lambda b, g: (0, 0)),
                  tab, tab],
        out_specs=pl.BlockSpec((None, None, nch, LANES), lambda b, g: (b, g, 0, 0)),
        compiler_params=_params(("parallel", "parallel")),
        name="nsa_compress",
    )(u, pos_emb.reshape(1, CMP_LEN * HEAD), w1, b1.reshape(1, CMP_HIDDEN), w2d, b2d, cos_e, sin_e)


def _cmp_attn_kernel(q_ref, kc_ref, vc_ref, ov_ref, o_ref, sel_ref, *, tq, n_cmp, n_slc):
    qi = pl.program_id(1)
    nk = kc_ref.shape[0]
    q = q_ref[...].reshape(B_HEADS * tq, LANES)
    s = lax.dot_general(kc_ref[...], q, _NT, preferred_element_type=F32)
    t_idx = qi * tq + lax.broadcasted_iota(I32, (nk, tq), 1)
    c_idx = lax.broadcasted_iota(I32, (nk, tq), 0)
    mask = ((c_idx * CMP_STRIDE + (CMP_LEN - 1)) <= t_idx) & (c_idx < n_cmp)
    masks = [mask] * B_HEADS
    s = _mask_heads(s, masks, tq, NEG)
    m = s.max(axis=0, keepdims=True)
    e = _mask_heads(jnp.exp2(s - m), masks, tq, 0.0)
    den = e.sum(axis=0, keepdims=True)
    p = e / jnp.where(den > 0, den, 1.0)
    o_t = lax.dot_general(vc_ref[...], p.astype(BF16), _TN, preferred_element_type=F32)
    hpg = B_HEADS // B_GROUPS
    _emit_heads(o_ref, o_t, tuple(h // hpg for h in range(B_HEADS)), tq)

    j_idx = lax.broadcasted_iota(I32, (n_slc, tq), 0)
    t_col = qi * tq + lax.broadcasted_iota(I32, (n_slc, tq), 1)
    cur = t_col // SLC_BLOCK
    forced = (j_idx == 0) | (j_idx == cur) | (j_idx == cur - 1)
    valid = j_idx * SLC_BLOCK <= t_col
    for g in range(B_GROUPS):
        ps = p[:, g * hpg * tq:(g * hpg + 1) * tq]
        for h in range(g * hpg + 1, (g + 1) * hpg):
            ps = ps + p[:, h * tq:(h + 1) * tq]
        hi = ps.astype(BF16)
        lo = (ps - hi.astype(F32)).astype(BF16)
        imp = (jnp.dot(ov_ref[...], hi, preferred_element_type=F32)
               + jnp.dot(ov_ref[...], lo, preferred_element_type=F32))
        score = jnp.where(forced, imp + FORCE_BONUS, jnp.where(valid, imp, -1.0))
        sel = jnp.zeros((n_slc, tq), F32)
        j_f = j_idx.astype(F32)
        for _ in range(min(N_SELECT, n_slc)):
            mx = score.max(axis=0, keepdims=True)
            idx = jnp.where(score == mx, j_f, float(n_slc)).min(axis=0, keepdims=True)
            pick = j_f == idx
            sel = jnp.where(pick, 1.0, sel)
            score = jnp.where(pick, LOWEST, score)
        sel_ref[g] = sel


def _cmp_attention(qb, kcc, vcc, B, S):
    tq = min(256, S)
    nk = S // CMP_STRIDE
    n_cmp = (S - CMP_LEN) // CMP_STRIDE + 1
    n_slc = S // SLC_BLOCK
    cmp_tok = np.arange(n_cmp)[:, None] * CMP_STRIDE + np.arange(CMP_LEN)[None, :]
    overlap = ((cmp_tok[:, :, None] // SLC_BLOCK) == np.arange(n_slc)[None, None, :]).mean(axis=1)
    ov_t = np.zeros((n_slc, nk), np.float32)
    ov_t[:, :n_cmp] = overlap.T
    kern = functools.partial(_cmp_attn_kernel, tq=tq, n_cmp=n_cmp, n_slc=n_slc)
    o, sel = pl.pallas_call(
        kern,
        out_shape=[jax.ShapeDtypeStruct((B, S, B_W), F32),
                   jax.ShapeDtypeStruct((B, B_GROUPS, n_slc, S), F32)],
        grid=(B, S // tq),
        in_specs=[pl.BlockSpec((B_HEADS, None, tq, LANES), lambda b, qi: (0, b, qi, 0)),
                  pl.BlockSpec((None, nk, LANES), lambda b, qi: (b, 0, 0)),
                  pl.BlockSpec((None, nk, LANES), lambda b, qi: (b, 0, 0)),
                  pl.BlockSpec((n_slc, nk), lambda b, qi: (0, 0))],
        out_specs=[pl.BlockSpec((None, tq, B_W), lambda b, qi: (b, qi, 0)),
                   pl.BlockSpec((None, B_GROUPS, n_slc, tq), lambda b, qi: (b, 0, 0, qi))],
        compiler_params=_params(("parallel", "parallel")),
        name="cmp_attn",
    )(qb.reshape(B_HEADS, B, S, LANES), kcc, vcc, jnp.asarray(ov_t, BF16))
    return o.reshape(B * S, B_W), sel


def _mla_prep_kernel(cq_ref, ckv_ref, misc_ref, gq_ref, gkv_ref, wq_ref, wk_ref, wv_ref, cc_ref, sc_ref,
                     q_ref, k_ref, v_ref):
    tm = cq_ref.shape[0]
    lane = lax.broadcasted_iota(I32, (tm, LANES), 1)
    cos = cc_ref[...]
    sin = sc_ref[...]

    def rope(t):
        return t * cos + _swap16(t, lane) * sin

    nq = _rms(cq_ref[...], gq_ref[...]).astype(BF16)
    q = jnp.dot(nq, wq_ref[...], preferred_element_type=F32)
    scale = (QK_NOPE + QK_ROPE) ** -0.5 * LOG2E
    for h in range(C_HEADS):
        q_ref[h] = (rope(q[:, h * LANES:(h + 1) * LANES]) * scale).astype(BF16)
    nkv = _rms(ckv_ref[...], gkv_ref[...]).astype(BF16)
    k = jnp.dot(nkv, wk_ref[...], preferred_element_type=F32)
    in_rope = (lane >= KR_LANE) & (lane < KR_LANE + QK_ROPE)
    kr = jnp.where(in_rope, rope(misc_ref[...]), 0.0)
    for h in range(C_HEADS):
        k_ref[:, h * LANES:(h + 1) * LANES] = (k[:, h * LANES:(h + 1) * LANES] + kr).astype(BF16)
    v_ref[...] = jnp.dot(nkv, wv_ref[...], preferred_element_type=F32).astype(BF16)


def _mla_prep(cq, ckv, misc, g_q, g_kv, wq_p, wk_p, wv_p, cc, sc):
    T = cq.shape[0]
    tm = min(512, T)
    row = lambda i: (i, 0)
    fix = lambda i: (0, 0)
    t128 = pl.BlockSpec((tm, LANES), row)
    return pl.pallas_call(
        _mla_prep_kernel,
        out_shape=[jax.ShapeDtypeStruct((C_HEADS, T, LANES), BF16),
                   jax.ShapeDtypeStruct((T, C_HEADS * LANES), BF16),
                   jax.ShapeDtypeStruct((T, C_W), BF16)],
        grid=(T // tm,),
        in_specs=[pl.BlockSpec((tm, Q_LORA), row), t128, t128,
                  pl.BlockSpec((1, Q_LORA), fix), pl.BlockSpec((1, KV_LORA), fix),
                  pl.BlockSpec((Q_LORA, C_HEADS * LANES), fix),
                  pl.BlockSpec((KV_LORA, C_HEADS * LANES), fix),
                  pl.BlockSpec((KV_LORA, C_W), fix), t128, t128],
        out_specs=[pl.BlockSpec((C_HEADS, tm, LANES), lambda i: (0, i, 0)),
                   pl.BlockSpec((tm, C_HEADS * LANES), row),
                   pl.BlockSpec((tm, C_W), row)],
        compiler_params=_params(("parallel",)),
        name="mla_prep",
    )(cq, ckv, misc, g_q, g_kv, wq_p, wk_p, wv_p, cc, sc)


def _outproj_kernel(x_ref, mod_ref, o1_ref, o2_ref, o3_ref, l1_ref, l2_ref, l3_ref,
                    ocmp_ref, oslc_ref, owin_ref, misc_ref, ge_ref, oc_ref,
                    ga_ref, gb_ref, gc_ref, w_ref, out_ref, scr):
    tm = x_ref.shape[0]

    def token_major(ref, d):
        if d == 1:
            return ref[...]
        for r in range(d):
            for p in range(A_W // LANES):
                scr[p, pl.ds(r, tm // d, stride=d), :] = ref[:, r * A_W + p * LANES:r * A_W + (p + 1) * LANES]
        return jnp.concatenate([scr[p] for p in range(A_W // LANES)], axis=1)

    o1, o2, o3 = [token_major(r, d) for r, d in zip((o1_ref, o2_ref, o3_ref), A_DILS)]
    l1, l2, l3 = [token_major(r, d) for r, d in zip((l1_ref, l2_ref, l3_ref), A_DILS)]
    mx = jnp.maximum(jnp.maximum(l1, l2), l3)
    e1, e2, e3 = jnp.exp(l1 - mx), jnp.exp(l2 - mx), jnp.exp(l3 - mx)
    oa = (e1 * o1 + e2 * o2 + e3 * o3) / (e1 + e2 + e3)

    sg = jax.nn.sigmoid(misc_ref[...])
    hi = sg.astype(BF16)
    lo = (sg - hi.astype(F32)).astype(BF16)

    def gate(br):
        return (jnp.dot(hi, ge_ref[br], preferred_element_type=F32)
                + jnp.dot(lo, ge_ref[br], preferred_element_type=F32))

    ob = gate(0) * ocmp_ref[...] + gate(1) * oslc_ref[...] + gate(2) * owin_ref[...]
    na = _rms(oa, ga_ref[...]).astype(BF16)
    nb = _rms(ob, gb_ref[...]).astype(BF16)
    nc = _rms(oc_ref[...], gc_ref[...]).astype(BF16)
    mixed = (jnp.dot(na, w_ref[0:A_W, :], preferred_element_type=F32)
             + jnp.dot(nb, w_ref[A_W:A_W + B_W, :], preferred_element_type=F32)
             + jnp.dot(nc, w_ref[A_W + B_W:, :], preferred_element_type=F32))
    out_ref[...] = x_ref[...] + mod_ref[0, 2:3, :] * mixed


def _gate_expand():
    ge = np.zeros((3, LANES, B_W), np.float32)
    for h in range(B_HEADS):
        for br in range(3):
            ge[br, h * 3 + br, h * HEAD:(h + 1) * HEAD] = 1.0
    return jnp.asarray(ge, BF16)


def _out_proj(x2, mods_l, oa_parts, lse_parts, ocmp, oslc, owin, misc, oc, g_a, g_b, g_c, w_out_b, S):
    T, D = x2.shape
    tm = min(256, S)
    spb = S // tm
    row = lambda i: (i, 0)
    fix = lambda i: (0, 0)
    t128 = pl.BlockSpec((tm, LANES), row)
    t256 = pl.BlockSpec((tm, 256), row)
    t512 = pl.BlockSpec((tm, 512), row)
    dil = [pl.BlockSpec((tm // d, d * A_W), row) for d in A_DILS]
    return pl.pallas_call(
        _outproj_kernel,
        out_shape=jax.ShapeDtypeStruct((T, D), F32),
        grid=(T // tm,),
        in_specs=[pl.BlockSpec((tm, D), row),
                  pl.BlockSpec((1, 6, D), lambda i: (i // spb, 0, 0)),
                  *dil, *dil, t512, t512, t512, t128,
                  pl.BlockSpec((3, LANES, B_W), lambda i: (0, 0, 0)),
                  t256,
                  pl.BlockSpec((1, A_W), fix), pl.BlockSpec((1, B_W), fix), pl.BlockSpec((1, C_W), fix),
                  pl.BlockSpec((D, D), fix)],
        out_specs=pl.BlockSpec((tm, D), row),
        scratch_shapes=[pltpu.VMEM((A_W // LANES, tm, LANES), F32)],
        compiler_params=_params(("parallel",)),
        name="out_proj",
    )(x2, mods_l, *oa_parts, *lse_parts, ocmp, oslc, owin, misc, _gate_expand(), oc, g_a, g_b, g_c, w_out_b)


def _pack_halves(v):
    bits = lax.bitcast_convert_type(v.astype(BF16).astype(F32), U32)
    half = v.shape[1] // 2
    return (bits[:, half:] & jnp.uint32(0xFFFF0000)) | (bits[:, :half] >> 16)


def _unpack_halves(w):
    lo = lax.bitcast_convert_type(w << 16, F32)
    hi = lax.bitcast_convert_type(w & jnp.uint32(0xFFFF0000), F32)
    return jnp.concatenate([lo, hi], axis=1)


def _put_planes(ref, w):
    for c in range(ref.shape[0]):
        ref[c] = w[:, c * LANES:(c + 1) * LANES]


def _get_planes(ref):
    return jnp.concatenate([ref[c] for c in range(ref.shape[0])], axis=1)


def _route_kernel(x_ref, mod_ref, g_ref, wh_ref, wl_ref, b_ref, hp_ref, e_ref, r_ref, gt_ref, cnt_ref, carry):
    tm = x_ref.shape[0]

    @pl.when(pl.program_id(0) == 0)
    def _():
        carry[...] = jnp.zeros(carry.shape, F32)

    h = _rms(x_ref[...], g_ref[...]) * (1.0 + mod_ref[0, 4:5, :]) + mod_ref[0, 3:4, :]
    hb = h.astype(BF16)
    hlo = (h - hb.astype(F32)).astype(BF16)
    _put_planes(hp_ref, _pack_halves(h))

    logits = (jnp.dot(hb, wh_ref[...], preferred_element_type=F32)
              + jnp.dot(hlo, wh_ref[...], preferred_element_type=F32)
              + jnp.dot(hb, wl_ref[...], preferred_element_type=F32)) + b_ref[...]
    lane = lax.broadcasted_iota(I32, (tm, LANES), 1)
    lane_f = lane.astype(F32)
    vals, idxs, picks = [], [], []
    lg = logits
    for _ in range(TOP_K):
        mx = lg.max(axis=-1, keepdims=True)
        idx = jnp.where(lg == mx, lane_f, float(LANES)).min(axis=-1, keepdims=True)
        pick = lane_f == idx
        vals.append(mx)
        idxs.append(idx.astype(I32))
        picks.append(pick)
        lg = jnp.where(pick, LOWEST, lg)
    es = [jnp.exp(v - vals[0]) for v in vals]
    den = es[0] + es[1] + es[2] + es[3]
    onehot = jnp.zeros((tm, LANES), F32)
    for pk in picks:
        onehot = jnp.where(pk, 1.0, onehot)
    tri = (lax.broadcasted_iota(I32, (tm, tm), 0) >= lax.broadcasted_iota(I32, (tm, tm), 1)).astype(BF16)
    cs = jnp.dot(tri, onehot.astype(BF16), preferred_element_type=F32)
    rank = carry[...] + cs - 1.0
    e_out = jnp.zeros((tm, LANES), I32)
    r_out = jnp.zeros((tm, LANES), I32)
    g_out = jnp.zeros((tm, LANES), F32)
    for k in range(TOP_K):
        rk = jnp.where(picks[k], rank, 0.0).sum(axis=-1, keepdims=True)
        e_out = jnp.where(lane == k, idxs[k], e_out)
        r_out = jnp.where(lane == k, rk.astype(I32), r_out)
        g_out = jnp.where(lane == k, es[k] / den, g_out)
    e_ref[...] = e_out
    r_ref[...] = r_out
    gt_ref[...] = g_out
    new = carry[...] + onehot.sum(axis=0, keepdims=True)
    carry[...] = new
    cnt_ref[...] = new


def _route(x2, mods_l, g2, wr_hi, wr_lo, br_p, S):
    T, D = x2.shape
    tm = min(256, S)
    spb = S // tm
    row = lambda i: (i, 0)
    fix = lambda i: (0, 0)
    t128 = pl.BlockSpec((tm, LANES), row)
    return pl.pallas_call(
        _route_kernel,
        out_shape=[jax.ShapeDtypeStruct((D // 2 // LANES, T, LANES), U32),
                   jax.ShapeDtypeStruct((T, LANES), I32), jax.ShapeDtypeStruct((T, LANES), I32),
                   jax.ShapeDtypeStruct((T, LANES), F32), jax.ShapeDtypeStruct((1, LANES), F32)],
        grid=(T // tm,),
        in_specs=[pl.BlockSpec((tm, D), row),
                  pl.BlockSpec((1, 6, D), lambda i: (i // spb, 0, 0)),
                  pl.BlockSpec((1, D), fix),
                  pl.BlockSpec((D, LANES), fix), pl.BlockSpec((D, LANES), fix), pl.BlockSpec((1, LANES), fix)],
        out_specs=[pl.BlockSpec((D // 2 // LANES, tm, LANES), lambda i: (0, i, 0)), t128, t128, t128,
                   pl.BlockSpec((1, LANES), fix)],
        scratch_shapes=[pltpu.VMEM((1, LANES), F32)],
        compiler_params=_params(("arbitrary",)),
        name="moe_route",
    )(x2, mods_l, g2, wr_hi, wr_lo, br_p)


SC_GATHER_ROWS = 128


def _sc_gather_rows(x, idx, name):
    n = idx.shape[0]
    W = x.shape[1]
    mesh = plsc.VectorSubcoreMesh(core_axis_name="core", subcore_axis_name="subcore")

    def body(x_hbm, i_hbm, o_hbm):
        def step(i_vmem, o_vmem):
            pltpu.sync_copy(x_hbm.at[i_vmem.at[0]], o_vmem)

        pltpu.emit_pipeline(
            step,
            grid=(n // SC_GATHER_ROWS,),
            in_specs=[pl.BlockSpec((1, SC_GATHER_ROWS), lambda i: (0, i))],
            out_specs=[pl.BlockSpec((SC_GATHER_ROWS, W), lambda i: (i, 0))],
            core_axis_name=("core", "subcore"),
            dimension_semantics=(pltpu.PARALLEL,),
        )(i_hbm, o_hbm)

    return pl.kernel(body, out_type=jax.ShapeDtypeStruct((n, W), x.dtype), mesh=mesh,
                     scratch_types=[], name=name)(x, idx.reshape(1, n))


def _sc_scatter_rows(x, idx, name):
    n, W = x.shape
    mesh = plsc.VectorSubcoreMesh(core_axis_name="core", subcore_axis_name="subcore")

    def body(x_hbm, i_hbm, o_hbm):
        def step(x_vmem, i_vmem):
            pltpu.sync_copy(x_vmem, o_hbm.at[i_vmem.at[0]])

        pltpu.emit_pipeline(
            step,
            grid=(n // SC_GATHER_ROWS,),
            in_specs=[pl.BlockSpec((SC_GATHER_ROWS, W), lambda i: (i, 0)),
                      pl.BlockSpec((1, SC_GATHER_ROWS), lambda i: (0, i))],
            out_specs=[],
            core_axis_name=("core", "subcore"),
            dimension_semantics=(pltpu.PARALLEL,),
        )(x_hbm, i_hbm)

    return pl.kernel(body, out_type=jax.ShapeDtypeStruct((n, W), x.dtype), mesh=mesh,
                     scratch_types=[], name=name)(x, idx.reshape(1, n))


SPLIT_COLS = 256


def _ffn_kernel(te_ref, nv_ref, chg_ref, x_ref, w1_ref, b1_ref, w2_ref, b2_ref, perm_ref, y_ref, w1b, w2b):
    j = pl.program_id(0)

    @pl.when(chg_ref[j] == 1)
    def _():
        half = SPLIT_COLS // 2
        for b in range(2 * D_FF // SPLIT_COLS):
            t = jnp.dot(w1_ref[:, b * SPLIT_COLS:(b + 1) * SPLIT_COLS].astype(BF16), perm_ref[...],
                        preferred_element_type=F32)
            w1b[:, b * half:(b + 1) * half] = t[:, :half].astype(BF16)
            w1b[:, D_FF + b * half:D_FF + (b + 1) * half] = t[:, half:].astype(BF16)
        w2b[...] = w2_ref[...].astype(BF16)

    @pl.when(j >= nv_ref[0])
    def _():
        y_ref[...] = jnp.zeros(y_ref.shape, U32)

    @pl.when(j < nv_ref[0])
    def _():
        xb = _unpack_halves(_get_planes(x_ref)).astype(BF16)
        u = jnp.dot(xb, w1b[...], preferred_element_type=F32) + b1_ref[0]
        glu = jnp.minimum(u[:, :D_FF], SWIGLU_LIMIT)
        lin = jnp.clip(u[:, D_FF:], -SWIGLU_LIMIT, SWIGLU_LIMIT)
        a = glu * jax.nn.sigmoid(SWIGLU_ALPHA * glu) * (lin + 1.0)
        y = jnp.dot(a.astype(BF16), w2b[...], preferred_element_type=F32) + b2_ref[0]
        _put_planes(y_ref, _pack_halves(y))


def _split_perm():
    p = np.zeros((SPLIT_COLS, SPLIT_COLS), np.float32)
    i = np.arange(SPLIT_COLS // 2)
    p[2 * i, i] = 1.0
    p[2 * i + 1, SPLIT_COLS // 2 + i] = 1.0
    return jnp.asarray(p, BF16)


def _expert_ffn(tile_e, n_valid, changed, xs, w1, b1p, w2, b2, tmf, layer):
    P, n_rows, _ = xs.shape
    D = 2 * P * LANES
    ntiles = n_rows // tmf
    return pl.pallas_call(
        _ffn_kernel,
        out_shape=jax.ShapeDtypeStruct((P, n_rows, LANES), U32),
        grid_spec=pltpu.PrefetchScalarGridSpec(
            num_scalar_prefetch=3,
            grid=(ntiles,),
            in_specs=[pl.BlockSpec((P, tmf, LANES), lambda j, te, nv, ch: (0, jnp.minimum(j, nv[0] - 1), 0)),
                      pl.BlockSpec((None, None, D, 2 * D_FF), lambda j, te, nv, ch: (layer, te[j], 0, 0)),
                      pl.BlockSpec((1, 1, 2 * D_FF), lambda j, te, nv, ch: (te[j], 0, 0)),
                      pl.BlockSpec((None, None, D_FF, D), lambda j, te, nv, ch: (layer, te[j], 0, 0)),
                      pl.BlockSpec((1, 1, D), lambda j, te, nv, ch: (te[j], 0, 0)),
                      pl.BlockSpec((SPLIT_COLS, SPLIT_COLS), lambda j, te, nv, ch: (0, 0))],
            out_specs=pl.BlockSpec((P, tmf, LANES), lambda j, te, nv, ch: (0, j, 0)),
            scratch_shapes=[pltpu.VMEM((D, 2 * D_FF), BF16), pltpu.VMEM((D_FF, D), BF16)]),
        compiler_params=_params(("arbitrary",)),
        name="moe_ffn",
    )(tile_e, n_valid, changed, xs, w1, b1p, w2, b2, _split_perm())


def _combine_kernel(yg_ref, gt_ref, x_ref, mod_ref, o_ref):
    g = gt_ref[...]
    y = None
    for k in range(TOP_K):
        yk = _unpack_halves(jnp.concatenate([yg_ref[c, k] for c in range(yg_ref.shape[0])], axis=1))
        y = g[:, k:k + 1] * yk if y is None else y + g[:, k:k + 1] * yk
    o_ref[...] = x_ref[...] + mod_ref[0, 5:6, :] * y


def _combine(yg, gates, x2, mods_l, S):
    T, D = x2.shape
    P = yg.shape[0]
    tc = min(512, S)
    spb = S // tc
    return pl.pallas_call(
        _combine_kernel,
        out_shape=jax.ShapeDtypeStruct((T, D), F32),
        grid=(T // tc,),
        in_specs=[pl.BlockSpec((P, TOP_K, tc, LANES), lambda i: (0, 0, i, 0)),
                  pl.BlockSpec((tc, LANES), lambda i: (i, 0)),
                  pl.BlockSpec((tc, D), lambda i: (i, 0)),
                  pl.BlockSpec((1, 6, D), lambda i: (i // spb, 0, 0))],
        out_specs=pl.BlockSpec((tc, D), lambda i: (i, 0)),
        compiler_params=_params(("parallel",)),
        name="moe_combine",
    )(yg, gates, x2, mods_l)


def _moe(x2, mods_l, g2, w_router, b_router, w1, b1p, w2, b2, S, layer):
    T, D = x2.shape
    tmf = 512
    wr = jnp.zeros((D, LANES), F32).at[:, :N_EXPERTS].set(w_router)
    wr_hi = wr.astype(BF16)
    wr_lo = (wr - wr_hi.astype(F32)).astype(BF16)
    br_p = jnp.full((1, LANES), NEG, F32).at[0, :N_EXPERTS].set(b_router)
    hp, e_idx, ranks, gates, cnt = _route(x2, mods_l, g2, wr_hi, wr_lo, br_p, S)

    counts = cnt[0, :N_EXPERTS].astype(I32)
    padded = ((counts + tmf - 1) // tmf) * tmf
    ends = jnp.cumsum(padded)
    offs = ends - padded
    dest = offs[e_idx[:, :TOP_K]] + ranks[:, :TOP_K]
    n_rows = TOP_K * T + N_EXPERTS * tmf
    ntiles = n_rows // tmf
    n_valid = (ends[-1] // tmf).astype(I32).reshape(1)
    tile_start = jnp.minimum(jnp.arange(ntiles, dtype=I32), n_valid[0] - 1) * tmf
    tile_e = jnp.sum(tile_start[:, None] >= ends[None, :], axis=1).astype(I32)
    n_pad = n_rows - TOP_K * T
    pad_cum = jnp.cumsum(padded - counts)
    p = jnp.arange(n_pad, dtype=I32)
    pe = jnp.sum(p[:, None] >= pad_cum[None, :], axis=1)
    pad_start = jnp.concatenate([offs + counts, ends[-1:]])
    pad_before = jnp.concatenate([jnp.zeros((1,), I32), pad_cum])
    pad_rows = pad_start[pe] + (p - pad_before[pe])
    perm = jnp.concatenate([dest.reshape(TOP_K * T), pad_rows.astype(I32)])
    vals = jnp.concatenate([jnp.arange(TOP_K * T, dtype=I32) // TOP_K, p % T])
    src = _sc_scatter_rows(jnp.broadcast_to(vals[:, None], (n_rows, LANES)), perm, "moe_invert_sc")[:, 0]
    changed = jnp.concatenate([jnp.ones((1,), I32), (tile_e[1:] != tile_e[:-1]).astype(I32)])
    P = hp.shape[0]
    plane = jnp.arange(P, dtype=I32)[:, None]
    src_rows = (src[None, :] + plane * T).reshape(P * n_rows)
    dest_rows = (dest.T.reshape(1, TOP_K * T) + plane * n_rows).reshape(P * TOP_K * T)

    xs = _sc_gather_rows(hp.reshape(P * T, LANES), src_rows, "moe_dispatch_sc").reshape(P, n_rows, LANES)
    ys = _expert_ffn(tile_e, n_valid, changed, xs, w1, b1p, w2, b2, tmf, layer)
    yg = _sc_gather_rows(ys.reshape(P * n_rows, LANES), dest_rows, "moe_collect_sc").reshape(P, TOP_K, T, LANES)
    return _combine(yg, gates, x2, mods_l, S)


def _final_kernel(x_ref, g_ref, o_ref):
    o_ref[...] = _rms(x_ref[...], g_ref[...])


def _final_norm(x2, g):
    T, D = x2.shape
    tm = min(512, T)
    return pl.pallas_call(
        _final_kernel,
        out_shape=jax.ShapeDtypeStruct((T, D), F32),
        grid=(T // tm,),
        in_specs=[pl.BlockSpec((tm, D), lambda i: (i, 0)), pl.BlockSpec((1, D), lambda i: (0, 0))],
        out_specs=pl.BlockSpec((tm, D), lambda i: (i, 0)),
        compiler_params=_params(("parallel",)),
        name="final_norm",
    )(x2, g.reshape(1, D))


def _rearrange_w_in(w):
    D = w.shape[0]
    main = w[:, :2048]
    gates = w[:, 2048:2072]
    cq = w[:, 2072:2328]
    ckv = w[:, 2328:2456]
    kr = w[:, 2456:2488]
    z = lambda n: jnp.zeros((D, n), w.dtype)
    misc = jnp.concatenate([gates, z(KR_LANE - 24), kr, z(LANES - KR_LANE - QK_ROPE)], axis=1)
    return jnp.concatenate([main, cq, ckv, misc], axis=1).astype(BF16)


def _mla_weights(w_q_up, w_kv_up):
    dq = QK_NOPE + QK_ROPE
    wq = jnp.zeros((Q_LORA, C_HEADS * LANES), F32)
    wk = jnp.zeros((KV_LORA, C_HEADS * LANES), F32)
    wv = []
    for h in range(C_HEADS):
        wq = wq.at[:, h * LANES:h * LANES + dq].set(w_q_up[:, h * dq:(h + 1) * dq])
        kv = w_kv_up[:, h * (QK_NOPE + HEAD):(h + 1) * (QK_NOPE + HEAD)]
        wk = wk.at[:, h * LANES:h * LANES + QK_NOPE].set(kv[:, :QK_NOPE])
        wv.append(kv[:, QK_NOPE:])
    return wq.astype(BF16), wk.astype(BF16), jnp.concatenate(wv, axis=1).astype(BF16)


def kernel(x, c, positions, w_ada, b_ada, g_norm1, g_norm2, w_in, nsa_pos_k, nsa_pos_v, nsa_w1_k, nsa_b1_k, nsa_w2_k, nsa_b2_k, nsa_w1_v, nsa_b1_v, nsa_w2_v, nsa_b2_v, mla_g_q, mla_g_kv, mla_w_q_up, mla_w_kv_up, g_out_a, g_out_b, g_out_c, w_out, w_router, b_router, w_exp1, b_exp1, w_exp2, b_exp2, g_final):
    B, S, D = x.shape
    T = B * S
    depth = w_ada.shape[0]
    x2 = x.reshape(T, D)

    cab, sab, cc, sc = _rope_tables(positions.reshape(T, 1))
    nch = S // CMP_STRIDE
    n_cmp = (S - CMP_LEN) // CMP_STRIDE + 1
    end_idx = np.minimum(np.arange(nch) * CMP_STRIDE + CMP_LEN - 1, S - 1)
    cos_e = cab.reshape(B, S, LANES)[:, end_idx]
    sin_e = sab.reshape(B, S, LANES)[:, end_idx]
    mods = _ada_mods(c, w_ada, b_ada).reshape(depth, B, 6, D)
    del n_cmp

    for l in range(depth):
        mods_l = mods[l]
        (qa, ka, va, qb, kcmp, vcmp, kslc, vslc, kwin, vwin, cq, ckv, misc) = _in_proj(
            x2, mods_l, g_norm1[l].reshape(1, D), _rearrange_w_in(w_in[l]), cab, sab, S)

        oa_parts, lse_parts = [], []
        for i, (window, d) in enumerate(A_PAIRS):
            o, lse = _dilated_attention(qa[i], ka[i], va[i], B, S, d, window // d)
            oa_parts.append(o)
            lse_parts.append(lse)

        kc = _compress(kcmp, nsa_pos_k[l], nsa_w1_k[l], nsa_b1_k[l], nsa_w2_k[l], nsa_b2_k[l],
                       cos_e, sin_e, B, S, True)
        vc = _compress(vcmp, nsa_pos_v[l], nsa_w1_v[l], nsa_b1_v[l], nsa_w2_v[l], nsa_b2_v[l],
                       cos_e, sin_e, B, S, False)
        kcc = jnp.concatenate([kc[:, 0, :, :HEAD], kc[:, 1, :, :HEAD]], axis=-1).astype(BF16)
        vcc = jnp.concatenate([vc[:, 0, :, :HEAD], vc[:, 1, :, :HEAD]], axis=-1).astype(BF16)
        ocmp, sel_t = _cmp_attention(qb, kcc, vcc, B, S)
        oslc = _select_attention(qb, kslc, vslc, sel_t, B, S)
        owin = _window_attention(qb, kwin, vwin, B, S)

        wq_p, wk_p, wv_p = _mla_weights(mla_w_q_up[l], mla_w_kv_up[l])
        qc, kcl, vcl = _mla_prep(cq, ckv, misc, mla_g_q[l].reshape(1, Q_LORA), mla_g_kv[l].reshape(1, KV_LORA),
                                 wq_p, wk_p, wv_p, cc, sc)
        oc = _latent_attention(qc, kcl, vcl, B, S)

        x2 = _out_proj(x2, mods_l, oa_parts, lse_parts, ocmp, oslc, owin, misc, oc,
                       g_out_a[l].reshape(1, A_W), g_out_b[l].reshape(1, B_W), g_out_c[l].reshape(1, C_W),
                       w_out[l].astype(BF16), S)

        b1r = b_exp1[l].reshape(N_EXPERTS, D_FF, 2)
        b1p = jnp.concatenate([b1r[..., 0], b1r[..., 1]], axis=-1).reshape(N_EXPERTS, 1, 2 * D_FF)
        x2 = _moe(x2, mods_l, g_norm2[l].reshape(1, D), w_router[l], b_router[l],
                  w_exp1, b1p, w_exp2, b_exp2[l].reshape(N_EXPERTS, 1, D), S, l)

    return _final_norm(x2, g_final).reshape(B, S, D)
```

```python
import functools

import numpy as np
import jax
import jax.numpy as jnp
from jax import lax
from jax.experimental import pallas as pl
from jax.experimental.pallas import tpu as pltpu
from jax.experimental.pallas import tpu_sc as plsc

F32 = jnp.float32
BF16 = jnp.bfloat16
I32 = jnp.int32
U32 = jnp.uint32

LANES = 128
HEAD = 64
NEG = -1e30
LOWEST = -3.0e38
LOG2E = 1.4426950408889634
LN2 = 0.6931471805599453
NORM_EPS = 1e-6
ROPE_THETA = 10000.0

A_HEADS, B_HEADS, B_GROUPS, C_HEADS = 4, 8, 2, 4
A_PAIRS = ((128, 1), (512, 4), (2048, 16))
A_W, B_W, C_W = 256, 512, 256
CMP_LEN, CMP_STRIDE, CMP_HIDDEN = 32, 16, 256
SLC_BLOCK, N_SELECT, WINDOW = 64, 16, 512
FORCE_BONUS = 1e4
Q_LORA, KV_LORA, QK_NOPE, QK_ROPE = 256, 128, 64, 32
N_EXPERTS, TOP_K, D_FF = 32, 4, 1024
SWIGLU_ALPHA, SWIGLU_LIMIT = 1.702, 7.0
IN_COLS = 2560
KR_LANE = 64

VMEM_LIMIT = 56 * 1024 * 1024


def _params(sem, vmem=VMEM_LIMIT):
    return pltpu.CompilerParams(dimension_semantics=sem, vmem_limit_bytes=vmem)


def _rms(x, g):
    return x * lax.rsqrt(jnp.mean(x * x, axis=-1, keepdims=True) + NORM_EPS) * g


def _rope_rows():
    inv64 = ROPE_THETA ** (-np.arange(32, dtype=np.float32) * (2.0 / 64))
    inv32 = ROPE_THETA ** (-np.arange(16, dtype=np.float32) * (2.0 / 32))
    inv_ab = np.tile(np.concatenate([inv64, inv64]), 2)
    sgn_ab = np.tile(np.concatenate([-np.ones(32), np.ones(32)]), 2)
    inv_c = np.zeros(LANES, np.float32)
    sgn_c = np.zeros(LANES, np.float32)
    inv_c[KR_LANE:KR_LANE + 16] = inv32
    inv_c[KR_LANE + 16:KR_LANE + 32] = inv32
    sgn_c[KR_LANE:KR_LANE + 16] = -1.0
    sgn_c[KR_LANE + 16:KR_LANE + 32] = 1.0
    rows = np.stack([inv_ab, sgn_ab, inv_c, sgn_c]).astype(np.float32)
    return jnp.asarray(rows)


def _rope_tab_kernel(pos_ref, rows_ref, cab_ref, sab_ref, cc_ref, sc_ref):
    pos = pos_ref[...].astype(F32)
    ang = pos * rows_ref[0:1, :]
    cab_ref[...] = jnp.cos(ang)
    sab_ref[...] = jnp.sin(ang) * rows_ref[1:2, :]
    ang = pos * rows_ref[2:3, :]
    cc_ref[...] = jnp.cos(ang)
    sc_ref[...] = jnp.sin(ang) * rows_ref[3:4, :]


def _rope_tables(pos_col):
    R = pos_col.shape[0]
    tm = min(R, 1024)
    spec = pl.BlockSpec((tm, LANES), lambda i: (i, 0))
    return pl.pallas_call(
        _rope_tab_kernel,
        out_shape=[jax.ShapeDtypeStruct((R, LANES), F32)] * 4,
        grid=(R // tm,),
        in_specs=[pl.BlockSpec((tm, 1), lambda i: (i, 0)), pl.BlockSpec((4, LANES), lambda i: (0, 0))],
        out_specs=[spec] * 4,
        compiler_params=_params(("parallel",)),
        name="rope_tables",
    )(pos_col, _rope_rows())


def _swap32(t, lane):
    return jnp.where((lane & 32) == 0, pltpu.roll(t, 96, 1), pltpu.roll(t, 32, 1))


def _swap16(t, lane):
    return jnp.where(lane < KR_LANE + 16, pltpu.roll(t, 112, 1), pltpu.roll(t, 16, 1))


def _mods_kernel(c_ref, w_ref, b_ref, o_ref):
    c = c_ref[...]
    ca = (c * jax.nn.sigmoid(c)).astype(BF16)
    o_ref[0] = jnp.dot(ca, w_ref[0].astype(BF16), preferred_element_type=F32) + b_ref[0]


def _ada_mods(c, w_ada, b_ada):
    L, D, N = w_ada.shape
    B = c.shape[0]
    tn = 1536
    return pl.pallas_call(
        _mods_kernel,
        out_shape=jax.ShapeDtypeStruct((L, B, N), F32),
        grid=(L, N // tn),
        in_specs=[pl.BlockSpec((B, D), lambda l, j: (0, 0)),
                  pl.BlockSpec((1, D, tn), lambda l, j: (l, 0, j)),
                  pl.BlockSpec((1, 1, tn), lambda l, j: (l, 0, j))],
        out_specs=pl.BlockSpec((1, B, tn), lambda l, j: (l, 0, j)),
        compiler_params=_params(("parallel", "parallel")),
        name="ada_mods",
    )(c, w_ada, b_ada.reshape(L, 1, N))


A_DILS = tuple(d for _, d in A_PAIRS)


def _inproj_kernel(x_ref, mod_ref, g_ref, w_ref, cab_ref, sab_ref, *refs):
    nd = len(A_DILS)
    qa_refs, ka_refs, va_refs = refs[0:nd], refs[nd:2 * nd], refs[2 * nd:3 * nd]
    (qb_ref, kcmp_ref, vcmp_ref, kslc_ref, vslc_ref, kwin_ref, vwin_ref, cq_ref, ckv_ref, misc_ref,
     scr) = refs[3 * nd:]
    tm = x_ref.shape[0]

    def spread(val, store):
        W = val.shape[1]
        for p in range(W // LANES):
            scr[p] = val[:, p * LANES:(p + 1) * LANES]
        for i, d in enumerate(A_DILS):
            if d == 1:
                store(i, 0, val.astype(BF16))
            else:
                for r in range(d):
                    planes = [scr[p, pl.ds(r, tm // d, stride=d), :] for p in range(W // LANES)]
                    piece = planes[0] if len(planes) == 1 else jnp.concatenate(planes, axis=1)
                    store(i, r, piece.astype(BF16))
    x = x_ref[...]
    h = (_rms(x, g_ref[...]) * (1.0 + mod_ref[0, 1:2, :]) + mod_ref[0, 0:1, :]).astype(BF16)
    cos = cab_ref[...]
    sin = sab_ref[...]
    lane = lax.broadcasted_iota(I32, (tm, LANES), 1)
    lo64 = lane < HEAD

    def proj(c0, n):
        return jnp.dot(h, w_ref[:, c0:c0 + n], preferred_element_type=F32)

    def rope(t):
        return t * cos + _swap32(t, lane) * sin

    def pad_q(h0, t, dst):
        t = rope(t) * (HEAD ** -0.5 * LOG2E)
        tr = pltpu.roll(t, HEAD, 1)
        return [jnp.where(lo64 if dst[i] == 0 else ~lo64, t if dst[i] == i else tr, 0.0) for i in range(2)]

    def put_a(arefs, width, lane0=0):
        def store(i, r, piece):
            arefs[i][:, r * A_W + lane0:r * A_W + lane0 + width] = piece
        return store

    y = proj(0, 256)
    for j in range(2):
        for i, qh in enumerate(pad_q(2 * j, y[:, 128 * j:128 * (j + 1)], (0, 1))):
            def store(di, r, piece, hd=2 * j + i):
                qa_refs[di][hd, :, r * LANES:(r + 1) * LANES] = piece

            spread(qh, store)
    y = proj(256, 256)
    spread(jnp.concatenate([rope(y[:, :128]), rope(y[:, 128:])], axis=1), put_a(ka_refs, A_W))
    spread(proj(512, 256), put_a(va_refs, A_W))
    for j in range(2):
        y = proj(768 + 256 * j, 256)
        for jj in range(2):
            for i, qh in enumerate(pad_q(4 * j + 2 * jj, y[:, 128 * jj:128 * (jj + 1)], (j, j))):
                qb_ref[4 * j + 2 * jj + i] = qh.astype(BF16)
    y = proj(1280, 256)
    kcmp_ref[...] = y[:, :128]
    vcmp_ref[...] = y[:, 128:]
    y = proj(1536, 256)
    kslc_ref[...] = rope(y[:, :128]).astype(BF16)
    vslc_ref[...] = y[:, 128:].astype(BF16)
    y = proj(1792, 256)
    kwin_ref[...] = rope(y[:, :128]).astype(BF16)
    vwin_ref[...] = y[:, 128:].astype(BF16)
    cq_ref[...] = proj(2048, 256)
    y = proj(2304, 256)
    ckv_ref[...] = y[:, :128]
    misc_ref[...] = y[:, 128:]


def _in_proj(x2, mods_l, g1, w_in_r, cab, sab, S):
    T, D = x2.shape
    tm = min(512, S)
    spb = S // tm
    row = lambda i: (i, 0)
    shp = jax.ShapeDtypeStruct
    outs = ([shp((A_HEADS, T // d, d * LANES), BF16) for d in A_DILS]
            + [shp((T // d, d * A_W), BF16) for d in A_DILS] * 2
            + [shp((B_HEADS, T, LANES), BF16),
               shp((T, LANES), F32), shp((T, LANES), F32), shp((T, LANES), BF16), shp((T, LANES), BF16),
               shp((T, LANES), BF16), shp((T, LANES), BF16),
               shp((T, Q_LORA), F32), shp((T, LANES), F32), shp((T, LANES), F32)])
    t128 = pl.BlockSpec((tm, LANES), row)
    t256 = pl.BlockSpec((tm, 256), row)
    out_specs = ([pl.BlockSpec((A_HEADS, tm // d, d * LANES), lambda i: (0, i, 0)) for d in A_DILS]
                 + [pl.BlockSpec((tm // d, d * A_W), row) for d in A_DILS] * 2
                 + [pl.BlockSpec((B_HEADS, tm, LANES), lambda i: (0, i, 0)),
                    t128, t128, t128, t128, t128, t128, t256, t128, t128])
    res = pl.pallas_call(
        _inproj_kernel,
        out_shape=outs,
        grid=(T // tm,),
        in_specs=[pl.BlockSpec((tm, D), row),
                  pl.BlockSpec((1, 6, D), lambda i: (i // spb, 0, 0)),
                  pl.BlockSpec((1, D), lambda i: (0, 0)),
                  pl.BlockSpec((D, IN_COLS), lambda i: (0, 0)),
                  t128, t128],
        out_specs=out_specs,
        scratch_shapes=[pltpu.VMEM((A_W // LANES, tm, LANES), F32)],
        compiler_params=_params(("parallel",)),
        name="in_proj",
    )(x2, mods_l, g1, w_in_r, cab, sab)
    nd = len(A_DILS)
    return (res[0:nd], res[nd:2 * nd], res[2 * nd:3 * nd]) + tuple(res[3 * nd:])


_NT = (((1,), (1,)), ((), ()))
_TN = (((0,), (0,)), ((), ()))


KEY_SUB = 128
QUERY_SUB = 256


def _online_chunk(k, v, q, keep, bias, m, l, acc):
    s = lax.dot_general(k, q, _NT, preferred_element_type=F32)
    if bias is not None:
        s = s + bias
    if keep is not None:
        s = jnp.where(keep, s, NEG)
    m_next = jnp.maximum(m, s.max(axis=0, keepdims=True))
    alpha = jnp.exp2(m - m_next)
    p = jnp.exp2(s - m_next)
    l = alpha * l + p.sum(axis=0, keepdims=True)
    acc = acc * alpha + lax.dot_general(v, p.astype(BF16), _TN, preferred_element_type=F32)
    return m_next, l, acc


def _mask_heads(s, masks, tq, fill):
    return jnp.concatenate([jnp.where(masks[h], s[:, h * tq:(h + 1) * tq], fill)
                            for h in range(len(masks))], axis=1)


def _emit_heads(o_ref, acc_t, halves, tq):
    nh = len(halves)
    lo64 = lax.broadcasted_iota(I32, (tq, LANES), 1) < HEAD
    for c in range(nh // 2):
        parts = []
        for h in (2 * c, 2 * c + 1):
            o = acc_t[:, h * tq:(h + 1) * tq].T
            if halves[h] != h % 2:
                o = pltpu.roll(o, HEAD, 1)
            parts.append(o)
        o_ref[:, c * LANES:(c + 1) * LANES] = jnp.where(lo64, parts[0], parts[1])


def _band_kernel(*refs, nblk, tq, tk, max_dist, stacks, halves, with_lse, qi_axis):
    q_ref = refs[0]
    k_refs = refs[1:1 + nblk]
    v_refs = refs[1 + nblk:1 + 2 * nblk]
    o_ref = refs[1 + 2 * nblk]
    lse_ref = refs[2 + 2 * nblk] if with_lse else None
    nh = len(halves)
    qi = pl.program_id(qi_axis)
    ks = min(KEY_SUB, tk)
    cq = min(QUERY_SUB, tq)
    rc = lax.broadcasted_iota(I32, (ks, cq), 0) - lax.broadcasted_iota(I32, (ks, cq), 1)
    head_blk = {}
    for (h0, n, blk) in stacks:
        for h in range(h0, h0 + n):
            head_blk[h] = blk
    outs, lses = [], []
    for h in range(nh):
        lanes = slice(head_blk[h] * LANES, (head_blk[h] + 1) * LANES)
        for c in range(tq // cq):
            q = q_ref[h, c * cq:(c + 1) * cq, :]
            m = jnp.full((1, cq), NEG, F32)
            l = jnp.zeros((1, cq), F32)
            acc = jnp.zeros((LANES, cq), F32)
            for i in range(nblk):
                kt = qi * (tq // tk) - (nblk - tq // tk) + i
                for kk in range(tk // ks):
                    base = qi * tq + c * cq - (kt * tk + kk * ks)
                    base = jnp.where(kt >= 0, base, -(1 << 20))
                    keep = (rc <= base) & (rc >= base - max_dist)
                    m, l, acc = _online_chunk(k_refs[i][kk * ks:(kk + 1) * ks, lanes],
                                              v_refs[i][kk * ks:(kk + 1) * ks, lanes], q, keep, None, m, l, acc)
            outs.append(acc / l)
            lses.append(jnp.broadcast_to(m * LN2 + jnp.log(l), (LANES, cq)))
    _emit_heads(o_ref, jnp.concatenate(outs, axis=1), halves, tq)
    if with_lse:
        _emit_heads(lse_ref, jnp.concatenate(lses, axis=1), tuple(h % 2 for h in range(nh)), tq)


def _dilated_attention(qa, ka, va, B, S, d, max_dist):
    L = S // d
    tq = min(512, L)
    tk = 128
    nprev = -(-max_dist // tk)
    nblk = nprev + tq // tk
    q_v = qa.reshape(A_HEADS, B, L, d * LANES)
    k_v = ka.reshape(B, L, d * A_W)
    v_v = va.reshape(B, L, d * A_W)
    kv_specs = []
    for i in range(nblk):
        kv_specs.append(pl.BlockSpec(
            (None, tk, A_W),
            functools.partial(lambda b, r, qi, i: (b, jnp.maximum(qi * (tq // tk) - nprev + i, 0), r), i=i)))
    out_spec = pl.BlockSpec((None, tq, A_W), lambda b, r, qi: (b, qi, r))
    kern = functools.partial(_band_kernel, nblk=nblk, tq=tq, tk=tk, max_dist=max_dist,
                             stacks=((0, 2, 0), (2, 2, 1)), halves=(0, 1, 0, 1), with_lse=True, qi_axis=2)
    o, lse = pl.pallas_call(
        kern,
        out_shape=[jax.ShapeDtypeStruct((B, L, d * A_W), F32)] * 2,
        grid=(B, d, L // tq),
        in_specs=[pl.BlockSpec((A_HEADS, None, tq, LANES), lambda b, r, qi: (0, b, qi, r))]
                 + kv_specs + kv_specs,
        out_specs=[out_spec, out_spec],
        compiler_params=_params(("parallel", "parallel", "parallel")),
        name=f"dilated_attn_d{d}",
    )(q_v, *([k_v] * nblk), *([v_v] * nblk))
    return o.reshape(B * L, d * A_W), lse.reshape(B * L, d * A_W)


def _window_attention(qb, kwin, vwin, B, S):
    tq = min(256, S)
    tk = tq
    max_dist = WINDOW - 1
    nprev = -(-max_dist // tk)
    nblk = nprev + 1
    q_v = qb.reshape(B_HEADS, B, S, LANES)
    k_v = kwin.reshape(B, S, LANES)
    v_v = vwin.reshape(B, S, LANES)
    kv_specs = [pl.BlockSpec((None, tk, LANES),
                             functools.partial(lambda b, qi, i: (b, jnp.maximum(qi - nprev + i, 0), 0), i=i))
                for i in range(nblk)]
    kern = functools.partial(_band_kernel, nblk=nblk, tq=tq, tk=tk, max_dist=max_dist,
                             stacks=((0, B_HEADS, 0),), halves=(0, 0, 0, 0, 1, 1, 1, 1),
                             with_lse=False, qi_axis=1)
    o = pl.pallas_call(
        kern,
        out_shape=jax.ShapeDtypeStruct((B, S, B_W), F32),
        grid=(B, S // tq),
        in_specs=[pl.BlockSpec((B_HEADS, None, tq, LANES), lambda b, qi: (0, b, qi, 0))]
                 + kv_specs + kv_specs,
        out_specs=pl.BlockSpec((None, tq, B_W), lambda b, qi: (b, qi, 0)),
        compiler_params=_params(("parallel", "parallel")),
        name="window_attn",
    )(q_v, *([k_v] * nblk), *([v_v] * nblk))
    return o.reshape(B * S, B_W)


def _flash_kernel(qi_ref, kj_ref, first_ref, last_ref, *refs, tq, tk, qk_groups, pv_groups, halves, select):
    if select:
        q_ref, k_ref, v_ref, sel_ref, o_ref, m_sc, l_sc, acc_sc = refs
    else:
        q_ref, k_ref, v_ref, o_ref, m_sc, l_sc, acc_sc = refs
    nh = len(halves)
    step = pl.program_id(1)
    qi = qi_ref[step]
    kj = kj_ref[step]

    @pl.when(first_ref[step] == 1)
    def _():
        m_sc[...] = jnp.full(m_sc.shape, NEG, F32)
        l_sc[...] = jnp.zeros(l_sc.shape, F32)
        acc_sc[...] = jnp.zeros(acc_sc.shape, F32)

    ks = min(KEY_SUB, tk)
    cq = min(QUERY_SUB, tq)
    rc = lax.broadcasted_iota(I32, (ks, cq), 0) - lax.broadcasted_iota(I32, (ks, cq), 1)
    head_kb, head_vb = {}, {}
    for (h0, n, kb) in qk_groups:
        for h in range(h0, h0 + n):
            head_kb[h] = kb
    for (h0, n, vb) in pv_groups:
        for h in range(h0, h0 + n):
            head_vb[h] = vb
    hpg = nh // B_GROUPS
    spb = ks // SLC_BLOCK if select else 1

    def run(on_diagonal):
        for h in range(nh):
            klanes = slice(head_kb[h] * LANES, (head_kb[h] + 1) * LANES)
            vlanes = slice(head_vb[h] * LANES, (head_vb[h] + 1) * LANES)
            for c in range(tq // cq):
                cols = slice(h * tq + c * cq, h * tq + (c + 1) * cq)
                q = q_ref[h, c * cq:(c + 1) * cq, :]
                m, l, acc = m_sc[:, cols], l_sc[:, cols], acc_sc[:, cols]
                for kk in range(tk // ks):
                    keep = None
                    if on_diagonal:
                        keep = rc <= (qi * tq + c * cq) - (kj * tk + kk * ks)
                    bias = None
                    if select:
                        rows = [jnp.broadcast_to(
                            (sel_ref[h // hpg, pl.ds((kj * (tk // ks) + kk) * spb + j, 1), c * cq:(c + 1) * cq] - 1.0)
                            * (-NEG), (SLC_BLOCK, cq)) for j in range(spb)]
                        bias = rows[0] if spb == 1 else jnp.concatenate(rows, axis=0)
                    m, l, acc = _online_chunk(k_ref[kk * ks:(kk + 1) * ks, klanes],
                                              v_ref[kk * ks:(kk + 1) * ks, vlanes], q, keep, bias, m, l, acc)
                m_sc[:, cols] = m
                l_sc[:, cols] = l
                acc_sc[:, cols] = acc

    crosses = kj * tk + (tk - 1) > qi * tq

    @pl.when(crosses)
    def _():
        run(True)

    @pl.when(jnp.logical_not(crosses))
    def _():
        run(False)

    @pl.when(last_ref[step] == 1)
    def _():
        _emit_heads(o_ref, acc_sc[...] / l_sc[...], halves, tq)


def _causal_schedule(S, tq, tk):
    qi, kj, first, last = [], [], [], []
    for i in range(S // tq):
        nk = (i * tq + tq - 1) // tk + 1
        for j in range(nk):
            qi.append(i)
            kj.append(j)
            first.append(1 if j == 0 else 0)
            last.append(1 if j == nk - 1 else 0)
    return [jnp.asarray(np.asarray(a, np.int32)) for a in (qi, kj, first, last)]


def _select_attention(qb, kslc, vslc, sel_t, B, S):
    tq = min(512, S)
    tk = min(512, S)
    n_slc = S // SLC_BLOCK
    tabs = _causal_schedule(S, tq, tk)
    nsteps = tabs[0].shape[0]
    kern = functools.partial(_flash_kernel, tq=tq, tk=tk, qk_groups=((0, B_HEADS, 0),),
                             pv_groups=((0, B_HEADS, 0),), halves=(0, 0, 0, 0, 1, 1, 1, 1), select=True)
    o = pl.pallas_call(
        kern,
        out_shape=jax.ShapeDtypeStruct((B, S, B_W), F32),
        grid_spec=pltpu.PrefetchScalarGridSpec(
            num_scalar_prefetch=4,
            grid=(B, nsteps),
            in_specs=[pl.BlockSpec((B_HEADS, None, tq, LANES), lambda b, s, qi, kj, f, l: (0, b, qi[s], 0)),
                      pl.BlockSpec((None, tk, LANES), lambda b, s, qi, kj, f, l: (b, kj[s], 0)),
                      pl.BlockSpec((None, tk, LANES), lambda b, s, qi, kj, f, l: (b, kj[s], 0)),
                      pl.BlockSpec((None, B_GROUPS, n_slc, tq), lambda b, s, qi, kj, f, l: (b, 0, 0, qi[s]))],
            out_specs=pl.BlockSpec((None, tq, B_W), lambda b, s, qi, kj, f, l: (b, qi[s], 0)),
            scratch_shapes=[pltpu.VMEM((1, B_HEADS * tq), F32), pltpu.VMEM((1, B_HEADS * tq), F32),
                            pltpu.VMEM((LANES, B_HEADS * tq), F32)]),
        compiler_params=_params(("parallel", "arbitrary")),
        name="select_attn",
    )(*tabs, qb.reshape(B_HEADS, B, S, LANES), kslc.reshape(B, S, LANES), vslc.reshape(B, S, LANES), sel_t)
    return o.reshape(B * S, B_W)


def _latent_attention(qc, kc, vc, B, S):
    tq = min(512, S)
    tk = min(512, S)
    tabs = _causal_schedule(S, tq, tk)
    nsteps = tabs[0].shape[0]
    kern = functools.partial(_flash_kernel, tq=tq, tk=tk,
                             qk_groups=tuple((h, 1, h) for h in range(C_HEADS)),
                             pv_groups=((0, 2, 0), (2, 2, 1)), halves=(0, 1, 0, 1), select=False)
    o = pl.pallas_call(
        kern,
        out_shape=jax.ShapeDtypeStruct((B, S, C_W), F32),
        grid_spec=pltpu.PrefetchScalarGridSpec(
            num_scalar_prefetch=4,
            grid=(B, nsteps),
            in_specs=[pl.BlockSpec((C_HEADS, None, tq, LANES), lambda b, s, qi, kj, f, l: (0, b, qi[s], 0)),
                      pl.BlockSpec((None, tk, C_HEADS * LANES), lambda b, s, qi, kj, f, l: (b, kj[s], 0)),
                      pl.BlockSpec((None, tk, C_W), lambda b, s, qi, kj, f, l: (b, kj[s], 0))],
            out_specs=pl.BlockSpec((None, tq, C_W), lambda b, s, qi, kj, f, l: (b, qi[s], 0)),
            scratch_shapes=[pltpu.VMEM((1, C_HEADS * tq), F32), pltpu.VMEM((1, C_HEADS * tq), F32),
                            pltpu.VMEM((LANES, C_HEADS * tq), F32)]),
        compiler_params=_params(("parallel", "arbitrary")),
        name="latent_attn",
    )(*tabs, qc.reshape(C_HEADS, B, S, LANES), kc.reshape(B, S, C_HEADS * LANES), vc.reshape(B, S, C_W))
    return o.reshape(B * S, C_W)


def _gelu_tanh(x):
    return 0.5 * x * (1.0 + jnp.tanh(0.7978845608028654 * (x + 0.044715 * x * x * x)))


def _compress_kernel(u_ref, pos_ref, w1_ref, b1_ref, w2_ref, b2_ref, cos_ref, sin_ref, o_ref, *, rotate):
    n = u_ref.shape[0]
    half = CMP_STRIDE * HEAD
    u = u_ref[...]
    ua = (u + pos_ref[:, :half]).astype(BF16)
    ub = (u + pos_ref[:, half:]).astype(BF16)
    pa = jnp.dot(ua, w1_ref[:half, :].astype(BF16), preferred_element_type=F32)
    pb = jnp.dot(ub, w1_ref[half:, :].astype(BF16), preferred_element_type=F32)
    hid = _gelu_tanh(pa + pltpu.roll(pb, n - 1, 0) + b1_ref[...])
    y = jnp.dot(hid.astype(BF16), w2_ref[...].astype(BF16), preferred_element_type=F32) + b2_ref[...]
    if rotate:
        lane = lax.broadcasted_iota(I32, y.shape, 1)
        y = y * cos_ref[...] + _swap32(y, lane) * sin_ref[...]
    o_ref[...] = y


def _compress(t, pos_emb, w1, b1, w2, b2, cos_e, sin_e, B, S, rotate):
    nch = S // CMP_STRIDE
    u = t.reshape(B, S, B_GROUPS, HEAD).transpose(0, 2, 1, 3).reshape(B, B_GROUPS, nch, CMP_STRIDE * HEAD)
    w2d = jnp.concatenate([w2, w2], axis=1)
    b2d = jnp.concatenate([b2, b2]).reshape(1, LANES)
    kern = functools.partial(_compress_kernel, rotate=rotate)
    tab = pl.BlockSpec((None, nch, LANES), lambda b, g: (b, 0, 0))
    return pl.pallas_call(
        kern,
        out_shape=jax.ShapeDtypeStruct((B, B_GROUPS, nch, LANES), F32),
        grid=(B, B_GROUPS),
        in_specs=[pl.BlockSpec((None, None, nch, CMP_STRIDE * HEAD), lambda b, g: (b, g, 0, 0)),
                  pl.BlockSpec((1, CMP_LEN * HEAD), lambda b, g: (0, 0)),
                  pl.BlockSpec((CMP_LEN * HEAD, CMP_HIDDEN), lambda b, g: (0, 0)),
                  pl.BlockSpec((1, CMP_HIDDEN), lambda b, g: (0, 0)),
                  pl.BlockSpec((CMP_HIDDEN, LANES), lambda b, g: (0, 0)),
                  pl.BlockSpec((1, LANES), lambda b, g: (0, 0)),
                  tab, tab],
        out_specs=pl.BlockSpec((None, None, nch, LANES), lambda b, g: (b, g, 0, 0)),
        compiler_params=_params(("parallel", "parallel")),
        name="nsa_compress",
    )(u, pos_emb.reshape(1, CMP_LEN * HEAD), w1, b1.reshape(1, CMP_HIDDEN), w2d, b2d, cos_e, sin_e)


def _cmp_attn_kernel(q_ref, kc_ref, vc_ref, ov_ref, o_ref, sel_ref, *, tq, n_cmp, n_slc):
    qi = pl.program_id(1)
    nk = kc_ref.shape[0]
    q = q_ref[...].reshape(B_HEADS * tq, LANES)
    s = lax.dot_general(kc_ref[...], q, _NT, preferred_element_type=F32)
    t_idx = qi * tq + lax.broadcasted_iota(I32, (nk, tq), 1)
    c_idx = lax.broadcasted_iota(I32, (nk, tq), 0)
    mask = ((c_idx * CMP_STRIDE + (CMP_LEN - 1)) <= t_idx) & (c_idx < n_cmp)
    masks = [mask] * B_HEADS
    s = _mask_heads(s, masks, tq, NEG)
    m = s.max(axis=0, keepdims=True)
    e = _mask_heads(jnp.exp2(s - m), masks, tq, 0.0)
    den = e.sum(axis=0, keepdims=True)
    p = e / jnp.where(den > 0, den, 1.0)
    o_t = lax.dot_general(vc_ref[...], p.astype(BF16), _TN, preferred_element_type=F32)
    hpg = B_HEADS // B_GROUPS
    _emit_heads(o_ref, o_t, tuple(h // hpg for h in range(B_HEADS)), tq)

    j_idx = lax.broadcasted_iota(I32, (n_slc, tq), 0)
    t_col = qi * tq + lax.broadcasted_iota(I32, (n_slc, tq), 1)
    cur = t_col // SLC_BLOCK
    forced = (j_idx == 0) | (j_idx == cur) | (j_idx == cur - 1)
    valid = j_idx * SLC_BLOCK <= t_col
    for g in range(B_GROUPS):
        ps = p[:, g * hpg * tq:(g * hpg + 1) * tq]
        for h in range(g * hpg + 1, (g + 1) * hpg):
            ps = ps + p[:, h * tq:(h + 1) * tq]
        hi = ps.astype(BF16)
        lo = (ps - hi.astype(F32)).astype(BF16)
        imp = (jnp.dot(ov_ref[...], hi, preferred_element_type=F32)
               + jnp.dot(ov_ref[...], lo, preferred_element_type=F32))
        score = jnp.where(forced, imp + FORCE_BONUS, jnp.where(valid, imp, -1.0))
        sel = jnp.zeros((n_slc, tq), F32)
        j_f = j_idx.astype(F32)
        for _ in range(min(N_SELECT, n_slc)):
            mx = score.max(axis=0, keepdims=True)
            idx = jnp.where(score == mx, j_f, float(n_slc)).min(axis=0, keepdims=True)
            pick = j_f == idx
            sel = jnp.where(pick, 1.0, sel)
            score = jnp.where(pick, LOWEST, score)
        sel_ref[g] = sel


def _cmp_attention(qb, kcc, vcc, B, S):
    tq = min(256, S)
    nk = S // CMP_STRIDE
    n_cmp = (S - CMP_LEN) // CMP_STRIDE + 1
    n_slc = S // SLC_BLOCK
    cmp_tok = np.arange(n_cmp)[:, None] * CMP_STRIDE + np.arange(CMP_LEN)[None, :]
    overlap = ((cmp_tok[:, :, None] // SLC_BLOCK) == np.arange(n_slc)[None, None, :]).mean(axis=1)
    ov_t = np.zeros((n_slc, nk), np.float32)
    ov_t[:, :n_cmp] = overlap.T
    kern = functools.partial(_cmp_attn_kernel, tq=tq, n_cmp=n_cmp, n_slc=n_slc)
    o, sel = pl.pallas_call(
        kern,
        out_shape=[jax.ShapeDtypeStruct((B, S, B_W), F32),
                   jax.ShapeDtypeStruct((B, B_GROUPS, n_slc, S), F32)],
        grid=(B, S // tq),
        in_specs=[pl.BlockSpec((B_HEADS, None, tq, LANES), lambda b, qi: (0, b, qi, 0)),
                  pl.BlockSpec((None, nk, LANES), lambda b, qi: (b, 0, 0)),
                  pl.BlockSpec((None, nk, LANES), lambda b, qi: (b, 0, 0)),
                  pl.BlockSpec((n_slc, nk), lambda b, qi: (0, 0))],
        out_specs=[pl.BlockSpec((None, tq, B_W), lambda b, qi: (b, qi, 0)),
                   pl.BlockSpec((None, B_GROUPS, n_slc, tq), lambda b, qi: (b, 0, 0, qi))],
        compiler_params=_params(("parallel", "parallel")),
        name="cmp_attn",
    )(qb.reshape(B_HEADS, B, S, LANES), kcc, vcc, jnp.asarray(ov_t, BF16))
    return o.reshape(B * S, B_W), sel


def _mla_prep_kernel(cq_ref, ckv_ref, misc_ref, gq_ref, gkv_ref, wq_ref, wk_ref, wv_ref, cc_ref, sc_ref,
                     q_ref, k_ref, v_ref):
    tm = cq_ref.shape[0]
    lane = lax.broadcasted_iota(I32, (tm, LANES), 1)
    cos = cc_ref[...]
    sin = sc_ref[...]

    def rope(t):
        return t * cos + _swap16(t, lane) * sin

    nq = _rms(cq_ref[...], gq_ref[...]).astype(BF16)
    q = jnp.dot(nq, wq_ref[...], preferred_element_type=F32)
    scale = (QK_NOPE + QK_ROPE) ** -0.5 * LOG2E
    for h in range(C_HEADS):
        q_ref[h] = (rope(q[:, h * LANES:(h + 1) * LANES]) * scale).astype(BF16)
    nkv = _rms(ckv_ref[...], gkv_ref[...]).astype(BF16)
    k = jnp.dot(nkv, wk_ref[...], preferred_element_type=F32)
    in_rope = (lane >= KR_LANE) & (lane < KR_LANE + QK_ROPE)
    kr = jnp.where(in_rope, rope(misc_ref[...]), 0.0)
    for h in range(C_HEADS):
        k_ref[:, h * LANES:(h + 1) * LANES] = (k[:, h * LANES:(h + 1) * LANES] + kr).astype(BF16)
    v_ref[...] = jnp.dot(nkv, wv_ref[...], preferred_element_type=F32).astype(BF16)


def _mla_prep(cq, ckv, misc, g_q, g_kv, wq_p, wk_p, wv_p, cc, sc):
    T = cq.shape[0]
    tm = min(512, T)
    row = lambda i: (i, 0)
    fix = lambda i: (0, 0)
    t128 = pl.BlockSpec((tm, LANES), row)
    return pl.pallas_call(
        _mla_prep_kernel,
        out_shape=[jax.ShapeDtypeStruct((C_HEADS, T, LANES), BF16),
                   jax.ShapeDtypeStruct((T, C_HEADS * LANES), BF16),
                   jax.ShapeDtypeStruct((T, C_W), BF16)],
        grid=(T // tm,),
        in_specs=[pl.BlockSpec((tm, Q_LORA), row), t128, t128,
                  pl.BlockSpec((1, Q_LORA), fix), pl.BlockSpec((1, KV_LORA), fix),
                  pl.BlockSpec((Q_LORA, C_HEADS * LANES), fix),
                  pl.BlockSpec((KV_LORA, C_HEADS * LANES), fix),
                  pl.BlockSpec((KV_LORA, C_W), fix), t128, t128],
        out_specs=[pl.BlockSpec((C_HEADS, tm, LANES), lambda i: (0, i, 0)),
                   pl.BlockSpec((tm, C_HEADS * LANES), row),
                   pl.BlockSpec((tm, C_W), row)],
        compiler_params=_params(("parallel",)),
        name="mla_prep",
    )(cq, ckv, misc, g_q, g_kv, wq_p, wk_p, wv_p, cc, sc)


def _outproj_kernel(x_ref, mod_ref, o1_ref, o2_ref, o3_ref, l1_ref, l2_ref, l3_ref,
                    ocmp_ref, oslc_ref, owin_ref, misc_ref, ge_ref, oc_ref,
                    ga_ref, gb_ref, gc_ref, w_ref, out_ref, scr):
    tm = x_ref.shape[0]

    def token_major(ref, d):
        if d == 1:
            return ref[...]
        for r in range(d):
            for p in range(A_W // LANES):
                scr[p, pl.ds(r, tm // d, stride=d), :] = ref[:, r * A_W + p * LANES:r * A_W + (p + 1) * LANES]
        return jnp.concatenate([scr[p] for p in range(A_W // LANES)], axis=1)

    o1, o2, o3 = [token_major(r, d) for r, d in zip((o1_ref, o2_ref, o3_ref), A_DILS)]
    l1, l2, l3 = [token_major(r, d) for r, d in zip((l1_ref, l2_ref, l3_ref), A_DILS)]
    mx = jnp.maximum(jnp.maximum(l1, l2), l3)
    e1, e2, e3 = jnp.exp(l1 - mx), jnp.exp(l2 - mx), jnp.exp(l3 - mx)
    oa = (e1 * o1 + e2 * o2 + e3 * o3) / (e1 + e2 + e3)

    sg = jax.nn.sigmoid(misc_ref[...])
    hi = sg.astype(BF16)
    lo = (sg - hi.astype(F32)).astype(BF16)

    def gate(br):
        return (jnp.dot(hi, ge_ref[br], preferred_element_type=F32)
                + jnp.dot(lo, ge_ref[br], preferred_element_type=F32))

    ob = gate(0) * ocmp_ref[...] + gate(1) * oslc_ref[...] + gate(2) * owin_ref[...]
    na = _rms(oa, ga_ref[...]).astype(BF16)
    nb = _rms(ob, gb_ref[...]).astype(BF16)
    nc = _rms(oc_ref[...], gc_ref[...]).astype(BF16)
    mixed = (jnp.dot(na, w_ref[0:A_W, :], preferred_element_type=F32)
             + jnp.dot(nb, w_ref[A_W:A_W + B_W, :], preferred_element_type=F32)
             + jnp.dot(nc, w_ref[A_W + B_W:, :], preferred_element_type=F32))
    out_ref[...] = x_ref[...] + mod_ref[0, 2:3, :] * mixed


def _gate_expand():
    ge = np.zeros((3, LANES, B_W), np.float32)
    for h in range(B_HEADS):
        for br in range(3):
            ge[br, h * 3 + br, h * HEAD:(h + 1) * HEAD] = 1.0
    return jnp.asarray(ge, BF16)


def _out_proj(x2, mods_l, oa_parts, lse_parts, ocmp, oslc, owin, misc, oc, g_a, g_b, g_c, w_out_b, S):
    T, D = x2.shape
    tm = min(512, S)
    spb = S // tm
    row = lambda i: (i, 0)
    fix = lambda i: (0, 0)
    t128 = pl.BlockSpec((tm, LANES), row)
    t256 = pl.BlockSpec((tm, 256), row)
    t512 = pl.BlockSpec((tm, 512), row)
    dil = [pl.BlockSpec((tm // d, d * A_W), row) for d in A_DILS]
    return pl.pallas_call(
        _outproj_kernel,
        out_shape=jax.ShapeDtypeStruct((T, D), F32),
        grid=(T // tm,),
        in_specs=[pl.BlockSpec((tm, D), row),
                  pl.BlockSpec((1, 6, D), lambda i: (i // spb, 0, 0)),
                  *dil, *dil, t512, t512, t512, t128,
                  pl.BlockSpec((3, LANES, B_W), lambda i: (0, 0, 0)),
                  t256,
                  pl.BlockSpec((1, A_W), fix), pl.BlockSpec((1, B_W), fix), pl.BlockSpec((1, C_W), fix),
                  pl.BlockSpec((D, D), fix)],
        out_specs=pl.BlockSpec((tm, D), row),
        scratch_shapes=[pltpu.VMEM((A_W // LANES, tm, LANES), F32)],
        compiler_params=_params(("parallel",)),
        name="out_proj",
    )(x2, mods_l, *oa_parts, *lse_parts, ocmp, oslc, owin, misc, _gate_expand(), oc, g_a, g_b, g_c, w_out_b)


def _pack_halves(v):
    bits = lax.bitcast_convert_type(v.astype(BF16).astype(F32), U32)
    half = v.shape[1] // 2
    return (bits[:, half:] & jnp.uint32(0xFFFF0000)) | (bits[:, :half] >> 16)


def _unpack_halves(w):
    lo = lax.bitcast_convert_type(w << 16, F32)
    hi = lax.bitcast_convert_type(w & jnp.uint32(0xFFFF0000), F32)
    return jnp.concatenate([lo, hi], axis=1)


def _put_planes(ref, w):
    for c in range(ref.shape[0]):
        ref[c] = w[:, c * LANES:(c + 1) * LANES]


def _get_planes(ref):
    return jnp.concatenate([ref[c] for c in range(ref.shape[0])], axis=1)


def _route_kernel(x_ref, mod_ref, g_ref, wh_ref, wl_ref, b_ref, hp_ref, e_ref, r_ref, gt_ref, cnt_ref, carry):
    tm = x_ref.shape[0]

    @pl.when(pl.program_id(0) == 0)
    def _():
        carry[...] = jnp.zeros(carry.shape, F32)

    h = _rms(x_ref[...], g_ref[...]) * (1.0 + mod_ref[0, 4:5, :]) + mod_ref[0, 3:4, :]
    hb = h.astype(BF16)
    hlo = (h - hb.astype(F32)).astype(BF16)
    _put_planes(hp_ref, _pack_halves(h))

    logits = (jnp.dot(hb, wh_ref[...], preferred_element_type=F32)
              + jnp.dot(hlo, wh_ref[...], preferred_element_type=F32)
              + jnp.dot(hb, wl_ref[...], preferred_element_type=F32)) + b_ref[...]
    lane = lax.broadcasted_iota(I32, (tm, LANES), 1)
    lane_f = lane.astype(F32)
    vals, idxs, picks = [], [], []
    lg = logits
    for _ in range(TOP_K):
        mx = lg.max(axis=-1, keepdims=True)
        idx = jnp.where(lg == mx, lane_f, float(LANES)).min(axis=-1, keepdims=True)
        pick = lane_f == idx
        vals.append(mx)
        idxs.append(idx.astype(I32))
        picks.append(pick)
        lg = jnp.where(pick, LOWEST, lg)
    es = [jnp.exp(v - vals[0]) for v in vals]
    den = es[0] + es[1] + es[2] + es[3]
    onehot = jnp.zeros((tm, LANES), F32)
    for pk in picks:
        onehot = jnp.where(pk, 1.0, onehot)
    tri = (lax.broadcasted_iota(I32, (tm, tm), 0) >= lax.broadcasted_iota(I32, (tm, tm), 1)).astype(BF16)
    cs = jnp.dot(tri, onehot.astype(BF16), preferred_element_type=F32)
    rank = carry[...] + cs - 1.0
    e_out = jnp.zeros((tm, LANES), I32)
    r_out = jnp.zeros((tm, LANES), I32)
    g_out = jnp.zeros((tm, LANES), F32)
    for k in range(TOP_K):
        rk = jnp.where(picks[k], rank, 0.0).sum(axis=-1, keepdims=True)
        e_out = jnp.where(lane == k, idxs[k], e_out)
        r_out = jnp.where(lane == k, rk.astype(I32), r_out)
        g_out = jnp.where(lane == k, es[k] / den, g_out)
    e_ref[...] = e_out
    r_ref[...] = r_out
    gt_ref[...] = g_out
    new = carry[...] + onehot.sum(axis=0, keepdims=True)
    carry[...] = new
    cnt_ref[...] = new


def _route(x2, mods_l, g2, wr_hi, wr_lo, br_p, S):
    T, D = x2.shape
    tm = min(256, S)
    spb = S // tm
    row = lambda i: (i, 0)
    fix = lambda i: (0, 0)
    t128 = pl.BlockSpec((tm, LANES), row)
    return pl.pallas_call(
        _route_kernel,
        out_shape=[jax.ShapeDtypeStruct((D // 2 // LANES, T, LANES), U32),
                   jax.ShapeDtypeStruct((T, LANES), I32), jax.ShapeDtypeStruct((T, LANES), I32),
                   jax.ShapeDtypeStruct((T, LANES), F32), jax.ShapeDtypeStruct((1, LANES), F32)],
        grid=(T // tm,),
        in_specs=[pl.BlockSpec((tm, D), row),
                  pl.BlockSpec((1, 6, D), lambda i: (i // spb, 0, 0)),
                  pl.BlockSpec((1, D), fix),
                  pl.BlockSpec((D, LANES), fix), pl.BlockSpec((D, LANES), fix), pl.BlockSpec((1, LANES), fix)],
        out_specs=[pl.BlockSpec((D // 2 // LANES, tm, LANES), lambda i: (0, i, 0)), t128, t128, t128,
                   pl.BlockSpec((1, LANES), fix)],
        scratch_shapes=[pltpu.VMEM((1, LANES), F32)],
        compiler_params=_params(("arbitrary",)),
        name="moe_route",
    )(x2, mods_l, g2, wr_hi, wr_lo, br_p)


SC_GATHER_ROWS = 128


def _sc_gather_rows(x, idx, name):
    n = idx.shape[0]
    W = x.shape[1]
    mesh = plsc.VectorSubcoreMesh(core_axis_name="core", subcore_axis_name="subcore")

    def body(x_hbm, i_hbm, o_hbm):
        def step(i_vmem, o_vmem):
            pltpu.sync_copy(x_hbm.at[i_vmem.at[0]], o_vmem)

        pltpu.emit_pipeline(
            step,
            grid=(n // SC_GATHER_ROWS,),
            in_specs=[pl.BlockSpec((1, SC_GATHER_ROWS), lambda i: (0, i))],
            out_specs=[pl.BlockSpec((SC_GATHER_ROWS, W), lambda i: (i, 0))],
            core_axis_name=("core", "subcore"),
            dimension_semantics=(pltpu.PARALLEL,),
        )(i_hbm, o_hbm)

    return pl.kernel(body, out_type=jax.ShapeDtypeStruct((n, W), x.dtype), mesh=mesh,
                     scratch_types=[], name=name)(x, idx.reshape(1, n))


def _sc_scatter_rows(x, idx, name):
    n, W = x.shape
    mesh = plsc.VectorSubcoreMesh(core_axis_name="core", subcore_axis_name="subcore")

    def body(x_hbm, i_hbm, o_hbm):
        def step(x_vmem, i_vmem):
            pltpu.sync_copy(x_vmem, o_hbm.at[i_vmem.at[0]])

        pltpu.emit_pipeline(
            step,
            grid=(n // SC_GATHER_ROWS,),
            in_specs=[pl.BlockSpec((SC_GATHER_ROWS, W), lambda i: (i, 0)),
                      pl.BlockSpec((1, SC_GATHER_ROWS), lambda i: (0, i))],
            out_specs=[],
            core_axis_name=("core", "subcore"),
            dimension_semantics=(pltpu.PARALLEL,),
        )(x_hbm, i_hbm)

    return pl.kernel(body, out_type=jax.ShapeDtypeStruct((n, W), x.dtype), mesh=mesh,
                     scratch_types=[], name=name)(x, idx.reshape(1, n))


SPLIT_COLS = 256


def _ffn_kernel(te_ref, nv_ref, chg_ref, x_ref, w1_ref, b1_ref, w2_ref, b2_ref, perm_ref, y_ref, w1b, w2b):
    j = pl.program_id(0)

    @pl.when(chg_ref[j] == 1)
    def _():
        half = SPLIT_COLS // 2
        for b in range(2 * D_FF // SPLIT_COLS):
            t = jnp.dot(w1_ref[:, b * SPLIT_COLS:(b + 1) * SPLIT_COLS].astype(BF16), perm_ref[...],
                        preferred_element_type=F32)
            w1b[:, b * half:(b + 1) * half] = t[:, :half].astype(BF16)
            w1b[:, D_FF + b * half:D_FF + (b + 1) * half] = t[:, half:].astype(BF16)
        w2b[...] = w2_ref[...].astype(BF16)

    @pl.when(j >= nv_ref[0])
    def _():
        y_ref[...] = jnp.zeros(y_ref.shape, U32)

    @pl.when(j < nv_ref[0])
    def _():
        xb = _unpack_halves(_get_planes(x_ref)).astype(BF16)
        u = jnp.dot(xb, w1b[...], preferred_element_type=F32) + b1_ref[0]
        glu = jnp.minimum(u[:, :D_FF], SWIGLU_LIMIT)
        lin = jnp.clip(u[:, D_FF:], -SWIGLU_LIMIT, SWIGLU_LIMIT)
        a = glu * jax.nn.sigmoid(SWIGLU_ALPHA * glu) * (lin + 1.0)
        y = jnp.dot(a.astype(BF16), w2b[...], preferred_element_type=F32) + b2_ref[0]
        _put_planes(y_ref, _pack_halves(y))


def _split_perm():
    p = np.zeros((SPLIT_COLS, SPLIT_COLS), np.float32)
    i = np.arange(SPLIT_COLS // 2)
    p[2 * i, i] = 1.0
    p[2 * i + 1, SPLIT_COLS // 2 + i] = 1.0
    return jnp.asarray(p, BF16)


def _expert_ffn(tile_e, n_valid, changed, xs, w1, b1p, w2, b2, tmf, layer):
    P, n_rows, _ = xs.shape
    D = 2 * P * LANES
    ntiles = n_rows // tmf
    return pl.pallas_call(
        _ffn_kernel,
        out_shape=jax.ShapeDtypeStruct((P, n_rows, LANES), U32),
        grid_spec=pltpu.PrefetchScalarGridSpec(
            num_scalar_prefetch=3,
            grid=(ntiles,),
            in_specs=[pl.BlockSpec((P, tmf, LANES), lambda j, te, nv, ch: (0, jnp.minimum(j, nv[0] - 1), 0)),
                      pl.BlockSpec((None, None, D, 2 * D_FF), lambda j, te, nv, ch: (layer, te[j], 0, 0)),
                      pl.BlockSpec((1, 1, 2 * D_FF), lambda j, te, nv, ch: (te[j], 0, 0)),
                      pl.BlockSpec((None, None, D_FF, D), lambda j, te, nv, ch: (layer, te[j], 0, 0)),
                      pl.BlockSpec((1, 1, D), lambda j, te, nv, ch: (te[j], 0, 0)),
                      pl.BlockSpec((SPLIT_COLS, SPLIT_COLS), lambda j, te, nv, ch: (0, 0))],
            out_specs=pl.BlockSpec((P, tmf, LANES), lambda j, te, nv, ch: (0, j, 0)),
            scratch_shapes=[pltpu.VMEM((D, 2 * D_FF), BF16), pltpu.VMEM((D_FF, D), BF16)]),
        compiler_params=_params(("arbitrary",)),
        name="moe_ffn",
    )(tile_e, n_valid, changed, xs, w1, b1p, w2, b2, _split_perm())


def _combine_kernel(yg_ref, gt_ref, x_ref, mod_ref, o_ref):
    g = gt_ref[...]
    y = None
    for k in range(TOP_K):
        yk = _unpack_halves(jnp.concatenate([yg_ref[c, k] for c in range(yg_ref.shape[0])], axis=1))
        y = g[:, k:k + 1] * yk if y is None else y + g[:, k:k + 1] * yk
    o_ref[...] = x_ref[...] + mod_ref[0, 5:6, :] * y


def _combine(yg, gates, x2, mods_l, S):
    T, D = x2.shape
    P = yg.shape[0]
    tc = min(512, S)
    spb = S // tc
    return pl.pallas_call(
        _combine_kernel,
        out_shape=jax.ShapeDtypeStruct((T, D), F32),
        grid=(T // tc,),
        in_specs=[pl.BlockSpec((P, TOP_K, tc, LANES), lambda i: (0, 0, i, 0)),
                  pl.BlockSpec((tc, LANES), lambda i: (i, 0)),
                  pl.BlockSpec((tc, D), lambda i: (i, 0)),
                  pl.BlockSpec((1, 6, D), lambda i: (i // spb, 0, 0))],
        out_specs=pl.BlockSpec((tc, D), lambda i: (i, 0)),
        compiler_params=_params(("parallel",)),
        name="moe_combine",
    )(yg, gates, x2, mods_l)


def _moe(x2, mods_l, g2, w_router, b_router, w1, b1p, w2, b2, S, layer):
    T, D = x2.shape
    tmf = 512
    wr = jnp.zeros((D, LANES), F32).at[:, :N_EXPERTS].set(w_router)
    wr_hi = wr.astype(BF16)
    wr_lo = (wr - wr_hi.astype(F32)).astype(BF16)
    br_p = jnp.full((1, LANES), NEG, F32).at[0, :N_EXPERTS].set(b_router)
    hp, e_idx, ranks, gates, cnt = _route(x2, mods_l, g2, wr_hi, wr_lo, br_p, S)

    counts = cnt[0, :N_EXPERTS].astype(I32)
    padded = ((counts + tmf - 1) // tmf) * tmf
    ends = jnp.cumsum(padded)
    offs = ends - padded
    dest = offs[e_idx[:, :TOP_K]] + ranks[:, :TOP_K]
    n_rows = TOP_K * T + N_EXPERTS * tmf
    ntiles = n_rows // tmf
    n_valid = (ends[-1] // tmf).astype(I32).reshape(1)
    tile_start = jnp.minimum(jnp.arange(ntiles, dtype=I32), n_valid[0] - 1) * tmf
    tile_e = jnp.sum(tile_start[:, None] >= ends[None, :], axis=1).astype(I32)
    n_pad = n_rows - TOP_K * T
    pad_cum = jnp.cumsum(padded - counts)
    p = jnp.arange(n_pad, dtype=I32)
    pe = jnp.sum(p[:, None] >= pad_cum[None, :], axis=1)
    pad_start = jnp.concatenate([offs + counts, ends[-1:]])
    pad_before = jnp.concatenate([jnp.zeros((1,), I32), pad_cum])
    pad_rows = pad_start[pe] + (p - pad_before[pe])
    perm = jnp.concatenate([dest.reshape(TOP_K * T), pad_rows.astype(I32)])
    vals = jnp.concatenate([jnp.arange(TOP_K * T, dtype=I32) // TOP_K, p % T])
    src = _sc_scatter_rows(jnp.broadcast_to(vals[:, None], (n_rows, LANES)), perm, "moe_invert_sc")[:, 0]
    changed = jnp.concatenate([jnp.ones((1,), I32), (tile_e[1:] != tile_e[:-1]).astype(I32)])
    P = hp.shape[0]
    plane = jnp.arange(P, dtype=I32)[:, None]
    src_rows = (src[None, :] + plane * T).reshape(P * n_rows)
    dest_rows = (dest.T.reshape(1, TOP_K * T) + plane * n_rows).reshape(P * TOP_K * T)

    xs = _sc_gather_rows(hp.reshape(P * T, LANES), src_rows, "moe_dispatch_sc").reshape(P, n_rows, LANES)
    ys = _expert_ffn(tile_e, n_valid, changed, xs, w1, b1p, w2, b2, tmf, layer)
    yg = _sc_gather_rows(ys.reshape(P * n_rows, LANES), dest_rows, "moe_collect_sc").reshape(P, TOP_K, T, LANES)
    return _combine(yg, gates, x2, mods_l, S)


def _final_kernel(x_ref, g_ref, o_ref):
    o_ref[...] = _rms(x_ref[...], g_ref[...])


def _final_norm(x2, g):
    T, D = x2.shape
    tm = min(512, T)
    return pl.pallas_call(
        _final_kernel,
        out_shape=jax.ShapeDtypeStruct((T, D), F32),
        grid=(T // tm,),
        in_specs=[pl.BlockSpec((tm, D), lambda i: (i, 0)), pl.BlockSpec((1, D), lambda i: (0, 0))],
        out_specs=pl.BlockSpec((tm, D), lambda i: (i, 0)),
        compiler_params=_params(("parallel",)),
        name="final_norm",
    )(x2, g.reshape(1, D))


def _rearrange_w_in(w):
    D = w.shape[0]
    main = w[:, :2048]
    gates = w[:, 2048:2072]
    cq = w[:, 2072:2328]
    ckv = w[:, 2328:2456]
    kr = w[:, 2456:2488]
    z = lambda n: jnp.zeros((D, n), w.dtype)
    misc = jnp.concatenate([gates, z(KR_LANE - 24), kr, z(LANES - KR_LANE - QK_ROPE)], axis=1)
    return jnp.concatenate([main, cq, ckv, misc], axis=1).astype(BF16)


def _mla_weights(w_q_up, w_kv_up):
    dq = QK_NOPE + QK_ROPE
    wq = jnp.zeros((Q_LORA, C_HEADS * LANES), F32)
    wk = jnp.zeros((KV_LORA, C_HEADS * LANES), F32)
    wv = []
    for h in range(C_HEADS):
        wq = wq.at[:, h * LANES:h * LANES + dq].set(w_q_up[:, h * dq:(h + 1) * dq])
        kv = w_kv_up[:, h * (QK_NOPE + HEAD):(h + 1) * (QK_NOPE + HEAD)]
        wk = wk.at[:, h * LANES:h * LANES + QK_NOPE].set(kv[:, :QK_NOPE])
        wv.append(kv[:, QK_NOPE:])
    return wq.astype(BF16), wk.astype(BF16), jnp.concatenate(wv, axis=1).astype(BF16)


def kernel(x, c, positions, w_ada, b_ada, g_norm1, g_norm2, w_in, nsa_pos_k, nsa_pos_v, nsa_w1_k, nsa_b1_k, nsa_w2_k, nsa_b2_k, nsa_w1_v, nsa_b1_v, nsa_w2_v, nsa_b2_v, mla_g_q, mla_g_kv, mla_w_q_up, mla_w_kv_up, g_out_a, g_out_b, g_out_c, w_out, w_router, b_router, w_exp1, b_exp1, w_exp2, b_exp2, g_final):
    B, S, D = x.shape
    T = B * S
    depth = w_ada.shape[0]
    x2 = x.reshape(T, D)

    cab, sab, cc, sc = _rope_tables(positions.reshape(T, 1))
    nch = S // CMP_STRIDE
    n_cmp = (S - CMP_LEN) // CMP_STRIDE + 1
    end_idx = np.minimum(np.arange(nch) * CMP_STRIDE + CMP_LEN - 1, S - 1)
    cos_e = cab.reshape(B, S, LANES)[:, end_idx]
    sin_e = sab.reshape(B, S, LANES)[:, end_idx]
    mods = _ada_mods(c, w_ada, b_ada).reshape(depth, B, 6, D)
    del n_cmp

    for l in range(depth):
        mods_l = mods[l]
        (qa, ka, va, qb, kcmp, vcmp, kslc, vslc, kwin, vwin, cq, ckv, misc) = _in_proj(
            x2, mods_l, g_norm1[l].reshape(1, D), _rearrange_w_in(w_in[l]), cab, sab, S)

        oa_parts, lse_parts = [], []
        for i, (window, d) in enumerate(A_PAIRS):
            o, lse = _dilated_attention(qa[i], ka[i], va[i], B, S, d, window // d)
            oa_parts.append(o)
            lse_parts.append(lse)

        kc = _compress(kcmp, nsa_pos_k[l], nsa_w1_k[l], nsa_b1_k[l], nsa_w2_k[l], nsa_b2_k[l],
                       cos_e, sin_e, B, S, True)
        vc = _compress(vcmp, nsa_pos_v[l], nsa_w1_v[l], nsa_b1_v[l], nsa_w2_v[l], nsa_b2_v[l],
                       cos_e, sin_e, B, S, False)
        kcc = jnp.concatenate([kc[:, 0, :, :HEAD], kc[:, 1, :, :HEAD]], axis=-1).astype(BF16)
        vcc = jnp.concatenate([vc[:, 0, :, :HEAD], vc[:, 1, :, :HEAD]], axis=-1).astype(BF16)
        ocmp, sel_t = _cmp_attention(qb, kcc, vcc, B, S)
        oslc = _select_attention(qb, kslc, vslc, sel_t, B, S)
        owin = _window_attention(qb, kwin, vwin, B, S)

        wq_p, wk_p, wv_p = _mla_weights(mla_w_q_up[l], mla_w_kv_up[l])
        qc, kcl, vcl = _mla_prep(cq, ckv, misc, mla_g_q[l].reshape(1, Q_LORA), mla_g_kv[l].reshape(1, KV_LORA),
                                 wq_p, wk_p, wv_p, cc, sc)
        oc = _latent_attention(qc, kcl, vcl, B, S)

        x2 = _out_proj(x2, mods_l, oa_parts, lse_parts, ocmp, oslc, owin, misc, oc,
                       g_out_a[l].reshape(1, A_W), g_out_b[l].reshape(1, B_W), g_out_c[l].reshape(1, C_W),
                       w_out[l].astype(BF16), S)

        b1r = b_exp1[l].reshape(N_EXPERTS, D_FF, 2)
        b1p = jnp.concatenate([b1r[..., 0], b1r[..., 1]], axis=-1).reshape(N_EXPERTS, 1, 2 * D_FF)
        x2 = _moe(x2, mods_l, g_norm2[l].reshape(1, D), w_router[l], b_router[l],
                  w_exp1, b1p, w_exp2, b_exp2[l].reshape(N_EXPERTS, 1, D), S, l)

    return _final_norm(x2, g_final).reshape(B, S, D)
```

```python
import functools

import numpy as np
import jax
import jax.numpy as jnp
from jax import lax
from jax.experimental import pallas as pl
from jax.experimental.pallas import tpu as pltpu
from jax.experimental.pallas import tpu_sc as plsc

F32 = jnp.float32
BF16 = jnp.bfloat16
I32 = jnp.int32
U32 = jnp.uint32

LANES = 128
HEAD = 64
NEG = -1e30
LOWEST = -3.0e38
LOG2E = 1.4426950408889634
LN2 = 0.6931471805599453
NORM_EPS = 1e-6
ROPE_THETA = 10000.0

A_HEADS, B_HEADS, B_GROUPS, C_HEADS = 4, 8, 2, 4
A_PAIRS = ((128, 1), (512, 4), (2048, 16))
A_W, B_W, C_W = 256, 512, 256
CMP_LEN, CMP_STRIDE, CMP_HIDDEN = 32, 16, 256
SLC_BLOCK, N_SELECT, WINDOW = 64, 16, 512
FORCE_BONUS = 1e4
Q_LORA, KV_LORA, QK_NOPE, QK_ROPE = 256, 128, 64, 32
N_EXPERTS, TOP_K, D_FF = 32, 4, 1024
SWIGLU_ALPHA, SWIGLU_LIMIT = 1.702, 7.0
IN_COLS = 2560
KR_LANE = 64

VMEM_LIMIT = 56 * 1024 * 1024


def _params(sem, vmem=VMEM_LIMIT):
    return pltpu.CompilerParams(dimension_semantics=sem, vmem_limit_bytes=vmem)


def _rms(x, g):
    return x * lax.rsqrt(jnp.mean(x * x, axis=-1, keepdims=True) + NORM_EPS) * g


def _rope_rows():
    inv64 = ROPE_THETA ** (-np.arange(32, dtype=np.float32) * (2.0 / 64))
    inv32 = ROPE_THETA ** (-np.arange(16, dtype=np.float32) * (2.0 / 32))
    inv_ab = np.tile(np.concatenate([inv64, inv64]), 2)
    sgn_ab = np.tile(np.concatenate([-np.ones(32), np.ones(32)]), 2)
    inv_c = np.zeros(LANES, np.float32)
    sgn_c = np.zeros(LANES, np.float32)
    inv_c[KR_LANE:KR_LANE + 16] = inv32
    inv_c[KR_LANE + 16:KR_LANE + 32] = inv32
    sgn_c[KR_LANE:KR_LANE + 16] = -1.0
    sgn_c[KR_LANE + 16:KR_LANE + 32] = 1.0
    rows = np.stack([inv_ab, sgn_ab, inv_c, sgn_c]).astype(np.float32)
    return jnp.asarray(rows)


def _rope_tab_kernel(pos_ref, rows_ref, cab_ref, sab_ref, cc_ref, sc_ref):
    pos = pos_ref[...].astype(F32)
    ang = pos * rows_ref[0:1, :]
    cab_ref[...] = jnp.cos(ang)
    sab_ref[...] = jnp.sin(ang) * rows_ref[1:2, :]
    ang = pos * rows_ref[2:3, :]
    cc_ref[...] = jnp.cos(ang)
    sc_ref[...] = jnp.sin(ang) * rows_ref[3:4, :]


def _rope_tables(pos_col):
    R = pos_col.shape[0]
    tm = min(R, 1024)
    spec = pl.BlockSpec((tm, LANES), lambda i: (i, 0))
    return pl.pallas_call(
        _rope_tab_kernel,
        out_shape=[jax.ShapeDtypeStruct((R, LANES), F32)] * 4,
        grid=(R // tm,),
        in_specs=[pl.BlockSpec((tm, 1), lambda i: (i, 0)), pl.BlockSpec((4, LANES), lambda i: (0, 0))],
        out_specs=[spec] * 4,
        compiler_params=_params(("parallel",)),
        name="rope_tables",
    )(pos_col, _rope_rows())


def _swap32(t, lane):
    return jnp.where((lane & 32) == 0, pltpu.roll(t, 96, 1), pltpu.roll(t, 32, 1))


def _swap16(t, lane):
    return jnp.where(lane < KR_LANE + 16, pltpu.roll(t, 112, 1), pltpu.roll(t, 16, 1))


def _mods_kernel(c_ref, w_ref, b_ref, o_ref):
    c = c_ref[...]
    ca = (c * jax.nn.sigmoid(c)).astype(BF16)
    o_ref[0] = jnp.dot(ca, w_ref[0].astype(BF16), preferred_element_type=F32) + b_ref[0]


def _ada_mods(c, w_ada, b_ada):
    L, D, N = w_ada.shape
    B = c.shape[0]
    tn = 1536
    return pl.pallas_call(
        _mods_kernel,
        out_shape=jax.ShapeDtypeStruct((L, B, N), F32),
        grid=(L, N // tn),
        in_specs=[pl.BlockSpec((B, D), lambda l, j: (0, 0)),
                  pl.BlockSpec((1, D, tn), lambda l, j: (l, 0, j)),
                  pl.BlockSpec((1, 1, tn), lambda l, j: (l, 0, j))],
        out_specs=pl.BlockSpec((1, B, tn), lambda l, j: (l, 0, j)),
        compiler_params=_params(("parallel", "parallel")),
        name="ada_mods",
    )(c, w_ada, b_ada.reshape(L, 1, N))


A_DILS = tuple(d for _, d in A_PAIRS)


def _inproj_kernel(x_ref, mod_ref, g_ref, w_ref, cab_ref, sab_ref, *refs):
    nd = len(A_DILS)
    qa_refs, ka_refs, va_refs = refs[0:nd], refs[nd:2 * nd], refs[2 * nd:3 * nd]
    (qb_ref, kcmp_ref, vcmp_ref, kslc_ref, vslc_ref, kwin_ref, vwin_ref, cq_ref, ckv_ref, misc_ref,
     scr) = refs[3 * nd:]
    tm = x_ref.shape[0]

    def spread(val, store):
        W = val.shape[1]
        for p in range(W // LANES):
            scr[p] = val[:, p * LANES:(p + 1) * LANES]
        for i, d in enumerate(A_DILS):
            if d == 1:
                store(i, 0, val.astype(BF16))
            else:
                for r in range(d):
                    planes = [scr[p, pl.ds(r, tm // d, stride=d), :] for p in range(W // LANES)]
                    piece = planes[0] if len(planes) == 1 else jnp.concatenate(planes, axis=1)
                    store(i, r, piece.astype(BF16))
    x = x_ref[...]
    h = (_rms(x, g_ref[...]) * (1.0 + mod_ref[0, 1:2, :]) + mod_ref[0, 0:1, :]).astype(BF16)
    cos = cab_ref[...]
    sin = sab_ref[...]
    lane = lax.broadcasted_iota(I32, (tm, LANES), 1)
    lo64 = lane < HEAD

    def proj(c0, n):
        return jnp.dot(h, w_ref[:, c0:c0 + n], preferred_element_type=F32)

    def rope(t):
        return t * cos + _swap32(t, lane) * sin

    def pad_q(h0, t, dst):
        t = rope(t) * (HEAD ** -0.5 * LOG2E)
        tr = pltpu.roll(t, HEAD, 1)
        return [jnp.where(lo64 if dst[i] == 0 else ~lo64, t if dst[i] == i else tr, 0.0) for i in range(2)]

    def put_a(arefs, width, lane0=0):
        def store(i, r, piece):
            arefs[i][:, r * A_W + lane0:r * A_W + lane0 + width] = piece
        return store

    y = proj(0, 256)
    for j in range(2):
        for i, qh in enumerate(pad_q(2 * j, y[:, 128 * j:128 * (j + 1)], (0, 1))):
            def store(di, r, piece, hd=2 * j + i):
                qa_refs[di][hd, :, r * LANES:(r + 1) * LANES] = piece

            spread(qh, store)
    y = proj(256, 256)
    spread(jnp.concatenate([rope(y[:, :128]), rope(y[:, 128:])], axis=1), put_a(ka_refs, A_W))
    spread(proj(512, 256), put_a(va_refs, A_W))
    for j in range(2):
        y = proj(768 + 256 * j, 256)
        for jj in range(2):
            for i, qh in enumerate(pad_q(4 * j + 2 * jj, y[:, 128 * jj:128 * (jj + 1)], (j, j))):
                qb_ref[4 * j + 2 * jj + i] = qh.astype(BF16)
    y = proj(1280, 256)
    kcmp_ref[...] = y[:, :128]
    vcmp_ref[...] = y[:, 128:]
    y = proj(1536, 256)
    kslc_ref[...] = rope(y[:, :128]).astype(BF16)
    vslc_ref[...] = y[:, 128:].astype(BF16)
    y = proj(1792, 256)
    kwin_ref[...] = rope(y[:, :128]).astype(BF16)
    vwin_ref[...] = y[:, 128:].astype(BF16)
    cq_ref[...] = proj(2048, 256)
    y = proj(2304, 256)
    ckv_ref[...] = y[:, :128]
    misc_ref[...] = y[:, 128:]


def _in_proj(x2, mods_l, g1, w_in_r, cab, sab, S):
    T, D = x2.shape
    tm = min(512, S)
    spb = S // tm
    row = lambda i: (i, 0)
    shp = jax.ShapeDtypeStruct
    outs = ([shp((A_HEADS, T // d, d * LANES), BF16) for d in A_DILS]
            + [shp((T // d, d * A_W), BF16) for d in A_DILS] * 2
            + [shp((B_HEADS, T, LANES), BF16),
               shp((T, LANES), F32), shp((T, LANES), F32), shp((T, LANES), BF16), shp((T, LANES), BF16),
               shp((T, LANES), BF16), shp((T, LANES), BF16),
               shp((T, Q_LORA), F32), shp((T, LANES), F32), shp((T, LANES), F32)])
    t128 = pl.BlockSpec((tm, LANES), row)
    t256 = pl.BlockSpec((tm, 256), row)
    out_specs = ([pl.BlockSpec((A_HEADS, tm // d, d * LANES), lambda i: (0, i, 0)) for d in A_DILS]
                 + [pl.BlockSpec((tm // d, d * A_W), row) for d in A_DILS] * 2
                 + [pl.BlockSpec((B_HEADS, tm, LANES), lambda i: (0, i, 0)),
                    t128, t128, t128, t128, t128, t128, t256, t128, t128])
    res = pl.pallas_call(
        _inproj_kernel,
        out_shape=outs,
        grid=(T // tm,),
        in_specs=[pl.BlockSpec((tm, D), row),
                  pl.BlockSpec((1, 6, D), lambda i: (i // spb, 0, 0)),
                  pl.BlockSpec((1, D), lambda i: (0, 0)),
                  pl.BlockSpec((D, IN_COLS), lambda i: (0, 0)),
                  t128, t128],
        out_specs=out_specs,
        scratch_shapes=[pltpu.VMEM((A_W // LANES, tm, LANES), F32)],
        compiler_params=_params(("parallel",)),
        name="in_proj",
    )(x2, mods_l, g1, w_in_r, cab, sab)
    nd = len(A_DILS)
    return (res[0:nd], res[nd:2 * nd], res[2 * nd:3 * nd]) + tuple(res[3 * nd:])


_NT = (((1,), (1,)), ((), ()))
_TN = (((0,), (0,)), ((), ()))


KEY_SUB = 128
QUERY_SUB = 256


def _online_chunk(k, v, q, keep, bias, m, l, acc):
    s = lax.dot_general(k, q, _NT, preferred_element_type=F32)
    if bias is not None:
        s = s + bias
    if keep is not None:
        s = jnp.where(keep, s, NEG)
    m_next = jnp.maximum(m, s.max(axis=0, keepdims=True))
    alpha = jnp.exp2(m - m_next)
    p = jnp.exp2(s - m_next)
    l = alpha * l + p.sum(axis=0, keepdims=True)
    acc = acc * alpha + lax.dot_general(v, p.astype(BF16), _TN, preferred_element_type=F32)
    return m_next, l, acc


def _mask_heads(s, masks, tq, fill):
    return jnp.concatenate([jnp.where(masks[h], s[:, h * tq:(h + 1) * tq], fill)
                            for h in range(len(masks))], axis=1)


def _emit_heads(o_ref, acc_t, halves, tq):
    nh = len(halves)
    lo64 = lax.broadcasted_iota(I32, (tq, LANES), 1) < HEAD
    for c in range(nh // 2):
        parts = []
        for h in (2 * c, 2 * c + 1):
            o = acc_t[:, h * tq:(h + 1) * tq].T
            if halves[h] != h % 2:
                o = pltpu.roll(o, HEAD, 1)
            parts.append(o)
        o_ref[:, c * LANES:(c + 1) * LANES] = jnp.where(lo64, parts[0], parts[1])


def _band_kernel(*refs, nblk, tq, tk, max_dist, stacks, halves, with_lse, qi_axis):
    q_ref = refs[0]
    k_refs = refs[1:1 + nblk]
    v_refs = refs[1 + nblk:1 + 2 * nblk]
    o_ref = refs[1 + 2 * nblk]
    lse_ref = refs[2 + 2 * nblk] if with_lse else None
    nh = len(halves)
    qi = pl.program_id(qi_axis)
    ks = min(KEY_SUB, tk)
    cq = min(QUERY_SUB, tq)
    rc = lax.broadcasted_iota(I32, (ks, cq), 0) - lax.broadcasted_iota(I32, (ks, cq), 1)
    head_blk = {}
    for (h0, n, blk) in stacks:
        for h in range(h0, h0 + n):
            head_blk[h] = blk
    outs, lses = [], []
    for h in range(nh):
        lanes = slice(head_blk[h] * LANES, (head_blk[h] + 1) * LANES)
        for c in range(tq // cq):
            q = q_ref[h, c * cq:(c + 1) * cq, :]
            m = jnp.full((1, cq), NEG, F32)
            l = jnp.zeros((1, cq), F32)
            acc = jnp.zeros((LANES, cq), F32)
            for i in range(nblk):
                kt = qi * (tq // tk) - (nblk - tq // tk) + i
                for kk in range(tk // ks):
                    k_min = (i - (nblk - tq // tk)) * tk + kk * ks
                    if (c * cq + cq - 1) - k_min < 0 or c * cq - (k_min + ks - 1) > max_dist:
                        continue
                    base = qi * tq + c * cq - (kt * tk + kk * ks)
                    base = jnp.where(kt >= 0, base, -(1 << 20))
                    keep = (rc <= base) & (rc >= base - max_dist)
                    m, l, acc = _online_chunk(k_refs[i][kk * ks:(kk + 1) * ks, lanes],
                                              v_refs[i][kk * ks:(kk + 1) * ks, lanes], q, keep, None, m, l, acc)
            outs.append(acc / l)
            lses.append(jnp.broadcast_to(m * LN2 + jnp.log(l), (LANES, cq)))
    _emit_heads(o_ref, jnp.concatenate(outs, axis=1), halves, tq)
    if with_lse:
        _emit_heads(lse_ref, jnp.concatenate(lses, axis=1), tuple(h % 2 for h in range(nh)), tq)


def _dilated_attention(qa, ka, va, B, S, d, max_dist):
    L = S // d
    tq = min(512, L)
    tk = 128
    nprev = -(-max_dist // tk)
    nblk = nprev + tq // tk
    q_v = qa.reshape(A_HEADS, B, L, d * LANES)
    k_v = ka.reshape(B, L, d * A_W)
    v_v = va.reshape(B, L, d * A_W)
    kv_specs = []
    for i in range(nblk):
        kv_specs.append(pl.BlockSpec(
            (None, tk, A_W),
            functools.partial(lambda b, r, qi, i: (b, jnp.maximum(qi * (tq // tk) - nprev + i, 0), r), i=i)))
    out_spec = pl.BlockSpec((None, tq, A_W), lambda b, r, qi: (b, qi, r))
    kern = functools.partial(_band_kernel, nblk=nblk, tq=tq, tk=tk, max_dist=max_dist,
                             stacks=((0, 2, 0), (2, 2, 1)), halves=(0, 1, 0, 1), with_lse=True, qi_axis=2)
    o, lse = pl.pallas_call(
        kern,
        out_shape=[jax.ShapeDtypeStruct((B, L, d * A_W), F32)] * 2,
        grid=(B, d, L // tq),
        in_specs=[pl.BlockSpec((A_HEADS, None, tq, LANES), lambda b, r, qi: (0, b, qi, r))]
                 + kv_specs + kv_specs,
        out_specs=[out_spec, out_spec],
        compiler_params=_params(("parallel", "parallel", "parallel")),
        name=f"dilated_attn_d{d}",
    )(q_v, *([k_v] * nblk), *([v_v] * nblk))
    return o.reshape(B * L, d * A_W), lse.reshape(B * L, d * A_W)


def _window_attention(qb, kwin, vwin, B, S):
    tq = min(512, S)
    tk = min(256, S)
    max_dist = WINDOW - 1
    nprev = -(-max_dist // tk)
    nblk = nprev + tq // tk
    q_v = qb.reshape(B_HEADS, B, S, LANES)
    k_v = kwin.reshape(B, S, LANES)
    v_v = vwin.reshape(B, S, LANES)
    kv_specs = [pl.BlockSpec((None, tk, LANES),
                             functools.partial(
                                 lambda b, qi, i: (b, jnp.maximum(qi * (tq // tk) - nprev + i, 0), 0), i=i))
                for i in range(nblk)]
    kern = functools.partial(_band_kernel, nblk=nblk, tq=tq, tk=tk, max_dist=max_dist,
                             stacks=((0, B_HEADS, 0),), halves=(0, 0, 0, 0, 1, 1, 1, 1),
                             with_lse=False, qi_axis=1)
    o = pl.pallas_call(
        kern,
        out_shape=jax.ShapeDtypeStruct((B, S, B_W), F32),
        grid=(B, S // tq),
        in_specs=[pl.BlockSpec((B_HEADS, None, tq, LANES), lambda b, qi: (0, b, qi, 0))]
                 + kv_specs + kv_specs,
        out_specs=pl.BlockSpec((None, tq, B_W), lambda b, qi: (b, qi, 0)),
        compiler_params=_params(("parallel", "parallel")),
        name="window_attn",
    )(q_v, *([k_v] * nblk), *([v_v] * nblk))
    return o.reshape(B * S, B_W)


def _flash_kernel(qi_ref, kj_ref, first_ref, last_ref, *refs, tq, tk, qk_groups, pv_groups, halves, select):
    if select:
        q_ref, k_ref, v_ref, sel_ref, o_ref, m_sc, l_sc, acc_sc = refs
    else:
        q_ref, k_ref, v_ref, o_ref, m_sc, l_sc, acc_sc = refs
    nh = len(halves)
    step = pl.program_id(1)
    qi = qi_ref[step]
    kj = kj_ref[step]

    @pl.when(first_ref[step] == 1)
    def _():
        m_sc[...] = jnp.full(m_sc.shape, NEG, F32)
        l_sc[...] = jnp.zeros(l_sc.shape, F32)
        acc_sc[...] = jnp.zeros(acc_sc.shape, F32)

    ks = min(KEY_SUB, tk)
    cq = min(QUERY_SUB, tq)
    rc = lax.broadcasted_iota(I32, (ks, cq), 0) - lax.broadcasted_iota(I32, (ks, cq), 1)
    head_kb, head_vb = {}, {}
    for (h0, n, kb) in qk_groups:
        for h in range(h0, h0 + n):
            head_kb[h] = kb
    for (h0, n, vb) in pv_groups:
        for h in range(h0, h0 + n):
            head_vb[h] = vb
    hpg = nh // B_GROUPS
    spb = ks // SLC_BLOCK if select else 1

    def run(on_diagonal):
        for h in range(nh):
            klanes = slice(head_kb[h] * LANES, (head_kb[h] + 1) * LANES)
            vlanes = slice(head_vb[h] * LANES, (head_vb[h] + 1) * LANES)
            for c in range(tq // cq):
                cols = slice(h * tq + c * cq, h * tq + (c + 1) * cq)
                q = q_ref[h, c * cq:(c + 1) * cq, :]
                m, l, acc = m_sc[:, cols], l_sc[:, cols], acc_sc[:, cols]
                for kk in range(tk // ks):
                    keep = None
                    if on_diagonal:
                        keep = rc <= (qi * tq + c * cq) - (kj * tk + kk * ks)
                    bias = None
                    if select:
                        rows = [jnp.broadcast_to(
                            (sel_ref[h // hpg, pl.ds((kj * (tk // ks) + kk) * spb + j, 1), c * cq:(c + 1) * cq] - 1.0)
                            * (-NEG), (SLC_BLOCK, cq)) for j in range(spb)]
                        bias = rows[0] if spb == 1 else jnp.concatenate(rows, axis=0)
                    m, l, acc = _online_chunk(k_ref[kk * ks:(kk + 1) * ks, klanes],
                                              v_ref[kk * ks:(kk + 1) * ks, vlanes], q, keep, bias, m, l, acc)
                m_sc[:, cols] = m
                l_sc[:, cols] = l
                acc_sc[:, cols] = acc

    crosses = kj * tk + (tk - 1) > qi * tq

    @pl.when(crosses)
    def _():
        run(True)

    @pl.when(jnp.logical_not(crosses))
    def _():
        run(False)

    @pl.when(last_ref[step] == 1)
    def _():
        _emit_heads(o_ref, acc_sc[...] / l_sc[...], halves, tq)


def _causal_schedule(S, tq, tk):
    qi, kj, first, last = [], [], [], []
    for i in range(S // tq):
        nk = (i * tq + tq - 1) // tk + 1
        for j in range(nk):
            qi.append(i)
            kj.append(j)
            first.append(1 if j == 0 else 0)
            last.append(1 if j == nk - 1 else 0)
    return [jnp.asarray(np.asarray(a, np.int32)) for a in (qi, kj, first, last)]


def _select_attention(qb, kslc, vslc, sel_t, B, S):
    tq = min(512, S)
    tk = min(512, S)
    n_slc = S // SLC_BLOCK
    tabs = _causal_schedule(S, tq, tk)
    nsteps = tabs[0].shape[0]
    kern = functools.partial(_flash_kernel, tq=tq, tk=tk, qk_groups=((0, B_HEADS, 0),),
                             pv_groups=((0, B_HEADS, 0),), halves=(0, 0, 0, 0, 1, 1, 1, 1), select=True)
    o = pl.pallas_call(
        kern,
        out_shape=jax.ShapeDtypeStruct((B, S, B_W), F32),
        grid_spec=pltpu.PrefetchScalarGridSpec(
            num_scalar_prefetch=4,
            grid=(B, nsteps),
            in_specs=[pl.BlockSpec((B_HEADS, None, tq, LANES), lambda b, s, qi, kj, f, l: (0, b, qi[s], 0)),
                      pl.BlockSpec((None, tk, LANES), lambda b, s, qi, kj, f, l: (b, kj[s], 0)),
                      pl.BlockSpec((None, tk, LANES), lambda b, s, qi, kj, f, l: (b, kj[s], 0)),
                      pl.BlockSpec((None, B_GROUPS, n_slc, tq), lambda b, s, qi, kj, f, l: (b, 0, 0, qi[s]))],
            out_specs=pl.BlockSpec((None, tq, B_W), lambda b, s, qi, kj, f, l: (b, qi[s], 0)),
            scratch_shapes=[pltpu.VMEM((1, B_HEADS * tq), F32), pltpu.VMEM((1, B_HEADS * tq), F32),
                            pltpu.VMEM((LANES, B_HEADS * tq), F32)]),
        compiler_params=_params(("parallel", "arbitrary")),
        name="select_attn",
    )(*tabs, qb.reshape(B_HEADS, B, S, LANES), kslc.reshape(B, S, LANES), vslc.reshape(B, S, LANES), sel_t)
    return o.reshape(B * S, B_W)


def _latent_attention(qc, kc, vc, B, S):
    tq = min(512, S)
    tk = min(512, S)
    tabs = _causal_schedule(S, tq, tk)
    nsteps = tabs[0].shape[0]
    kern = functools.partial(_flash_kernel, tq=tq, tk=tk,
                             qk_groups=tuple((h, 1, h) for h in range(C_HEADS)),
                             pv_groups=((0, 2, 0), (2, 2, 1)), halves=(0, 1, 0, 1), select=False)
    o = pl.pallas_call(
        kern,
        out_shape=jax.ShapeDtypeStruct((B, S, C_W), F32),
        grid_spec=pltpu.PrefetchScalarGridSpec(
            num_scalar_prefetch=4,
            grid=(B, nsteps),
            in_specs=[pl.BlockSpec((C_HEADS, None, tq, LANES), lambda b, s, qi, kj, f, l: (0, b, qi[s], 0)),
                      pl.BlockSpec((None, tk, C_HEADS * LANES), lambda b, s, qi, kj, f, l: (b, kj[s], 0)),
                      pl.BlockSpec((None, tk, C_W), lambda b, s, qi, kj, f, l: (b, kj[s], 0))],
            out_specs=pl.BlockSpec((None, tq, C_W), lambda b, s, qi, kj, f, l: (b, qi[s], 0)),
            scratch_shapes=[pltpu.VMEM((1, C_HEADS * tq), F32), pltpu.VMEM((1, C_HEADS * tq), F32),
                            pltpu.VMEM((LANES, C_HEADS * tq), F32)]),
        compiler_params=_params(("parallel", "arbitrary")),
        name="latent_attn",
    )(*tabs, qc.reshape(C_HEADS, B, S, LANES), kc.reshape(B, S, C_HEADS * LANES), vc.reshape(B, S, C_W))
    return o.reshape(B * S, C_W)


def _gelu_tanh(x):
    return 0.5 * x * (1.0 + jnp.tanh(0.7978845608028654 * (x + 0.044715 * x * x * x)))


def _compress_kernel(u_ref, pos_ref, w1_ref, b1_ref, w2_ref, b2_ref, cos_ref, sin_ref, o_ref, *, rotate):
    n = u_ref.shape[0]
    half = CMP_STRIDE * HEAD
    u = u_ref[...]
    ua = (u + pos_ref[:, :half]).astype(BF16)
    ub = (u + pos_ref[:, half:]).astype(BF16)
    pa = jnp.dot(ua, w1_ref[:half, :].astype(BF16), preferred_element_type=F32)
    pb = jnp.dot(ub, w1_ref[half:, :].astype(BF16), preferred_element_type=F32)
    hid = _gelu_tanh(pa + pltpu.roll(pb, n - 1, 0) + b1_ref[...])
    y = jnp.dot(hid.astype(BF16), w2_ref[...].astype(BF16), preferred_element_type=F32) + b2_ref[...]
    if rotate:
        lane = lax.broadcasted_iota(I32, y.shape, 1)
        y = y * cos_ref[...] + _swap32(y, lane) * sin_ref[...]
    o_ref[...] = y


def _compress(t, pos_emb, w1, b1, w2, b2, cos_e, sin_e, B, S, rotate):
    nch = S // CMP_STRIDE
    u = t.reshape(B, S, B_GROUPS, HEAD).transpose(0, 2, 1, 3).reshape(B, B_GROUPS, nch, CMP_STRIDE * HEAD)
    w2d = jnp.concatenate([w2, w2], axis=1)
    b2d = jnp.concatenate([b2, b2]).reshape(1, LANES)
    kern = functools.partial(_compress_kernel, rotate=rotate)
    tab = pl.BlockSpec((None, nch, LANES), lambda b, g: (b, 0, 0))
    return pl.pallas_call(
        kern,
        out_shape=jax.ShapeDtypeStruct((B, B_GROUPS, nch, LANES), F32),
        grid=(B, B_GROUPS),
        in_specs=[pl.BlockSpec((None, None, nch, CMP_STRIDE * HEAD), lambda b, g: (b, g, 0, 0)),
                  pl.BlockSpec((1, CMP_LEN * HEAD), lambda b, g: (0, 0)),
                  pl.BlockSpec((CMP_LEN * HEAD, CMP_HIDDEN), lambda b, g: (0, 0)),
                  pl.BlockSpec((1, CMP_HIDDEN), lambda b, g: (0, 0)),
                  pl.BlockSpec((CMP_HIDDEN, LANES), lambda b, g: (0, 0)),
                  pl.BlockSpec((1, LANES), lambda b, g: (0, 0)),
                  tab, tab],
        out_specs=pl.BlockSpec((None, None, nch, LANES), lambda b, g: (b, g, 0, 0)),
        compiler_params=_params(("parallel", "parallel")),
        name="nsa_compress",
    )(u, pos_emb.reshape(1, CMP_LEN * HEAD), w1, b1.reshape(1, CMP_HIDDEN), w2d, b2d, cos_e, sin_e)


def _cmp_attn_kernel(q_ref, kc_ref, vc_ref, ov_ref, o_ref, sel_ref, *, tq, n_cmp, n_slc):
    qi = pl.program_id(1)
    nk = kc_ref.shape[0]
    q = q_ref[...].reshape(B_HEADS * tq, LANES)
    s = lax.dot_general(kc_ref[...], q, _NT, preferred_element_type=F32)
    t_idx = qi * tq + lax.broadcasted_iota(I32, (nk, tq), 1)
    c_idx = lax.broadcasted_iota(I32, (nk, tq), 0)
    mask = ((c_idx * CMP_STRIDE + (CMP_LEN - 1)) <= t_idx) & (c_idx < n_cmp)
    masks = [mask] * B_HEADS
    s = _mask_heads(s, masks, tq, NEG)
    m = s.max(axis=0, keepdims=True)
    e = _mask_heads(jnp.exp2(s - m), masks, tq, 0.0)
    den = e.sum(axis=0, keepdims=True)
    p = e / jnp.where(den > 0, den, 1.0)
    o_t = lax.dot_general(vc_ref[...], p.astype(BF16), _TN, preferred_element_type=F32)
    hpg = B_HEADS // B_GROUPS
    _emit_heads(o_ref, o_t, tuple(h // hpg for h in range(B_HEADS)), tq)

    j_idx = lax.broadcasted_iota(I32, (n_slc, tq), 0)
    t_col = qi * tq + lax.broadcasted_iota(I32, (n_slc, tq), 1)
    cur = t_col // SLC_BLOCK
    forced = (j_idx == 0) | (j_idx == cur) | (j_idx == cur - 1)
    valid = j_idx * SLC_BLOCK <= t_col
    for g in range(B_GROUPS):
        ps = p[:, g * hpg * tq:(g * hpg + 1) * tq]
        for h in range(g * hpg + 1, (g + 1) * hpg):
            ps = ps + p[:, h * tq:(h + 1) * tq]
        hi = ps.astype(BF16)
        lo = (ps - hi.astype(F32)).astype(BF16)
        imp = (jnp.dot(ov_ref[...], hi, preferred_element_type=F32)
               + jnp.dot(ov_ref[...], lo, preferred_element_type=F32))
        score = jnp.where(forced, imp + FORCE_BONUS, jnp.where(valid, imp, -1.0))
        sel = jnp.zeros((n_slc, tq), F32)
        j_f = j_idx.astype(F32)
        for _ in range(min(N_SELECT, n_slc)):
            mx = score.max(axis=0, keepdims=True)
            idx = jnp.where(score == mx, j_f, float(n_slc)).min(axis=0, keepdims=True)
            pick = j_f == idx
            sel = jnp.where(pick, 1.0, sel)
            score = jnp.where(pick, LOWEST, score)
        sel_ref[g] = sel


def _cmp_attention(qb, kcc, vcc, B, S):
    tq = min(256, S)
    nk = S // CMP_STRIDE
    n_cmp = (S - CMP_LEN) // CMP_STRIDE + 1
    n_slc = S // SLC_BLOCK
    cmp_tok = np.arange(n_cmp)[:, None] * CMP_STRIDE + np.arange(CMP_LEN)[None, :]
    overlap = ((cmp_tok[:, :, None] // SLC_BLOCK) == np.arange(n_slc)[None, None, :]).mean(axis=1)
    ov_t = np.zeros((n_slc, nk), np.float32)
    ov_t[:, :n_cmp] = overlap.T
    kern = functools.partial(_cmp_attn_kernel, tq=tq, n_cmp=n_cmp, n_slc=n_slc)
    o, sel = pl.pallas_call(
        kern,
        out_shape=[jax.ShapeDtypeStruct((B, S, B_W), F32),
                   jax.ShapeDtypeStruct((B, B_GROUPS, n_slc, S), F32)],
        grid=(B, S // tq),
        in_specs=[pl.BlockSpec((B_HEADS, None, tq, LANES), lambda b, qi: (0, b, qi, 0)),
                  pl.BlockSpec((None, nk, LANES), lambda b, qi: (b, 0, 0)),
                  pl.BlockSpec((None, nk, LANES), lambda b, qi: (b, 0, 0)),
                  pl.BlockSpec((n_slc, nk), lambda b, qi: (0, 0))],
        out_specs=[pl.BlockSpec((None, tq, B_W), lambda b, qi: (b, qi, 0)),
                   pl.BlockSpec((None, B_GROUPS, n_slc, tq), lambda b, qi: (b, 0, 0, qi))],
        compiler_params=_params(("parallel", "parallel")),
        name="cmp_attn",
    )(qb.reshape(B_HEADS, B, S, LANES), kcc, vcc, jnp.asarray(ov_t, BF16))
    return o.reshape(B * S, B_W), sel


def _mla_prep_kernel(cq_ref, ckv_ref, misc_ref, gq_ref, gkv_ref, wq_ref, wk_ref, wv_ref, cc_ref, sc_ref,
                     q_ref, k_ref, v_ref):
    tm = cq_ref.shape[0]
    lane = lax.broadcasted_iota(I32, (tm, LANES), 1)
    cos = cc_ref[...]
    sin = sc_ref[...]

    def rope(t):
        return t * cos + _swap16(t, lane) * sin

    nq = _rms(cq_ref[...], gq_ref[...]).astype(BF16)
    q = jnp.dot(nq, wq_ref[...], preferred_element_type=F32)
    scale = (QK_NOPE + QK_ROPE) ** -0.5 * LOG2E
    for h in range(C_HEADS):
        q_ref[h] = (rope(q[:, h * LANES:(h + 1) * LANES]) * scale).astype(BF16)
    nkv = _rms(ckv_ref[...], gkv_ref[...]).astype(BF16)
    k = jnp.dot(nkv, wk_ref[...], preferred_element_type=F32)
    in_rope = (lane >= KR_LANE) & (lane < KR_LANE + QK_ROPE)
    kr = jnp.where(in_rope, rope(misc_ref[...]), 0.0)
    for h in range(C_HEADS):
        k_ref[:, h * LANES:(h + 1) * LANES] = (k[:, h * LANES:(h + 1) * LANES] + kr).astype(BF16)
    v_ref[...] = jnp.dot(nkv, wv_ref[...], preferred_element_type=F32).astype(BF16)


def _mla_prep(cq, ckv, misc, g_q, g_kv, wq_p, wk_p, wv_p, cc, sc):
    T = cq.shape[0]
    tm = min(512, T)
    row = lambda i: (i, 0)
    fix = lambda i: (0, 0)
    t128 = pl.BlockSpec((tm, LANES), row)
    return pl.pallas_call(
        _mla_prep_kernel,
        out_shape=[jax.ShapeDtypeStruct((C_HEADS, T, LANES), BF16),
                   jax.ShapeDtypeStruct((T, C_HEADS * LANES), BF16),
                   jax.ShapeDtypeStruct((T, C_W), BF16)],
        grid=(T // tm,),
        in_specs=[pl.BlockSpec((tm, Q_LORA), row), t128, t128,
                  pl.BlockSpec((1, Q_LORA), fix), pl.BlockSpec((1, KV_LORA), fix),
                  pl.BlockSpec((Q_LORA, C_HEADS * LANES), fix),
                  pl.BlockSpec((KV_LORA, C_HEADS * LANES), fix),
                  pl.BlockSpec((KV_LORA, C_W), fix), t128, t128],
        out_specs=[pl.BlockSpec((C_HEADS, tm, LANES), lambda i: (0, i, 0)),
                   pl.BlockSpec((tm, C_HEADS * LANES), row),
                   pl.BlockSpec((tm, C_W), row)],
        compiler_params=_params(("parallel",)),
        name="mla_prep",
    )(cq, ckv, misc, g_q, g_kv, wq_p, wk_p, wv_p, cc, sc)


def _outproj_kernel(x_ref, mod_ref, o1_ref, o2_ref, o3_ref, l1_ref, l2_ref, l3_ref,
                    ocmp_ref, oslc_ref, owin_ref, misc_ref, ge_ref, oc_ref,
                    ga_ref, gb_ref, gc_ref, w_ref, out_ref, scr):
    tm = x_ref.shape[0]

    def token_major(ref, d):
        if d == 1:
            return ref[...]
        for r in range(d):
            for p in range(A_W // LANES):
                scr[p, pl.ds(r, tm // d, stride=d), :] = ref[:, r * A_W + p * LANES:r * A_W + (p + 1) * LANES]
        return jnp.concatenate([scr[p] for p in range(A_W // LANES)], axis=1)

    o1, o2, o3 = [token_major(r, d) for r, d in zip((o1_ref, o2_ref, o3_ref), A_DILS)]
    l1, l2, l3 = [token_major(r, d) for r, d in zip((l1_ref, l2_ref, l3_ref), A_DILS)]
    mx = jnp.maximum(jnp.maximum(l1, l2), l3)
    e1, e2, e3 = jnp.exp(l1 - mx), jnp.exp(l2 - mx), jnp.exp(l3 - mx)
    oa = (e1 * o1 + e2 * o2 + e3 * o3) / (e1 + e2 + e3)

    sg = jax.nn.sigmoid(misc_ref[...])
    hi = sg.astype(BF16)
    lo = (sg - hi.astype(F32)).astype(BF16)

    def gate(br):
        return (jnp.dot(hi, ge_ref[br], preferred_element_type=F32)
                + jnp.dot(lo, ge_ref[br], preferred_element_type=F32))

    ob = gate(0) * ocmp_ref[...] + gate(1) * oslc_ref[...] + gate(2) * owin_ref[...]
    na = _rms(oa, ga_ref[...]).astype(BF16)
    nb = _rms(ob, gb_ref[...]).astype(BF16)
    nc = _rms(oc_ref[...], gc_ref[...]).astype(BF16)
    mixed = (jnp.dot(na, w_ref[0:A_W, :], preferred_element_type=F32)
             + jnp.dot(nb, w_ref[A_W:A_W + B_W, :], preferred_element_type=F32)
             + jnp.dot(nc, w_ref[A_W + B_W:, :], preferred_element_type=F32))
    out_ref[...] = x_ref[...] + mod_ref[0, 2:3, :] * mixed


def _gate_expand():
    ge = np.zeros((3, LANES, B_W), np.float32)
    for h in range(B_HEADS):
        for br in range(3):
            ge[br, h * 3 + br, h * HEAD:(h + 1) * HEAD] = 1.0
    return jnp.asarray(ge, BF16)


def _out_proj(x2, mods_l, oa_parts, lse_parts, ocmp, oslc, owin, misc, oc, g_a, g_b, g_c, w_out_b, S):
    T, D = x2.shape
    tm = min(512, S)
    spb = S // tm
    row = lambda i: (i, 0)
    fix = lambda i: (0, 0)
    t128 = pl.BlockSpec((tm, LANES), row)
    t256 = pl.BlockSpec((tm, 256), row)
    t512 = pl.BlockSpec((tm, 512), row)
    dil = [pl.BlockSpec((tm // d, d * A_W), row) for d in A_DILS]
    return pl.pallas_call(
        _outproj_kernel,
        out_shape=jax.ShapeDtypeStruct((T, D), F32),
        grid=(T // tm,),
        in_specs=[pl.BlockSpec((tm, D), row),
                  pl.BlockSpec((1, 6, D), lambda i: (i // spb, 0, 0)),
                  *dil, *dil, t512, t512, t512, t128,
                  pl.BlockSpec((3, LANES, B_W), lambda i: (0, 0, 0)),
                  t256,
                  pl.BlockSpec((1, A_W), fix), pl.BlockSpec((1, B_W), fix), pl.BlockSpec((1, C_W), fix),
                  pl.BlockSpec((D, D), fix)],
        out_specs=pl.BlockSpec((tm, D), row),
        scratch_shapes=[pltpu.VMEM((A_W // LANES, tm, LANES), F32)],
        compiler_params=_params(("parallel",)),
        name="out_proj",
    )(x2, mods_l, *oa_parts, *lse_parts, ocmp, oslc, owin, misc, _gate_expand(), oc, g_a, g_b, g_c, w_out_b)


def _pack_halves(v):
    bits = lax.bitcast_convert_type(v.astype(BF16).astype(F32), U32)
    half = v.shape[1] // 2
    return (bits[:, half:] & jnp.uint32(0xFFFF0000)) | (bits[:, :half] >> 16)


def _unpack_halves(w):
    lo = lax.bitcast_convert_type(w << 16, F32)
    hi = lax.bitcast_convert_type(w & jnp.uint32(0xFFFF0000), F32)
    return jnp.concatenate([lo, hi], axis=1)


def _put_planes(ref, w):
    for c in range(ref.shape[0]):
        ref[c] = w[:, c * LANES:(c + 1) * LANES]


def _get_planes(ref):
    return jnp.concatenate([ref[c] for c in range(ref.shape[0])], axis=1)


def _route_kernel(x_ref, mod_ref, g_ref, wh_ref, wl_ref, b_ref, hp_ref, e_ref, r_ref, gt_ref, cnt_ref, carry):
    tm = x_ref.shape[0]

    @pl.when(pl.program_id(0) == 0)
    def _():
        carry[...] = jnp.zeros(carry.shape, F32)

    h = _rms(x_ref[...], g_ref[...]) * (1.0 + mod_ref[0, 4:5, :]) + mod_ref[0, 3:4, :]
    hb = h.astype(BF16)
    hlo = (h - hb.astype(F32)).astype(BF16)
    _put_planes(hp_ref, _pack_halves(h))

    logits = (jnp.dot(hb, wh_ref[...], preferred_element_type=F32)
              + jnp.dot(hlo, wh_ref[...], preferred_element_type=F32)
              + jnp.dot(hb, wl_ref[...], preferred_element_type=F32)) + b_ref[...]
    lane = lax.broadcasted_iota(I32, (tm, LANES), 1)
    lane_f = lane.astype(F32)
    vals, idxs, picks = [], [], []
    lg = logits
    for _ in range(TOP_K):
        mx = lg.max(axis=-1, keepdims=True)
        idx = jnp.where(lg == mx, lane_f, float(LANES)).min(axis=-1, keepdims=True)
        pick = lane_f == idx
        vals.append(mx)
        idxs.append(idx.astype(I32))
        picks.append(pick)
        lg = jnp.where(pick, LOWEST, lg)
    es = [jnp.exp(v - vals[0]) for v in vals]
    den = es[0] + es[1] + es[2] + es[3]
    onehot = jnp.zeros((tm, LANES), F32)
    for pk in picks:
        onehot = jnp.where(pk, 1.0, onehot)
    tri = (lax.broadcasted_iota(I32, (tm, tm), 0) >= lax.broadcasted_iota(I32, (tm, tm), 1)).astype(BF16)
    cs = jnp.dot(tri, onehot.astype(BF16), preferred_element_type=F32)
    rank = carry[...] + cs - 1.0
    e_out = jnp.zeros((tm, LANES), I32)
    r_out = jnp.zeros((tm, LANES), I32)
    g_out = jnp.zeros((tm, LANES), F32)
    for k in range(TOP_K):
        rk = jnp.where(picks[k], rank, 0.0).sum(axis=-1, keepdims=True)
        e_out = jnp.where(lane == k, idxs[k], e_out)
        r_out = jnp.where(lane == k, rk.astype(I32), r_out)
        g_out = jnp.where(lane == k, es[k] / den, g_out)
    e_ref[...] = e_out
    r_ref[...] = r_out
    gt_ref[...] = g_out
    new = carry[...] + onehot.sum(axis=0, keepdims=True)
    carry[...] = new
    cnt_ref[...] = new


def _route(x2, mods_l, g2, wr_hi, wr_lo, br_p, S):
    T, D = x2.shape
    tm = min(512, S)
    spb = S // tm
    row = lambda i: (i, 0)
    fix = lambda i: (0, 0)
    t128 = pl.BlockSpec((tm, LANES), row)
    return pl.pallas_call(
        _route_kernel,
        out_shape=[jax.ShapeDtypeStruct((D // 2 // LANES, T, LANES), U32),
                   jax.ShapeDtypeStruct((T, LANES), I32), jax.ShapeDtypeStruct((T, LANES), I32),
                   jax.ShapeDtypeStruct((T, LANES), F32), jax.ShapeDtypeStruct((1, LANES), F32)],
        grid=(T // tm,),
        in_specs=[pl.BlockSpec((tm, D), row),
                  pl.BlockSpec((1, 6, D), lambda i: (i // spb, 0, 0)),
                  pl.BlockSpec((1, D), fix),
                  pl.BlockSpec((D, LANES), fix), pl.BlockSpec((D, LANES), fix), pl.BlockSpec((1, LANES), fix)],
        out_specs=[pl.BlockSpec((D // 2 // LANES, tm, LANES), lambda i: (0, i, 0)), t128, t128, t128,
                   pl.BlockSpec((1, LANES), fix)],
        scratch_shapes=[pltpu.VMEM((1, LANES), F32)],
        compiler_params=_params(("arbitrary",)),
        name="moe_route",
    )(x2, mods_l, g2, wr_hi, wr_lo, br_p)


SC_GATHER_ROWS = 128


def _sc_gather_rows(x, idx, name):
    n = idx.shape[0]
    W = x.shape[1]
    mesh = plsc.VectorSubcoreMesh(core_axis_name="core", subcore_axis_name="subcore")

    def body(x_hbm, i_hbm, o_hbm):
        def step(i_vmem, o_vmem):
            pltpu.sync_copy(x_hbm.at[i_vmem.at[0]], o_vmem)

        pltpu.emit_pipeline(
            step,
            grid=(n // SC_GATHER_ROWS,),
            in_specs=[pl.BlockSpec((1, SC_GATHER_ROWS), lambda i: (0, i))],
            out_specs=[pl.BlockSpec((SC_GATHER_ROWS, W), lambda i: (i, 0))],
            core_axis_name=("core", "subcore"),
            dimension_semantics=(pltpu.PARALLEL,),
        )(i_hbm, o_hbm)

    return pl.kernel(body, out_type=jax.ShapeDtypeStruct((n, W), x.dtype), mesh=mesh,
                     scratch_types=[], name=name)(x, idx.reshape(1, n))


def _sc_scatter_rows(x, idx, name):
    n, W = x.shape
    mesh = plsc.VectorSubcoreMesh(core_axis_name="core", subcore_axis_name="subcore")

    def body(x_hbm, i_hbm, o_hbm):
        def step(x_vmem, i_vmem):
            pltpu.sync_copy(x_vmem, o_hbm.at[i_vmem.at[0]])

        pltpu.emit_pipeline(
            step,
            grid=(n // SC_GATHER_ROWS,),
            in_specs=[pl.BlockSpec((SC_GATHER_ROWS, W), lambda i: (i, 0)),
                      pl.BlockSpec((1, SC_GATHER_ROWS), lambda i: (0, i))],
            out_specs=[],
            core_axis_name=("core", "subcore"),
            dimension_semantics=(pltpu.PARALLEL,),
        )(x_hbm, i_hbm)

    return pl.kernel(body, out_type=jax.ShapeDtypeStruct((n, W), x.dtype), mesh=mesh,
                     scratch_types=[], name=name)(x, idx.reshape(1, n))


SPLIT_COLS = 256


def _ffn_kernel(te_ref, nv_ref, chg_ref, x_ref, w1_ref, b1_ref, w2_ref, b2_ref, perm_ref, y_ref, w1b, w2b):
    j = pl.program_id(0)

    @pl.when(chg_ref[j] == 1)
    def _():
        half = SPLIT_COLS // 2
        for b in range(2 * D_FF // SPLIT_COLS):
            t = jnp.dot(w1_ref[:, b * SPLIT_COLS:(b + 1) * SPLIT_COLS].astype(BF16), perm_ref[...],
                        preferred_element_type=F32)
            w1b[:, b * half:(b + 1) * half] = t[:, :half].astype(BF16)
            w1b[:, D_FF + b * half:D_FF + (b + 1) * half] = t[:, half:].astype(BF16)
        w2b[...] = w2_ref[...].astype(BF16)

    @pl.when(j >= nv_ref[0])
    def _():
        y_ref[...] = jnp.zeros(y_ref.shape, U32)

    @pl.when(j < nv_ref[0])
    def _():
        xb = _unpack_halves(_get_planes(x_ref)).astype(BF16)
        u = jnp.dot(xb, w1b[...], preferred_element_type=F32) + b1_ref[0]
        glu = jnp.minimum(u[:, :D_FF], SWIGLU_LIMIT)
        lin = jnp.clip(u[:, D_FF:], -SWIGLU_LIMIT, SWIGLU_LIMIT)
        a = glu * jax.nn.sigmoid(SWIGLU_ALPHA * glu) * (lin + 1.0)
        y = jnp.dot(a.astype(BF16), w2b[...], preferred_element_type=F32) + b2_ref[0]
        _put_planes(y_ref, _pack_halves(y))


def _split_perm():
    p = np.zeros((SPLIT_COLS, SPLIT_COLS), np.float32)
    i = np.arange(SPLIT_COLS // 2)
    p[2 * i, i] = 1.0
    p[2 * i + 1, SPLIT_COLS // 2 + i] = 1.0
    return jnp.asarray(p, BF16)


def _expert_ffn(tile_e, n_valid, changed, xs, w1, b1p, w2, b2, tmf, layer):
    P, n_rows, _ = xs.shape
    D = 2 * P * LANES
    ntiles = n_rows // tmf
    return pl.pallas_call(
        _ffn_kernel,
        out_shape=jax.ShapeDtypeStruct((P, n_rows, LANES), U32),
        grid_spec=pltpu.PrefetchScalarGridSpec(
            num_scalar_prefetch=3,
            grid=(ntiles,),
            in_specs=[pl.BlockSpec((P, tmf, LANES), lambda j, te, nv, ch: (0, jnp.minimum(j, nv[0] - 1), 0)),
                      pl.BlockSpec((None, None, D, 2 * D_FF), lambda j, te, nv, ch: (layer, te[j], 0, 0)),
                      pl.BlockSpec((1, 1, 2 * D_FF), lambda j, te, nv, ch: (te[j], 0, 0)),
                      pl.BlockSpec((None, None, D_FF, D), lambda j, te, nv, ch: (layer, te[j], 0, 0)),
                      pl.BlockSpec((1, 1, D), lambda j, te, nv, ch: (te[j], 0, 0)),
                      pl.BlockSpec((SPLIT_COLS, SPLIT_COLS), lambda j, te, nv, ch: (0, 0))],
            out_specs=pl.BlockSpec((P, tmf, LANES), lambda j, te, nv, ch: (0, j, 0)),
            scratch_shapes=[pltpu.VMEM((D, 2 * D_FF), BF16), pltpu.VMEM((D_FF, D), BF16)]),
        compiler_params=_params(("arbitrary",)),
        name="moe_ffn",
    )(tile_e, n_valid, changed, xs, w1, b1p, w2, b2, _split_perm())


def _combine_kernel(yg_ref, gt_ref, x_ref, mod_ref, o_ref):
    g = gt_ref[...]
    y = None
    for k in range(TOP_K):
        yk = _unpack_halves(jnp.concatenate([yg_ref[c, k] for c in range(yg_ref.shape[0])], axis=1))
        y = g[:, k:k + 1] * yk if y is None else y + g[:, k:k + 1] * yk
    o_ref[...] = x_ref[...] + mod_ref[0, 5:6, :] * y


def _combine(yg, gates, x2, mods_l, S):
    T, D = x2.shape
    P = yg.shape[0]
    tc = min(512, S)
    spb = S // tc
    return pl.pallas_call(
        _combine_kernel,
        out_shape=jax.ShapeDtypeStruct((T, D), F32),
        grid=(T // tc,),
        in_specs=[pl.BlockSpec((P, TOP_K, tc, LANES), lambda i: (0, 0, i, 0)),
                  pl.BlockSpec((tc, LANES), lambda i: (i, 0)),
                  pl.BlockSpec((tc, D), lambda i: (i, 0)),
                  pl.BlockSpec((1, 6, D), lambda i: (i // spb, 0, 0))],
        out_specs=pl.BlockSpec((tc, D), lambda i: (i, 0)),
        compiler_params=_params(("parallel",)),
        name="moe_combine",
    )(yg, gates, x2, mods_l)


def _moe(x2, mods_l, g2, w_router, b_router, w1, b1p, w2, b2, S, layer):
    T, D = x2.shape
    tmf = 512
    wr = jnp.zeros((D, LANES), F32).at[:, :N_EXPERTS].set(w_router)
    wr_hi = wr.astype(BF16)
    wr_lo = (wr - wr_hi.astype(F32)).astype(BF16)
    br_p = jnp.full((1, LANES), NEG, F32).at[0, :N_EXPERTS].set(b_router)
    hp, e_idx, ranks, gates, cnt = _route(x2, mods_l, g2, wr_hi, wr_lo, br_p, S)

    counts = cnt[0, :N_EXPERTS].astype(I32)
    padded = ((counts + tmf - 1) // tmf) * tmf
    ends = jnp.cumsum(padded)
    offs = ends - padded
    dest = offs[e_idx[:, :TOP_K]] + ranks[:, :TOP_K]
    n_rows = TOP_K * T + N_EXPERTS * tmf
    ntiles = n_rows // tmf
    n_valid = (ends[-1] // tmf).astype(I32).reshape(1)
    tile_start = jnp.minimum(jnp.arange(ntiles, dtype=I32), n_valid[0] - 1) * tmf
    tile_e = jnp.sum(tile_start[:, None] >= ends[None, :], axis=1).astype(I32)
    n_pad = n_rows - TOP_K * T
    pad_cum = jnp.cumsum(padded - counts)
    p = jnp.arange(n_pad, dtype=I32)
    pe = jnp.sum(p[:, None] >= pad_cum[None, :], axis=1)
    pad_start = jnp.concatenate([offs + counts, ends[-1:]])
    pad_before = jnp.concatenate([jnp.zeros((1,), I32), pad_cum])
    pad_rows = pad_start[pe] + (p - pad_before[pe])
    perm = jnp.concatenate([dest.reshape(TOP_K * T), pad_rows.astype(I32)])
    vals = jnp.concatenate([jnp.arange(TOP_K * T, dtype=I32) // TOP_K, p % T])
    src = _sc_scatter_rows(jnp.broadcast_to(vals[:, None], (n_rows, LANES)), perm, "moe_invert_sc")[:, 0]
    changed = jnp.concatenate([jnp.ones((1,), I32), (tile_e[1:] != tile_e[:-1]).astype(I32)])
    P = hp.shape[0]
    plane = jnp.arange(P, dtype=I32)[:, None]
    src_rows = (src[None, :] + plane * T).reshape(P * n_rows)
    dest_rows = (dest.T.reshape(1, TOP_K * T) + plane * n_rows).reshape(P * TOP_K * T)

    xs = _sc_gather_rows(hp.reshape(P * T, LANES), src_rows, "moe_dispatch_sc").reshape(P, n_rows, LANES)
    ys = _expert_ffn(tile_e, n_valid, changed, xs, w1, b1p, w2, b2, tmf, layer)
    yg = _sc_gather_rows(ys.reshape(P * n_rows, LANES), dest_rows, "moe_collect_sc").reshape(P, TOP_K, T, LANES)
    return _combine(yg, gates, x2, mods_l, S)


def _final_kernel(x_ref, g_ref, o_ref):
    o_ref[...] = _rms(x_ref[...], g_ref[...])


def _final_norm(x2, g):
    T, D = x2.shape
    tm = min(512, T)
    return pl.pallas_call(
        _final_kernel,
        out_shape=jax.ShapeDtypeStruct((T, D), F32),
        grid=(T // tm,),
        in_specs=[pl.BlockSpec((tm, D), lambda i: (i, 0)), pl.BlockSpec((1, D), lambda i: (0, 0))],
        out_specs=pl.BlockSpec((tm, D), lambda i: (i, 0)),
        compiler_params=_params(("parallel",)),
        name="final_norm",
    )(x2, g.reshape(1, D))


def _rearrange_w_in(w):
    D = w.shape[0]
    main = w[:, :2048]
    gates = w[:, 2048:2072]
    cq = w[:, 2072:2328]
    ckv = w[:, 2328:2456]
    kr = w[:, 2456:2488]
    z = lambda n: jnp.zeros((D, n), w.dtype)
    misc = jnp.concatenate([gates, z(KR_LANE - 24), kr, z(LANES - KR_LANE - QK_ROPE)], axis=1)
    return jnp.concatenate([main, cq, ckv, misc], axis=1).astype(BF16)


def _mla_weights(w_q_up, w_kv_up):
    dq = QK_NOPE + QK_ROPE
    wq = jnp.zeros((Q_LORA, C_HEADS * LANES), F32)
    wk = jnp.zeros((KV_LORA, C_HEADS * LANES), F32)
    wv = []
    for h in range(C_HEADS):
        wq = wq.at[:, h * LANES:h * LANES + dq].set(w_q_up[:, h * dq:(h + 1) * dq])
        kv = w_kv_up[:, h * (QK_NOPE + HEAD):(h + 1) * (QK_NOPE + HEAD)]
        wk = wk.at[:, h * LANES:h * LANES + QK_NOPE].set(kv[:, :QK_NOPE])
        wv.append(kv[:, QK_NOPE:])
    return wq.astype(BF16), wk.astype(BF16), jnp.concatenate(wv, axis=1).astype(BF16)


def kernel(x, c, positions, w_ada, b_ada, g_norm1, g_norm2, w_in, nsa_pos_k, nsa_pos_v, nsa_w1_k, nsa_b1_k, nsa_w2_k, nsa_b2_k, nsa_w1_v, nsa_b1_v, nsa_w2_v, nsa_b2_v, mla_g_q, mla_g_kv, mla_w_q_up, mla_w_kv_up, g_out_a, g_out_b, g_out_c, w_out, w_router, b_router, w_exp1, b_exp1, w_exp2, b_exp2, g_final):
    B, S, D = x.shape
    T = B * S
    depth = w_ada.shape[0]
    x2 = x.reshape(T, D)

    cab, sab, cc, sc = _rope_tables(positions.reshape(T, 1))
    nch = S // CMP_STRIDE
    n_cmp = (S - CMP_LEN) // CMP_STRIDE + 1
    end_idx = np.minimum(np.arange(nch) * CMP_STRIDE + CMP_LEN - 1, S - 1)
    cos_e = cab.reshape(B, S, LANES)[:, end_idx]
    sin_e = sab.reshape(B, S, LANES)[:, end_idx]
    mods = _ada_mods(c, w_ada, b_ada).reshape(depth, B, 6, D)
    del n_cmp

    for l in range(depth):
        mods_l = mods[l]
        (qa, ka, va, qb, kcmp, vcmp, kslc, vslc, kwin, vwin, cq, ckv, misc) = _in_proj(
            x2, mods_l, g_norm1[l].reshape(1, D), _rearrange_w_in(w_in[l]), cab, sab, S)

        oa_parts, lse_parts = [], []
        for i, (window, d) in enumerate(A_PAIRS):
            o, lse = _dilated_attention(qa[i], ka[i], va[i], B, S, d, window // d)
            oa_parts.append(o)
            lse_parts.append(lse)

        kc = _compress(kcmp, nsa_pos_k[l], nsa_w1_k[l], nsa_b1_k[l], nsa_w2_k[l], nsa_b2_k[l],
                       cos_e, sin_e, B, S, True)
        vc = _compress(vcmp, nsa_pos_v[l], nsa_w1_v[l], nsa_b1_v[l], nsa_w2_v[l], nsa_b2_v[l],
                       cos_e, sin_e, B, S, False)
        kcc = jnp.concatenate([kc[:, 0, :, :HEAD], kc[:, 1, :, :HEAD]], axis=-1).astype(BF16)
        vcc = jnp.concatenate([vc[:, 0, :, :HEAD], vc[:, 1, :, :HEAD]], axis=-1).astype(BF16)
        ocmp, sel_t = _cmp_attention(qb, kcc, vcc, B, S)
        oslc = _select_attention(qb, kslc, vslc, sel_t, B, S)
        owin = _window_attention(qb, kwin, vwin, B, S)

        wq_p, wk_p, wv_p = _mla_weights(mla_w_q_up[l], mla_w_kv_up[l])
        qc, kcl, vcl = _mla_prep(cq, ckv, misc, mla_g_q[l].reshape(1, Q_LORA), mla_g_kv[l].reshape(1, KV_LORA),
                                 wq_p, wk_p, wv_p, cc, sc)
        oc = _latent_attention(qc, kcl, vcl, B, S)

        x2 = _out_proj(x2, mods_l, oa_parts, lse_parts, ocmp, oslc, owin, misc, oc,
                       g_out_a[l].reshape(1, A_W), g_out_b[l].reshape(1, B_W), g_out_c[l].reshape(1, C_W),
                       w_out[l].astype(BF16), S)

        b1r = b_exp1[l].reshape(N_EXPERTS, D_FF, 2)
        b1p = jnp.concatenate([b1r[..., 0], b1r[..., 1]], axis=-1).reshape(N_EXPERTS, 1, 2 * D_FF)
        x2 = _moe(x2, mods_l, g_norm2[l].reshape(1, D), w_router[l], b_router[l],
                  w_exp1, b1p, w_exp2, b_exp2[l].reshape(N_EXPERTS, 1, D), S, l)

    return _final_norm(x2, g_final).reshape(B, S, D)
```
